```python
import jax
import jax.numpy as jnp
from jax import lax
import numpy as np

D_MODEL = 2048
BATCH = 2
SEQ = 16384
DEPTH = 4

CTX_LEN = 256
GRID_W = 64
N_MIXERS = 4
NORM_EPS = 1e-6
ROPE_THETA = 10000.0
FFN_HIDDEN = ((8 * D_MODEL + 3 * 256 - 1) // (3 * 256)) * 256

LRU_WIDTH = D_MODEL
LRU_BLOCKS = 16
LRU_BLOCK = LRU_WIDTH // LRU_BLOCKS
LRU_C = 8.0
CONV_WIDTH = 4
CONV_PAD_LEFT = 2
CONV_PAD_RIGHT = CONV_WIDTH - 1 - CONV_PAD_LEFT

HGRN_HEAD_DIM = 128
HGRN_HEADS = D_MODEL // HGRN_HEAD_DIM
HGRN_WIDTH = HGRN_HEADS * HGRN_HEAD_DIM
GLA_CHUNK = 64

MLA_HEADS = 16
MLA_Q_RANK = 512
MLA_KV_RANK = 512
MLA_NOPE = 128
MLA_ROPE = 64
MLA_V = 128
MLA_SCALE = (MLA_NOPE + MLA_ROPE) ** -0.5
Q_BLOCK = 128

SWA_Q_HEADS = 16
SWA_KV_HEADS = 4
SWA_GROUP = SWA_Q_HEADS // SWA_KV_HEADS
SWA_HEAD_DIM = 128
SWA_WINDOW = 128
SWA_BLOCK = 128
SWA_SCALE = SWA_HEAD_DIM ** -0.5

kernel_name = 'hybrid_interleaved_diffusion_trunk'


def rms_norm(x, g):
    xf = x.astype(jnp.float32)
    y = xf * lax.rsqrt(jnp.mean(xf * xf, axis=-1, keepdims=True) + NORM_EPS)
    return (y * g.astype(jnp.float32)).astype(x.dtype)


def adaln(x, g, shift, scale):
    return rms_norm(x, g) * (1.0 + scale) + shift


def swiglu(h, w_in, w_out):
    gate, up = jnp.split(h @ w_in, 2, axis=-1)
    return (jax.nn.silu(gate) * up) @ w_out


def axial_rope_tables(row, col, rot_dim):
    n_freq = rot_dim // 4
    inv = ROPE_THETA ** (-jnp.arange(n_freq, dtype=jnp.float32) / n_freq)
    ang_r = row.astype(jnp.float32)[:, None] * inv[None, :]
    ang_c = col.astype(jnp.float32)[:, None] * inv[None, :]
    return (jnp.cos(ang_r), jnp.sin(ang_r), jnp.cos(ang_c), jnp.sin(ang_c))


def _rotate_half(x, cos, sin):
    x1, x2 = jnp.split(x, 2, axis=-1)
    return jnp.concatenate([x1 * cos - x2 * sin, x1 * sin + x2 * cos], axis=-1)


def apply_axial_rope(x, tables):
    cos_r, sin_r, cos_c, sin_c = tables
    shape = (1, cos_r.shape[0]) + (1,) * (x.ndim - 3) + (cos_r.shape[1],)
    f = lambda t: t.reshape(shape).astype(x.dtype)
    x_row, x_col = jnp.split(x, 2, axis=-1)
    return jnp.concatenate([_rotate_half(x_row, f(cos_r), f(sin_r)),
                            _rotate_half(x_col, f(cos_c), f(sin_c))], axis=-1)


def centred_depthwise_conv(u, w, b):
    y = lax.conv_general_dilated(u, w[:, None, :].astype(u.dtype), window_strides=(1,),
                                 padding=[(CONV_PAD_LEFT, CONV_PAD_RIGHT)],
                                 dimension_numbers=('NWC', 'WIO', 'NWC'),
                                 feature_group_count=u.shape[-1])
    return y + b


def rglru_coeffs(u, gate_w, gate_b, lam):
    b, n, w = u.shape
    ub = u.reshape(b, n, LRU_BLOCKS, LRU_BLOCK)
    gates = jnp.einsum('bnkc,gkcd->gbnkd', ub, gate_w).reshape(2, b, n, w) + gate_b[:, None, None, :]
    gates = gates.astype(jnp.float32)
    r = jax.nn.sigmoid(gates[0])
    i = jax.nn.sigmoid(gates[1])
    log_a = -LRU_C * r * jax.nn.softplus(-lam.astype(jnp.float32))
    a = jnp.exp(log_a)
    bx = jnp.sqrt(-jnp.expm1(2.0 * log_a)) * (i * u.astype(jnp.float32))
    return a, bx


def linear_scan(a, b, h0, reverse):
    if h0 is not None:
        edge = -1 if reverse else 0
        b = b.at[:, edge].add(a[:, edge] * h0)

    def combine(e1, e2):
        a1, b1 = e1
        a2, b2 = e2
        return a1 * a2, a2 * b1 + b2

    return lax.associative_scan(combine, (a, b), axis=1, reverse=reverse)[1]


def rglru_mixer(h_ctx, h_lat, w_in, conv_w, conv_b, gate_w, gate_b, lam, w_out, need_ctx):
    def branches(h):
        gate_in, rec_in = jnp.split(h @ w_in, 2, axis=-1)
        return gate_in, centred_depthwise_conv(rec_in, conv_w, conv_b)

    g_ctx, u_ctx = branches(h_ctx)
    g_lat, u_lat = branches(h_lat)
    ys_ctx, ys_lat = [], []
    for d, reverse in enumerate((False, True)):
        a, bx = rglru_coeffs(u_ctx, gate_w[d], gate_b[d], lam[d])
        hc = linear_scan(a, bx, None, reverse)
        h_end = hc[:, 0] if reverse else hc[:, -1]
        a, bx = rglru_coeffs(u_lat, gate_w[d], gate_b[d], lam[d])
        ys_lat.append(linear_scan(a, bx, h_end, reverse))
        ys_ctx.append(hc)
    y_lat = (jax.nn.gelu(g_lat) * (ys_lat[0] + ys_lat[1]).astype(g_lat.dtype)) @ w_out
    y_ctx = None
    if need_ctx:
        y_ctx = (jax.nn.gelu(g_ctx) * (ys_ctx[0] + ys_ctx[1]).astype(g_ctx.dtype)) @ w_out
    return y_ctx, y_lat


def hgrn_lower_bounds(logits):
    cum = jnp.cumsum(jax.nn.softmax(logits.astype(jnp.float32), axis=0), axis=0)
    return cum - cum[:1]


def gla_chunk_scan(q, k, v, log_f, s0):
    b, n, h, _ = q.shape
    nc = n // GLA_CHUNK
    to_chunks = lambda t: t.reshape(b, nc, GLA_CHUNK, h, t.shape[-1]).transpose(1, 0, 3, 2, 4)
    tri = jnp.tril(jnp.ones((GLA_CHUNK, GLA_CHUNK), dtype=bool))

    def step(state, blk):
        qc, kc, vc, gc = blk
        cum = jnp.cumsum(gc, axis=2)
        rel = jnp.where(tri[:, :, None], cum[:, :, :, None, :] - cum[:, :, None, :, :], -jnp.inf)
        scores = jnp.einsum('bhtd,bhsd,bhtsd->bhts', qc, kc, jnp.exp(rel))
        o = (jnp.einsum('bhts,bhse->bhte', scores, vc)
             + jnp.einsum('bhtd,bhde->bhte', qc * jnp.exp(cum), state))
        last = cum[:, :, -1:, :]
        new_state = (jnp.exp(last)[:, :, 0, :, None] * state
                     + jnp.einsum('bhsd,bhse->bhde', kc * jnp.exp(last - cum), vc))
        return new_state, o

    state, o = lax.scan(step, s0, (to_chunks(q), to_chunks(k), to_chunks(v), to_chunks(log_f)))
    return o.transpose(1, 0, 3, 2, 4).reshape(b, n, h, v.shape[-1]), state


def hgrn2_mixer(h_ctx, h_lat, w_in, lb, gnorm_g, w_out, need_ctx):
    def project(h):
        b, n, _ = h.shape
        q, f_fw, f_bw, i, g = jnp.split(h @ w_in, 5, axis=-1)
        heads = lambda t: t.reshape(b, n, HGRN_HEADS, HGRN_HEAD_DIM).astype(jnp.float32)
        forget = []
        for d, z in enumerate((f_fw, f_bw)):
            lbd = lb[d].reshape(HGRN_HEADS, HGRN_HEAD_DIM)
            forget.append(lbd + (1.0 - lbd) * jax.nn.sigmoid(heads(z)))
        return jax.nn.silu(heads(q)), forget, heads(i), g

    q_c, f_c, i_c, g_c = project(h_ctx)
    q_l, f_l, i_l, g_l = project(h_lat)
    s0 = jnp.zeros((h_lat.shape[0], HGRN_HEADS, HGRN_HEAD_DIM, HGRN_HEAD_DIM), jnp.float32)
    outs_ctx, outs_lat = [], []
    for d in range(2):
        orient = (lambda t: jnp.flip(t, axis=1)) if d == 1 else (lambda t: t)
        o_c, s_c = gla_chunk_scan(orient(q_c), orient(1.0 - f_c[d]), orient(i_c), orient(jnp.log(f_c[d])), s0)
        o_l, _ = gla_chunk_scan(orient(q_l), orient(1.0 - f_l[d]), orient(i_l), orient(jnp.log(f_l[d])), s_c)
        outs_ctx.append(orient(o_c))
        outs_lat.append(orient(o_l))

    def readout(o, g):
        b, n = o.shape[:2]
        o = rms_norm(o, gnorm_g).reshape(b, n, HGRN_WIDTH).astype(g.dtype)
        return (o * jax.nn.silu(g)) @ w_out

    y_lat = readout(outs_lat[0] + outs_lat[1], g_l)
    y_ctx = readout(outs_ctx[0] + outs_ctx[1], g_c) if need_ctx else None
    return y_ctx, y_lat


def blocked_softmax_attention(q, k, v, scale):
    b, n, h, dq = q.shape
    nb = n // Q_BLOCK
    qb = jnp.moveaxis(q.reshape(b, nb, Q_BLOCK, h, dq), 1, 0)

    def one_block(qblk):
        s = jnp.einsum('bqhd,bkhd->bhqk', qblk, k).astype(jnp.float32) * scale
        p = jax.nn.softmax(s, axis=-1).astype(v.dtype)
        return jnp.einsum('bhqk,bkhd->bqhd', p, v)

    o = lax.map(one_block, qb)
    return jnp.moveaxis(o, 0, 1).reshape(b, n, h, v.shape[-1])


def mla_mixer(h_ctx, h_lat, w_in, q_norm_g, kv_norm_g, w_uq, w_ukv, qk_g, w_out, rope_tab, need_ctx):
    def qkv(h, tab):
        b, n, _ = h.shape
        cq, ckv, k_rope = jnp.split(h @ w_in, [MLA_Q_RANK, MLA_Q_RANK + MLA_KV_RANK], axis=-1)
        q = (rms_norm(cq, q_norm_g) @ w_uq).reshape(b, n, MLA_HEADS, MLA_NOPE + MLA_ROPE)
        kv = (rms_norm(ckv, kv_norm_g) @ w_ukv).reshape(b, n, MLA_HEADS, MLA_NOPE + MLA_V)
        q_nope, q_rope = jnp.split(q, [MLA_NOPE], axis=-1)
        k_nope, v = jnp.split(kv, [MLA_NOPE], axis=-1)
        k_rope = k_rope[:, :, None, :]
        q_nope = rms_norm(q_nope, qk_g[0, :MLA_NOPE])
        q_rope = rms_norm(q_rope, qk_g[0, MLA_NOPE:])
        k_nope = rms_norm(k_nope, qk_g[1, :MLA_NOPE])
        k_rope = rms_norm(k_rope, qk_g[1, MLA_NOPE:])
        if tab is not None:
            q_rope = apply_axial_rope(q_rope, tab)
            k_rope = apply_axial_rope(k_rope, tab)
        q = jnp.concatenate([q_nope, q_rope], axis=-1)
        k = jnp.concatenate([k_nope, jnp.broadcast_to(k_rope, (b, n, MLA_HEADS, MLA_ROPE))], axis=-1)
        return q, k, v

    q_c, k_c, v_c = qkv(h_ctx, None)
    q_l, k_l, v_l = qkv(h_lat, rope_tab)
    b, n = h_lat.shape[:2]
    o_l = blocked_softmax_attention(q_l, jnp.concatenate([k_c, k_l], axis=1),
                                    jnp.concatenate([v_c, v_l], axis=1), MLA_SCALE)
    y_lat = o_l.reshape(b, n, MLA_HEADS * MLA_V) @ w_out
    y_ctx = None
    if need_ctx:
        o_c = blocked_softmax_attention(q_c, k_c, v_c, MLA_SCALE)
        y_ctx = o_c.reshape(b, h_ctx.shape[1], MLA_HEADS * MLA_V) @ w_out
    return y_ctx, y_lat


def attend_with_sink(q, keys, values, masks, sink):
    b, nq = q.shape[:2]
    logits = [jnp.broadcast_to(sink.astype(jnp.float32).reshape(SWA_KV_HEADS, SWA_GROUP)[None, :, :, None, None],
                               (b, SWA_KV_HEADS, SWA_GROUP, nq, 1))]
    for k, m in zip(keys, masks):
        s = jnp.einsum('bqhgd,bkhd->bhgqk', q, k).astype(jnp.float32) * SWA_SCALE
        logits.append(s if m is None else jnp.where(m, s, -jnp.inf))
    p = jax.nn.softmax(jnp.concatenate(logits, axis=-1), axis=-1)
    out = None
    start = 1
    for k, v in zip(keys, values):
        width = k.shape[1]
        part = jnp.einsum('bhgqk,bkhd->bqhgd', p[..., start:start + width].astype(v.dtype), v)
        out = part if out is None else out + part
        start += width
    return out


def swa_mixer(h_ctx, h_lat, w_in, qk_g, sink, w_out, rope_tab, need_ctx):
    def qkv(h, tab):
        b, n, _ = h.shape
        q, k, v = jnp.split(h @ w_in, [SWA_Q_HEADS * SWA_HEAD_DIM, (SWA_Q_HEADS + SWA_KV_HEADS) * SWA_HEAD_DIM], axis=-1)
        q = rms_norm(q.reshape(b, n, SWA_KV_HEADS, SWA_GROUP, SWA_HEAD_DIM), qk_g[0])
        k = rms_norm(k.reshape(b, n, SWA_KV_HEADS, SWA_HEAD_DIM), qk_g[1])
        v = v.reshape(b, n, SWA_KV_HEADS, SWA_HEAD_DIM)
        if tab is not None:
            q = apply_axial_rope(q, tab)
            k = apply_axial_rope(k, tab)
        return q, k, v

    q_c, k_c, v_c = qkv(h_ctx, None)
    q_l, k_l, v_l = qkv(h_lat, rope_tab)
    b, n = h_lat.shape[:2]
    nb = n // SWA_BLOCK
    pad = ((0, 0), (SWA_BLOCK, SWA_BLOCK), (0, 0), (0, 0))
    k_pad, v_pad = jnp.pad(k_l, pad), jnp.pad(v_l, pad)
    q_blocks = jnp.moveaxis(q_l.reshape(b, nb, SWA_BLOCK, SWA_KV_HEADS, SWA_GROUP, SWA_HEAD_DIM), 1, 0)
    band = 3 * SWA_BLOCK
    offset = jnp.arange(SWA_BLOCK)[:, None] + SWA_BLOCK - jnp.arange(band)[None, :]
    in_window = jnp.abs(offset) <= SWA_WINDOW

    def one_block(args):
        j, q_blk = args
        start = j * SWA_BLOCK
        k_band = lax.dynamic_slice_in_dim(k_pad, start, band, axis=1)
        v_band = lax.dynamic_slice_in_dim(v_pad, start, band, axis=1)
        key_pos = start - SWA_BLOCK + jnp.arange(band)
        mask = in_window & ((key_pos >= 0) & (key_pos < n))[None, :]
        return attend_with_sink(q_blk, (k_c, k_band), (v_c, v_band), (None, mask), sink)

    o_l = lax.map(one_block, (jnp.arange(nb), q_blocks))
    y_lat = jnp.moveaxis(o_l, 0, 1).reshape(b, n, SWA_Q_HEADS * SWA_HEAD_DIM) @ w_out
    y_ctx = None
    if need_ctx:
        o_c = attend_with_sink(q_c, (k_c,), (v_c,), (None,), sink)
        y_ctx = o_c.reshape(b, h_ctx.shape[1], SWA_Q_HEADS * SWA_HEAD_DIM) @ w_out
    return y_ctx, y_lat


def setup_inputs(seed: int = 0) -> dict:
    key = jax.random.key(seed)
    keys = iter(jax.random.split(key, 48))

    def normal(shape, scale):
        return scale * jax.random.normal(next(keys), shape, jnp.float32)

    def gain(shape):
        return 1.0 + normal(shape, 0.05)

    n_a, n_b, n_c, n_d = (len(range(m, DEPTH, N_MIXERS)) for m in range(N_MIXERS))
    D = D_MODEL
    lam_u = jax.random.uniform(next(keys), (n_a, 2, LRU_WIDTH), jnp.float32, 0.9, 0.999)
    lam_p = lam_u ** (1.0 / LRU_C)
    return {
        'x': normal((BATCH, SEQ, D), 1.0),
        'c': normal((BATCH, D), 1.0),
        'ctx': normal((BATCH, CTX_LEN, D), 1.0),
        'c_ctx': normal((D,), 1.0),
        'mod_w': normal((DEPTH, D, 6 * D), 0.5 * D ** -0.5),
        'mod_b': normal((DEPTH, 6 * D), 0.1),
        'norm_g': gain((DEPTH, 2, D)),
        'ffn_w_in': normal((DEPTH, D, 2 * FFN_HIDDEN), D ** -0.5),
        'ffn_w_out': normal((DEPTH, FFN_HIDDEN, D), FFN_HIDDEN ** -0.5),
        'rglru_w_in': normal((n_a, D, 2 * LRU_WIDTH), D ** -0.5),
        'rglru_conv_w': normal((n_a, CONV_WIDTH, LRU_WIDTH), CONV_WIDTH ** -0.5),
        'rglru_conv_b': normal((n_a, LRU_WIDTH), 0.02),
        'rglru_gate_w': normal((n_a, 2, 2, LRU_BLOCKS, LRU_BLOCK, LRU_BLOCK), LRU_BLOCK ** -0.5),
        'rglru_gate_b': normal((n_a, 2, 2, LRU_WIDTH), 0.02),
        'rglru_lambda': jnp.log(lam_p) - jnp.log1p(-lam_p),
        'rglru_w_out': normal((n_a, LRU_WIDTH, D), LRU_WIDTH ** -0.5),
        'hgrn_w_in': normal((n_b, D, 5 * HGRN_WIDTH), D ** -0.5),
        'hgrn_lb_logits': normal((DEPTH, 2, HGRN_WIDTH), 0.1),
        'hgrn_gnorm_g': gain((n_b, HGRN_HEAD_DIM)),
        'hgrn_w_out': normal((n_b, HGRN_WIDTH, D), HGRN_WIDTH ** -0.5),
        'mla_w_in': normal((n_c, D, MLA_Q_RANK + MLA_KV_RANK + MLA_ROPE), D ** -0.5),
        'mla_q_norm_g': gain((n_c, MLA_Q_RANK)),
        'mla_kv_norm_g': gain((n_c, MLA_KV_RANK)),
        'mla_w_uq': normal((n_c, MLA_Q_RANK, MLA_HEADS * (MLA_NOPE + MLA_ROPE)), MLA_Q_RANK ** -0.5),
        'mla_w_ukv': normal((n_c, MLA_KV_RANK, MLA_HEADS * (MLA_NOPE + MLA_V)), MLA_KV_RANK ** -0.5),
        'mla_qk_g': gain((n_c, 2, MLA_NOPE + MLA_ROPE)),
        'mla_w_out': normal((n_c, MLA_HEADS * MLA_V, D), (MLA_HEADS * MLA_V) ** -0.5),
        'swa_w_in': normal((n_d, D, (SWA_Q_HEADS + 2 * SWA_KV_HEADS) * SWA_HEAD_DIM), D ** -0.5),
        'swa_qk_g': gain((n_d, 2, SWA_HEAD_DIM)),
        'swa_sink': normal((n_d, SWA_Q_HEADS), 0.5),
        'swa_w_out': normal((n_d, SWA_Q_HEADS * SWA_HEAD_DIM, D), (SWA_Q_HEADS * SWA_HEAD_DIM) ** -0.5),
    }


def reference(x, c, ctx, c_ctx, mod_w, mod_b, norm_g, ffn_w_in, ffn_w_out,
              rglru_w_in, rglru_conv_w, rglru_conv_b, rglru_gate_w, rglru_gate_b, rglru_lambda, rglru_w_out,
              hgrn_w_in, hgrn_lb_logits, hgrn_gnorm_g, hgrn_w_out,
              mla_w_in, mla_q_norm_g, mla_kv_norm_g, mla_w_uq, mla_w_ukv, mla_qk_g, mla_w_out,
              swa_w_in, swa_qk_g, swa_sink, swa_w_out):
    n_lat = x.shape[1]
    rows = n_lat // GRID_W
    row = jnp.repeat(jnp.arange(rows), GRID_W)
    col = jnp.tile(jnp.arange(GRID_W), rows)
    rope_mla = axial_rope_tables(row, col, MLA_ROPE)
    rope_swa = axial_rope_tables(row, col, SWA_HEAD_DIM)
    lower_bounds = hgrn_lower_bounds(hgrn_lb_logits)
    cond_lat = jax.nn.silu(c)[:, None, :]
    cond_ctx = jax.nn.silu(c_ctx)[None, None, :]
    x_lat, x_ctx = x, ctx
    for layer in range(DEPTH):
        kind, slot = layer % N_MIXERS, layer // N_MIXERS
        need_ctx = layer < DEPTH - 1
        m_lat = jnp.split(cond_lat @ mod_w[layer] + mod_b[layer], 6, axis=-1)
        m_ctx = jnp.split(cond_ctx @ mod_w[layer] + mod_b[layer], 6, axis=-1)
        h_lat = adaln(x_lat, norm_g[layer, 0], m_lat[0], m_lat[1])
        h_ctx = adaln(x_ctx, norm_g[layer, 0], m_ctx[0], m_ctx[1])
        if kind == 0:
            y_ctx, y_lat = rglru_mixer(h_ctx, h_lat, rglru_w_in[slot], rglru_conv_w[slot], rglru_conv_b[slot],
                                       rglru_gate_w[slot], rglru_gate_b[slot], rglru_lambda[slot],
                                       rglru_w_out[slot], need_ctx)
        elif kind == 1:
            y_ctx, y_lat = hgrn2_mixer(h_ctx, h_lat, hgrn_w_in[slot], lower_bounds[layer], hgrn_gnorm_g[slot],
                                       hgrn_w_out[slot], need_ctx)
        elif kind == 2:
            y_ctx, y_lat = mla_mixer(h_ctx, h_lat, mla_w_in[slot], mla_q_norm_g[slot], mla_kv_norm_g[slot],
                                     mla_w_uq[slot], mla_w_ukv[slot], mla_qk_g[slot], mla_w_out[slot],
                                     rope_mla, need_ctx)
        else:
            y_ctx, y_lat = swa_mixer(h_ctx, h_lat, swa_w_in[slot], swa_qk_g[slot], swa_sink[slot],
                                     swa_w_out[slot], rope_swa, need_ctx)
        x_lat = x_lat + m_lat[2] * y_lat
        f_lat = adaln(x_lat, norm_g[layer, 1], m_lat[3], m_lat[4])
        x_lat = x_lat + m_lat[5] * swiglu(f_lat, ffn_w_in[layer], ffn_w_out[layer])
        if need_ctx:
            x_ctx = x_ctx + m_ctx[2] * y_ctx
            f_ctx = adaln(x_ctx, norm_g[layer, 1], m_ctx[3], m_ctx[4])
            x_ctx = x_ctx + m_ctx[5] * swiglu(f_ctx, ffn_w_in[layer], ffn_w_out[layer])
    return x_lat
```

```python
import functools
import math

import jax
import jax.numpy as jnp
from jax import lax
from jax.experimental import pallas as pl
from jax.experimental.pallas import tpu as pltpu

F32 = jnp.float32
BF16 = jnp.bfloat16

DEPTH = 4
GRID_W = 64
NORM_EPS = 1e-6
ROPE_THETA = 10000.0
LOG2E = math.log2(math.e)

LRU_BLOCKS = 16
LRU_BLOCK = 128
LRU_C = 8.0
CONV_WIDTH = 4
CONV_PAD_LEFT = 2

HGRN_HEADS = 16
HGRN_HEAD_DIM = 128
GLA_CHUNK = 64

MLA_HEADS = 16
MLA_Q_RANK = 512
MLA_KV_RANK = 512
MLA_NOPE = 128
MLA_ROPE = 64
MLA_V = 128
MLA_SCALE = (MLA_NOPE + MLA_ROPE) ** -0.5
MLA_QK_PAD = 256

SWA_Q_HEADS = 16
SWA_KV_HEADS = 4
SWA_GROUP = SWA_Q_HEADS // SWA_KV_HEADS
SWA_HEAD_DIM = 128
SWA_WINDOW = 128
SWA_BLOCK = 128
SWA_SCALE = SWA_HEAD_DIM ** -0.5

LANES = 128
MIB = 1024 * 1024
MASK_VALUE = -1e30


def _cparams(semantics, vmem_mib):
    return pltpu.CompilerParams(dimension_semantics=semantics, vmem_limit_bytes=vmem_mib * MIB)


def _sigmoid(x):
    return jax.nn.sigmoid(x)


def _silu(x):
    return x * _sigmoid(x)


def _rms(x, eps=NORM_EPS):
    return x * lax.rsqrt(jnp.mean(x * x, axis=-1, keepdims=True) + eps)


def _adaln(x, g, shift, scale):
    return (_rms(x) * g) * (1.0 + scale) + shift


def _dot(a, b):
    return jnp.dot(a, b, preferred_element_type=F32)


def _dot_nt(a, b):
    return lax.dot_general(a, b, (((1,), (1,)), ((), ())), preferred_element_type=F32)


def _dot_tn(a, b):
    return lax.dot_general(a, b, (((0,), (0,)), ((), ())), preferred_element_type=F32)


def _mod_kernel(c_ref, w_ref, b_ref, o_ref):
    s = _silu(c_ref[...]).astype(BF16)
    o_ref[0] = _dot(s, w_ref[0].astype(BF16)) + b_ref[0]


def _modulation(cond, mod_w, mod_b):
    depth, d, n = mod_w.shape
    tn = 1024
    return pl.pallas_call(
        _mod_kernel,
        grid=(depth, n // tn),
        in_specs=[pl.BlockSpec((8, d), lambda l, j: (0, 0)),
                  pl.BlockSpec((1, d, tn), lambda l, j: (l, 0, j)),
                  pl.BlockSpec((1, 1, tn), lambda l, j: (l, 0, j))],
        out_specs=pl.BlockSpec((1, 8, tn), lambda l, j: (l, 0, j)),
        out_shape=jax.ShapeDtypeStruct((depth, 8, n), F32),
        compiler_params=_cparams(("parallel", "parallel"), 40),
        name="modulation",
    )(cond, mod_w, mod_b.reshape(depth, 1, n))


def _inproj_kernel(x_ref, g_ref, sh_ref, sc_ref, w_ref, o_ref, h_ref, *, head_major):
    @pl.when(pl.program_id(2) == 0)
    def _():
        h_ref[...] = _adaln(x_ref[0], g_ref[...], sh_ref[0], sc_ref[0]).astype(BF16)

    res = _dot(h_ref[...], w_ref[...])
    if head_major:
        for c in range(res.shape[1] // LANES):
            o_ref[0, c] = res[:, c * LANES:(c + 1) * LANES].astype(o_ref.dtype)
    else:
        o_ref[0] = res.astype(o_ref.dtype)


def _inproj(x, g, shift, scale, w, *, tn, head_major=False, name="inproj"):
    b, r, d = x.shape
    n = w.shape[1]
    tm = min(r, 512)
    assert r % tm == 0 and n % tn == 0
    if head_major:
        out_shape = jax.ShapeDtypeStruct((b, n // LANES, r, LANES), BF16)
        out_spec = pl.BlockSpec((1, tn // LANES, tm, LANES), lambda bi, i, j: (bi, j, i, 0))
    else:
        out_shape = jax.ShapeDtypeStruct((b, r, n), BF16)
        out_spec = pl.BlockSpec((1, tm, tn), lambda bi, i, j: (bi, i, j))
    vec = pl.BlockSpec((1, 1, d), lambda bi, i, j: (bi, 0, 0))
    return pl.pallas_call(
        functools.partial(_inproj_kernel, head_major=head_major),
        grid=(b, r // tm, n // tn),
        in_specs=[pl.BlockSpec((1, tm, d), lambda bi, i, j: (bi, i, 0)),
                  pl.BlockSpec((1, d), lambda bi, i, j: (0, 0)),
                  vec, vec,
                  pl.BlockSpec((d, tn), lambda bi, i, j: (0, j))],
        out_specs=out_spec,
        out_shape=out_shape,
        scratch_shapes=[pltpu.VMEM((tm, d), BF16)],
        compiler_params=_cparams(("parallel", "parallel", "arbitrary"), 48),
        name=name,
    )(x, g, shift, scale, w)


def _outproj_kernel(a_ref, w_ref, x_ref, gate_ref, o_ref):
    o_ref[0] = x_ref[0] + gate_ref[0] * _dot(a_ref[0], w_ref[...])


def _outproj(a, w, x, gate, name="outproj"):
    b, r, k = a.shape
    d = w.shape[1]
    tm = min(r, 512)
    return pl.pallas_call(
        _outproj_kernel,
        grid=(b, r // tm),
        in_specs=[pl.BlockSpec((1, tm, k), lambda bi, i: (bi, i, 0)),
                  pl.BlockSpec((k, d), lambda bi, i: (0, 0)),
                  pl.BlockSpec((1, tm, d), lambda bi, i: (bi, i, 0)),
                  pl.BlockSpec((1, 1, d), lambda bi, i: (bi, 0, 0))],
        out_specs=pl.BlockSpec((1, tm, d), lambda bi, i: (bi, i, 0)),
        out_shape=jax.ShapeDtypeStruct((b, r, d), F32),
        compiler_params=_cparams(("parallel", "parallel"), 48),
        name=name,
    )(a, w, x, gate)


def _ffn_kernel(x_ref, g_ref, sh_ref, sc_ref, gate_ref, wg_ref, wu_ref, wo_ref, o_ref, f_ref, acc_ref):
    j = pl.program_id(2)

    @pl.when(j == 0)
    def _():
        f_ref[...] = _adaln(x_ref[0], g_ref[...], sh_ref[0], sc_ref[0]).astype(BF16)
        acc_ref[...] = jnp.zeros_like(acc_ref)

    f = f_ref[...]
    gt = _dot(f, wg_ref[...])
    up = _dot(f, wu_ref[...])
    act = (_silu(gt) * up).astype(BF16)
    acc_ref[...] += _dot(act, wo_ref[...])

    @pl.when(j == pl.num_programs(2) - 1)
    def _():
        o_ref[0] = x_ref[0] + gate_ref[0] * acc_ref[...]


def _ffn(x, g, shift, scale, gate, w_in, w_out, name="ffn"):
    b, r, d = x.shape
    hidden = w_out.shape[0]
    tm = min(r, 512)
    tf = 512
    nf = hidden // tf
    assert hidden % tf == 0
    vec = pl.BlockSpec((1, 1, d), lambda bi, i, j: (bi, 0, 0))
    return pl.pallas_call(
        _ffn_kernel,
        grid=(b, r // tm, nf),
        in_specs=[pl.BlockSpec((1, tm, d), lambda bi, i, j: (bi, i, 0)),
                  pl.BlockSpec((1, d), lambda bi, i, j: (0, 0)),
                  vec, vec, vec,
                  pl.BlockSpec((d, tf), lambda bi, i, j: (0, j)),
                  pl.BlockSpec((d, tf), lambda bi, i, j: (0, j + nf)),
                  pl.BlockSpec((tf, d), lambda bi, i, j: (j, 0))],
        out_specs=pl.BlockSpec((1, tm, d), lambda bi, i, j: (bi, i, 0)),
        out_shape=jax.ShapeDtypeStruct((b, r, d), F32),
        scratch_shapes=[pltpu.VMEM((tm, d), BF16), pltpu.VMEM((tm, d), F32)],
        compiler_params=_cparams(("parallel", "parallel", "arbitrary"), 48),
        name=name,
    )(x, g, shift, scale, gate, w_in, w_in, w_out)


def _softplus(x):
    return jnp.maximum(x, 0.0) + jnp.log1p(jnp.exp(-jnp.abs(x)))


def _gelu_tanh(x):
    return 0.5 * x * (1.0 + jnp.tanh(math.sqrt(2.0 / math.pi) * (x + 0.044715 * (x * x * x))))


def _rglru_tile(rec, prev8, next8, wg_ref, gb_ref, lam_ref, cw_ref, cb_ref,
                ubuf, a_s, b_s, h_s, hcar, *, reverse):
    tt = rec.shape[0]
    ubuf[0:8, :] = prev8
    ubuf[8:8 + tt, :] = rec
    ubuf[8 + tt:16 + tt, :] = next8
    base = 8 - CONV_PAD_LEFT
    u = cb_ref[...] + cw_ref[0:1, :] * ubuf[base:base + tt, :]
    for j in range(1, CONV_WIDTH):
        u = u + cw_ref[j:j + 1, :] * ubuf[base + j:base + j + tt, :]
    sp = _softplus(-lam_ref[...])
    for k in range(LRU_BLOCKS):
        cs = slice(k * LRU_BLOCK, (k + 1) * LRU_BLOCK)
        uk = u[:, cs]
        gts = _dot(uk.astype(BF16), wg_ref[k])
        r = _sigmoid(gts[:, :LRU_BLOCK] + gb_ref[0:1, cs])
        i = _sigmoid(gts[:, LRU_BLOCK:] + gb_ref[1:2, cs])
        log_a = (-LRU_C) * r * sp[:, cs]
        a = jnp.exp(log_a)
        a_s[:, cs] = a
        b_s[:, cs] = jnp.sqrt(-jnp.tanh(log_a) * (a * a + 1.0)) * (i * uk)

    def body(t, h):
        row = (tt - 1 - t) if reverse else t
        h = a_s[pl.ds(row, 1), :] * h + b_s[pl.ds(row, 1), :]
        h_s[pl.ds(row, 1), :] = h
        return h

    hcar[...] = lax.fori_loop(0, tt, body, hcar[...], unroll=8)


def _rglru_kernel(*refs, reverse, tt, nt):
    if reverse:
        (gl_ref, rl_ref, pv_ref, nx_ref, gc_ref, rc_ref, hfl_ref, hfc_ref,
         wg_ref, gb_ref, lam_ref, cw_ref, cb_ref, ol_ref, oc_ref, ubuf, a_s, b_s, h_s, hcar) = refs
    else:
        (rl_ref, pv_ref, nx_ref, rc_ref,
         wg_ref, gb_ref, lam_ref, cw_ref, cb_ref, ol_ref, oc_ref, ubuf, a_s, b_s, h_s, hcar) = refs
    s = pl.program_id(1)
    tile = functools.partial(_rglru_tile, wg_ref=wg_ref, gb_ref=gb_ref, lam_ref=lam_ref, cw_ref=cw_ref,
                             cb_ref=cb_ref, ubuf=ubuf, a_s=a_s, b_s=b_s, h_s=h_s, hcar=hcar, reverse=reverse)
    zeros8 = jnp.zeros((8, rl_ref.shape[2]), F32)

    @pl.when(s == 0)
    def _():
        hcar[...] = jnp.zeros_like(hcar)
        tile(rc_ref[0].astype(F32), zeros8, zeros8)
        if reverse:
            oc_ref[0] = (_gelu_tanh(gc_ref[0].astype(F32)) * (hfc_ref[0].astype(F32) + h_s[...])).astype(oc_ref.dtype)
        else:
            oc_ref[0] = h_s[...].astype(oc_ref.dtype)

    @pl.when(s > 0)
    def _():
        tl = (nt - s) if reverse else (s - 1)
        has_prev = (tl > 0).astype(F32)
        has_next = (tl < nt - 1).astype(F32)
        prev8 = pv_ref[0].astype(F32)[8:16, :] * has_prev
        next8 = nx_ref[0].astype(F32)[0:8, :] * has_next
        tile(rl_ref[0].astype(F32), prev8, next8)
        if reverse:
            ol_ref[0] = (_gelu_tanh(gl_ref[0].astype(F32)) * (hfl_ref[0].astype(F32) + h_s[...])).astype(ol_ref.dtype)
        else:
            ol_ref[0] = h_s[...].astype(ol_ref.dtype)


def _rglru_scan(gr_lat, gr_ctx, wg, gb, lam, cw, cb, hf_lat=None, hf_ctx=None):
    reverse = hf_lat is not None
    b, s_len, w2 = gr_lat.shape
    w = w2 // 2
    tt = gr_ctx.shape[1]
    assert s_len % tt == 0 and tt % 16 == 0
    nt = s_len // tt
    hb = tt // 16

    def lat_tile(si):
        return (nt - jnp.maximum(si, 1)) if reverse else jnp.maximum(si - 1, 0)

    lat_rows = lambda col: pl.BlockSpec((1, tt, w), lambda bi, si: (bi, lat_tile(si), col))
    ctx_rows = lambda col: pl.BlockSpec((1, tt, w), lambda bi, si: (bi, 0, col))
    prev_spec = pl.BlockSpec((1, 16, w), lambda bi, si: (bi, jnp.maximum(lat_tile(si) * hb - 1, 0), 1))
    next_spec = pl.BlockSpec((1, 16, w), lambda bi, si: (bi, jnp.minimum((lat_tile(si) + 1) * hb, nt * hb - 1), 1))
    full = lambda shape: pl.BlockSpec(shape, lambda bi, si: (0,) * len(shape))
    params = [wg, gb, lam, cw, cb]
    param_specs = [full(wg.shape), full(gb.shape), full(lam.shape), full(cw.shape), full(cb.shape)]
    if reverse:
        args = [gr_lat, gr_lat, gr_lat, gr_lat, gr_ctx, gr_ctx, hf_lat, hf_ctx] + params
        in_specs = [lat_rows(0), lat_rows(1), prev_spec, next_spec, ctx_rows(0), ctx_rows(1),
                    lat_rows(0), ctx_rows(0)] + param_specs
    else:
        args = [gr_lat, gr_lat, gr_lat, gr_ctx] + params
        in_specs = [lat_rows(1), prev_spec, next_spec, ctx_rows(1)] + param_specs
    return pl.pallas_call(
        functools.partial(_rglru_kernel, reverse=reverse, tt=tt, nt=nt),
        grid=(b, nt + 1),
        in_specs=in_specs,
        out_specs=[lat_rows(0), ctx_rows(0)],
        out_shape=[jax.ShapeDtypeStruct((b, s_len, w), BF16), jax.ShapeDtypeStruct((b, tt, w), BF16)],
        scratch_shapes=[pltpu.VMEM((tt + 16, w), F32), pltpu.VMEM((tt, w), F32), pltpu.VMEM((tt, w), F32),
                        pltpu.VMEM((tt, w), F32), pltpu.VMEM((1, w), F32)],
        compiler_params=_cparams(("parallel", "arbitrary"), 48),
        name="rglru_bwd" if reverse else "rglru_fwd",
    )(*args)


def _gla_chunk(q, k, v, g, st_t, tri, mask, reverse):
    c = q.shape[0]
    hi = g.astype(BF16)
    r1 = g - hi.astype(F32)
    mid = r1.astype(BF16)
    lo = (r1 - mid.astype(F32)).astype(BF16)
    c3 = _dot(tri, jnp.concatenate([hi, mid, lo], axis=1))
    cum = (c3[:, :LANES] + c3[:, LANES:2 * LANES]) + c3[:, 2 * LANES:]
    half = c // 2
    if reverse:
        total, ref = cum[0:1], cum[half:half + 1]
    else:
        total, ref = cum[c - 1:c], cum[half - 1:half]
    qt = (q * jnp.exp(cum - ref)).astype(BF16)
    kt = (k * jnp.exp(ref - cum)).astype(BF16)
    sc = jnp.where(mask, _dot_nt(qt, kt), 0.0).astype(BF16)
    vb = v.astype(BF16)
    o = _dot(sc, vb) + _dot_nt((q * jnp.exp(cum)).astype(BF16), st_t.astype(BF16))
    kbar = (k * jnp.exp(total - cum)).astype(BF16)
    st_t = st_t * jnp.exp(total) + _dot_tn(vb, kbar)
    return o, st_t


def _hgrn_tile(q_ref, f_ref, i_ref, lb, st_ref, *, reverse):
    rows = q_ref.shape[2]
    c = GLA_CHUNK
    q = _silu(q_ref[0, 0].astype(F32))
    f = lb + (1.0 - lb) * _sigmoid(f_ref[0, 0].astype(F32))
    g = jnp.log(f)
    k = 1.0 - f
    v = i_ref[0, 0].astype(F32)
    r_i = lax.broadcasted_iota(jnp.int32, (c, c), 0)
    c_i = lax.broadcasted_iota(jnp.int32, (c, c), 1)
    mask = (c_i >= r_i) if reverse else (c_i <= r_i)
    tri = jnp.where(mask, 1.0, 0.0).astype(BF16)
    nchunk = rows // c
    outs = [None] * nchunk
    st_t = st_ref[...]
    for n in (range(nchunk - 1, -1, -1) if reverse else range(nchunk)):
        rs = slice(n * c, (n + 1) * c)
        outs[n], st_t = _gla_chunk(q[rs], k[rs], v[rs], g[rs], st_t, tri, mask, reverse)
    st_ref[...] = st_t
    return jnp.concatenate(outs, axis=0)


def _hgrn_kernel(*refs, reverse, layer):
    if reverse:
        (ql, fl, il, gl, ofl, qc, fc, ic, gc, ofc, lg_ref, gn_ref, ol_ref, oc_ref, st_ref) = refs
    else:
        (ql, fl, il, qc, fc, ic, lg_ref, ol_ref, oc_ref, st_ref) = refs
    s = pl.program_id(2)
    lg = lg_ref[...]
    e = jnp.exp(lg - jnp.max(lg, axis=0, keepdims=True))
    sm = e / jnp.sum(e, axis=0, keepdims=True)
    lb = jnp.zeros((1, lg.shape[1]), F32)
    for l in range(1, layer + 1):
        lb = lb + sm[l:l + 1]

    def finish(o, of_ref, g_ref, out_ref):
        if reverse:
            tot = of_ref[0, 0].astype(F32) + o
            y = (_rms(tot) * gn_ref[...]) * _silu(g_ref[0, 0].astype(F32))
            out_ref[0] = y.astype(out_ref.dtype)
        else:
            out_ref[0, 0] = o.astype(out_ref.dtype)

    @pl.when(s == 0)
    def _():
        st_ref[...] = jnp.zeros_like(st_ref)
        o = _hgrn_tile(qc, fc, ic, lb, st_ref, reverse=reverse)
        finish(o, ofc if reverse else None, gc if reverse else None, oc_ref)

    @pl.when(s > 0)
    def _():
        o = _hgrn_tile(ql, fl, il, lb, st_ref, reverse=reverse)
        finish(o, ofl if reverse else None, gl if reverse else None, ol_ref)


def _hgrn_scan(p_lat, p_ctx, logits_d, layer, gn=None, of_lat=None, of_ctx=None):
    reverse = of_lat is not None
    b, _, s_len, hd = p_lat.shape
    h = HGRN_HEADS
    rc = p_ctx.shape[2]
    tt = min(s_len, 512)
    assert s_len % tt == 0 and tt % GLA_CHUNK == 0 and rc % GLA_CHUNK == 0
    nt = s_len // tt
    fsel = 2 if reverse else 1

    def lat_tile(si):
        return (nt - jnp.maximum(si, 1)) if reverse else jnp.maximum(si - 1, 0)

    lat = lambda grp: pl.BlockSpec((1, 1, tt, hd), lambda bi, hi, si: (bi, grp * h + hi, lat_tile(si), 0))
    ctx = lambda grp: pl.BlockSpec((1, 1, rc, hd), lambda bi, hi, si: (bi, grp * h + hi, 0, 0))
    lg_spec = pl.BlockSpec((logits_d.shape[0], hd), lambda bi, hi, si: (0, hi))
    if reverse:
        args = [p_lat, p_lat, p_lat, p_lat, of_lat, p_ctx, p_ctx, p_ctx, p_ctx, of_ctx, logits_d, gn]
        in_specs = [lat(0), lat(fsel), lat(3), lat(4), lat(0), ctx(0), ctx(fsel), ctx(3), ctx(4), ctx(0),
                    lg_spec, pl.BlockSpec((1, hd), lambda bi, hi, si: (0, 0))]
        out_specs = [pl.BlockSpec((1, tt, hd), lambda bi, hi, si: (bi, lat_tile(si), hi)),
                     pl.BlockSpec((1, rc, hd), lambda bi, hi, si: (bi, 0, hi))]
        out_shape = [jax.ShapeDtypeStruct((b, s_len, h * hd), BF16), jax.ShapeDtypeStruct((b, rc, h * hd), BF16)]
    else:
        args = [p_lat, p_lat, p_lat, p_ctx, p_ctx, p_ctx, logits_d]
        in_specs = [lat(0), lat(fsel), lat(3), ctx(0), ctx(fsel), ctx(3), lg_spec]
        out_specs = [lat(0), ctx(0)]
        out_shape = [jax.ShapeDtypeStruct((b, h, s_len, hd), BF16), jax.ShapeDtypeStruct((b, h, rc, hd), BF16)]
    return pl.pallas_call(
        functools.partial(_hgrn_kernel, reverse=reverse, layer=layer),
        grid=(b, h, nt + 1),
        in_specs=in_specs,
        out_specs=out_specs,
        out_shape=out_shape,
        scratch_shapes=[pltpu.VMEM((hd, hd), F32)],
        compiler_params=_cparams(("parallel", "parallel", "arbitrary"), 32),
        name="hgrn_bwd" if reverse else "hgrn_fwd",
    )(*args)


def _rope_tables(n_lat, rot_dim, reps):
    n_freq = rot_dim // 4
    t = jnp.arange(n_lat)
    inv = ROPE_THETA ** (-jnp.arange(n_freq, dtype=F32) / n_freq)
    ang_r = (t // GRID_W).astype(F32)[:, None] * inv[None, :]
    ang_c = (t % GRID_W).astype(F32)[:, None] * inv[None, :]
    cos = jnp.concatenate([jnp.cos(ang_r)] * 2 + [jnp.cos(ang_c)] * 2, axis=-1)
    sin = jnp.concatenate([-jnp.sin(ang_r), jnp.sin(ang_r), -jnp.sin(ang_c), jnp.sin(ang_c)], axis=-1)
    return jnp.concatenate([jnp.tile(cos, (1, reps)), jnp.tile(sin, (1, reps))], axis=-1)


def _rope(x, cos, sin, n_freq):
    width = x.shape[1]
    lane = lax.broadcasted_iota(jnp.int32, (1, width), 1)
    first = (lane % (2 * n_freq)) < n_freq
    partner = jnp.where(first, pltpu.roll(x, width - n_freq, 1), pltpu.roll(x, n_freq, 1))
    return x * cos + partner * sin


def _mla_qkv_kernel(*refs, use_rope):
    if use_rope:
        (c_ref, qg_ref, kvg_ref, wq_ref, wk_ref, wvt_ref, gqn_ref, gqr_ref, gkn_ref, gkr_ref, tab_ref,
         q_ref, k_ref, vt_ref) = refs
    else:
        (c_ref, qg_ref, kvg_ref, wq_ref, wk_ref, wvt_ref, gqn_ref, gqr_ref, gkn_ref, gkr_ref,
         q_ref, k_ref, vt_ref) = refs
    c = c_ref[0].astype(F32)
    cqn = (_rms(c[:, :MLA_Q_RANK]) * qg_ref[...]).astype(BF16)
    ckvn = (_rms(c[:, MLA_Q_RANK:MLA_Q_RANK + MLA_KV_RANK]) * kvg_ref[...]).astype(BF16)
    kr = c[:, MLA_Q_RANK + MLA_KV_RANK:]
    qa = _dot(cqn, wq_ref[...])
    ka = _dot(ckvn, wk_ref[...])
    vt = _dot_nt(wvt_ref[...], ckvn)
    lane = lax.broadcasted_iota(jnp.int32, (1, LANES), 1)
    low = lane < MLA_ROPE
    if use_rope:
        cos, sin = tab_ref[:, :LANES], tab_ref[:, LANES:]
    inv_rope = 1.0 / MLA_ROPE

    def halves_rms(x):
        sq = x * x
        ss_lo = jnp.sum(jnp.where(low, sq, 0.0), axis=-1, keepdims=True)
        ss_hi = jnp.sum(jnp.where(low, 0.0, sq), axis=-1, keepdims=True)
        return x * jnp.where(low, lax.rsqrt(ss_lo * inv_rope + NORM_EPS), lax.rsqrt(ss_hi * inv_rope + NORM_EPS))

    krn = halves_rms(kr) * gkr_ref[...]
    if use_rope:
        krn = _rope(krn, cos, sin, MLA_ROPE // 4)
    krn = jnp.where(low, krn, 0.0)
    nope_w = MLA_HEADS * MLA_NOPE
    qscale = MLA_SCALE * LOG2E
    for p in range(MLA_HEADS // 2):
        qr = halves_rms(qa[:, nope_w + p * LANES:nope_w + (p + 1) * LANES]) * gqr_ref[...]
        if use_rope:
            qr = _rope(qr, cos, sin, MLA_ROPE // 4)
        for e in range(2):
            h = 2 * p + e
            hs = slice(h * MLA_NOPE, (h + 1) * MLA_NOPE)
            qn = _rms(qa[:, hs]) * gqn_ref[...]
            rot = qr if e == 0 else pltpu.roll(qr, MLA_ROPE, 1)
            rot = jnp.where(low, rot, 0.0)
            q_ref[0, h] = (jnp.concatenate([qn, rot], axis=1) * qscale).astype(BF16)
            kn = _rms(ka[:, hs]) * gkn_ref[...]
            k_ref[0, h] = jnp.concatenate([kn, krn], axis=1).astype(BF16)
            vt_ref[0, h, 0] = vt[hs, :].astype(BF16)


def _mla_qkv(c, qg, kvg, wq, wk, wvt, gqn, gqr, gkn, gkr, tab, tm):
    b, r, cw = c.shape
    assert r % tm == 0
    h = MLA_HEADS
    use_rope = tab is not None
    full = lambda a: pl.BlockSpec(a.shape, lambda bi, i: (0,) * a.ndim)
    args = [c, qg, kvg, wq, wk, wvt, gqn, gqr, gkn, gkr]
    in_specs = [pl.BlockSpec((1, tm, cw), lambda bi, i: (bi, i, 0))] + [full(a) for a in args[1:]]
    if use_rope:
        args.append(tab)
        in_specs.append(pl.BlockSpec((tm, tab.shape[1]), lambda bi, i: (i, 0)))
    return pl.pallas_call(
        functools.partial(_mla_qkv_kernel, use_rope=use_rope),
        grid=(b, r // tm),
        in_specs=in_specs,
        out_specs=[pl.BlockSpec((1, h, tm, MLA_QK_PAD), lambda bi, i: (bi, 0, i, 0)),
                   pl.BlockSpec((1, h, tm, MLA_QK_PAD), lambda bi, i: (bi, 0, i, 0)),
                   pl.BlockSpec((1, h, 1, MLA_V, tm), lambda bi, i: (bi, 0, i, 0, 0))],
        out_shape=[jax.ShapeDtypeStruct((b, h, r, MLA_QK_PAD), BF16),
                   jax.ShapeDtypeStruct((b, h, r, MLA_QK_PAD), BF16),
                   jax.ShapeDtypeStruct((b, h, r // tm, MLA_V, tm), BF16)],
        compiler_params=_cparams(("parallel", "parallel"), 48),
        name="mla_qkv_rope" if use_rope else "mla_qkv",
    )(*args)


def _attn_step(q, k, vt, carry):
    m, l, acc = carry
    s = _dot_nt(k, q)
    m_new = jnp.maximum(m, jnp.max(s, axis=0, keepdims=True))
    alpha = jnp.exp2(m - m_new)
    p = jnp.exp2(s - m_new)
    l = alpha * l + jnp.sum(p, axis=0, keepdims=True)
    acc = alpha * acc + _dot(vt, p.astype(BF16))
    return m_new, l, acc


def _mla_attn_kernel(*refs, with_lat):
    if with_lat:
        q_ref, kc_ref, vc_ref, kl_ref, vl_ref, o_ref = refs
    else:
        q_ref, kc_ref, vc_ref, o_ref = refs
    q = q_ref[0, 0]
    tq = q.shape[0]
    carry = (jnp.full((1, tq), -jnp.inf, F32), jnp.zeros((1, tq), F32), jnp.zeros((MLA_V, tq), F32))
    carry = _attn_step(q, kc_ref[0, 0], vc_ref[0, 0, 0], carry)
    if with_lat:
        nk, tk = vl_ref.shape[2], vl_ref.shape[4]

        def body(j, cr):
            k = kl_ref[0, 0, pl.ds(pl.multiple_of(j * tk, tk), tk), :]
            return _attn_step(q, k, vl_ref[0, 0, j], cr)

        carry = lax.fori_loop(0, nk, body, carry)
    _, l, acc = carry
    o_ref[0] = (acc / l).T.astype(o_ref.dtype)


def _mla_attention(q, k_ctx, vt_ctx, k_lat=None, vt_lat=None):
    b, h, n, dq = q.shape
    with_lat = k_lat is not None
    tq = min(n, 512)
    rc = k_ctx.shape[2]
    args = [q, k_ctx, vt_ctx]
    in_specs = [pl.BlockSpec((1, 1, tq, dq), lambda bi, hi, i: (bi, hi, i, 0)),
                pl.BlockSpec((1, 1, rc, dq), lambda bi, hi, i: (bi, hi, 0, 0)),
                pl.BlockSpec((1, 1, 1, MLA_V, rc), lambda bi, hi, i: (bi, hi, 0, 0, 0))]
    if with_lat:
        nk, tk = vt_lat.shape[2], vt_lat.shape[4]
        args += [k_lat, vt_lat]
        in_specs += [pl.BlockSpec((1, 1, nk * tk, dq), lambda bi, hi, i: (bi, hi, 0, 0)),
                     pl.BlockSpec((1, 1, nk, MLA_V, tk), lambda bi, hi, i: (bi, hi, 0, 0, 0))]
    return pl.pallas_call(
        functools.partial(_mla_attn_kernel, with_lat=with_lat),
        grid=(b, h, n // tq),
        in_specs=in_specs,
        out_specs=pl.BlockSpec((1, tq, MLA_V), lambda bi, hi, i: (bi, i, hi)),
        out_shape=jax.ShapeDtypeStruct((b, n, h * MLA_V), BF16),
        compiler_params=_cparams(("parallel", "parallel", "arbitrary"), 48),
        name="mla_attn" if with_lat else "mla_attn_ctx",
    )(*args)


def _swa_inproj_kernel(*refs, use_rope, n_norm_tiles):
    if use_rope:
        x_ref, g_ref, sh_ref, sc_ref, w_ref, hg_ref, tab_ref, o_ref, h_ref = refs
    else:
        x_ref, g_ref, sh_ref, sc_ref, w_ref, hg_ref, o_ref, h_ref = refs
    j = pl.program_id(2)

    @pl.when(j == 0)
    def _():
        h_ref[...] = _adaln(x_ref[0], g_ref[...], sh_ref[0], sc_ref[0]).astype(BF16)

    res = _dot(h_ref[...], w_ref[...])

    @pl.when(j < n_norm_tiles)
    def _():
        for e in range(res.shape[1] // SWA_HEAD_DIM):
            cs = slice(e * SWA_HEAD_DIM, (e + 1) * SWA_HEAD_DIM)
            xh = _rms(res[:, cs]) * hg_ref[:, cs]
            if use_rope:
                xh = _rope(xh, tab_ref[:, :SWA_HEAD_DIM], tab_ref[:, SWA_HEAD_DIM:], SWA_HEAD_DIM // 4)
            o_ref[0, :, cs] = xh.astype(o_ref.dtype)

    @pl.when(j >= n_norm_tiles)
    def _():
        o_ref[0] = res.astype(o_ref.dtype)


def _swa_inproj(x, g, shift, scale, w, head_gain, tab):
    b, r, d = x.shape
    n = w.shape[1]
    tm = min(r, 512)
    tn = 512
    use_rope = tab is not None
    n_norm_tiles = (SWA_Q_HEADS + SWA_KV_HEADS) * SWA_HEAD_DIM // tn
    vec = pl.BlockSpec((1, 1, d), lambda bi, i, j: (bi, 0, 0))
    args = [x, g, shift, scale, w, head_gain]
    in_specs = [pl.BlockSpec((1, tm, d), lambda bi, i, j: (bi, i, 0)),
                pl.BlockSpec((1, d), lambda bi, i, j: (0, 0)),
                vec, vec,
                pl.BlockSpec((d, tn), lambda bi, i, j: (0, j)),
                pl.BlockSpec((1, tn), lambda bi, i, j: (0, j))]
    if use_rope:
        args.append(tab)
        in_specs.append(pl.BlockSpec((tm, tab.shape[1]), lambda bi, i, j: (i, 0)))
    return pl.pallas_call(
        functools.partial(_swa_inproj_kernel, use_rope=use_rope, n_norm_tiles=n_norm_tiles),
        grid=(b, r // tm, n // tn),
        in_specs=in_specs,
        out_specs=pl.BlockSpec((1, tm, tn), lambda bi, i, j: (bi, i, j)),
        out_shape=jax.ShapeDtypeStruct((b, r, n), BF16),
        scratch_shapes=[pltpu.VMEM((tm, d), BF16)],
        compiler_params=_cparams(("parallel", "parallel", "arbitrary"), 48),
        name="swa_inproj_rope" if use_rope else "swa_inproj",
    )(*args)


def _swa_attn_kernel(sink_ref, q_ref, kp_ref, kc_ref, kn_ref, vp_ref, vc_ref, vn_ref, kx_ref, vx_ref, o_ref):
    hk = pl.program_id(1)
    j = pl.program_id(2)
    nb = pl.num_programs(2)
    blk, grp = SWA_BLOCK, SWA_GROUP
    q = q_ref[0]
    qs = jnp.concatenate([q[:, e * SWA_HEAD_DIM:(e + 1) * SWA_HEAD_DIM] for e in range(grp)], axis=0)
    nctx = kx_ref.shape[1]
    keys = jnp.concatenate([kx_ref[0], kp_ref[0], kc_ref[0], kn_ref[0]], axis=0)
    vals = jnp.concatenate([vx_ref[0], vp_ref[0], vc_ref[0], vn_ref[0]], axis=0)
    s = _dot_nt(qs, keys)
    rows = s.shape[0]
    r_i = lax.broadcasted_iota(jnp.int32, (rows, blk), 0) % blk
    c_i = lax.broadcasted_iota(jnp.int32, (rows, blk), 1)
    off_prev = jnp.where(j > 0, 0, 2 * blk)
    off_next = jnp.where(j < nb - 1, 0, 2 * blk)
    s_prev = jnp.where(c_i >= r_i + off_prev, s[:, nctx:nctx + blk], MASK_VALUE)
    s_next = jnp.where(c_i + off_next <= r_i, s[:, nctx + 2 * blk:], MASK_VALUE)
    s = jnp.concatenate([s[:, :nctx], s_prev, s[:, nctx + blk:nctx + 2 * blk], s_next], axis=1)
    row1 = lax.broadcasted_iota(jnp.int32, (rows, 1), 0)
    sink = jnp.zeros((rows, 1), F32)
    for e in range(grp):
        sink = jnp.where(row1 // blk == e, sink_ref[hk * grp + e] * LOG2E, sink)
    m = jnp.maximum(jnp.max(s, axis=1, keepdims=True), sink)
    p = jnp.exp2(s - m)
    l = jnp.sum(p, axis=1, keepdims=True) + jnp.exp2(sink - m)
    o = _dot(p.astype(BF16), vals) / l
    o_ref[0] = jnp.concatenate([o[e * blk:(e + 1) * blk] for e in range(grp)], axis=1).astype(o_ref.dtype)


def _swa_attention(qkv_lat, qkv_ctx, sink):
    b, n, _ = qkv_lat.shape
    nctx = qkv_ctx.shape[1]
    blk, dh, grp = SWA_BLOCK, SWA_HEAD_DIM, SWA_GROUP
    nb = n // blk
    kcol = SWA_Q_HEADS
    vcol = SWA_Q_HEADS + SWA_KV_HEADS
    prev = lambda j: jnp.maximum(j - 1, 0)
    nxt = lambda j: jnp.minimum(j + 1, nb - 1)
    lat = lambda col0, rowf: pl.BlockSpec((1, blk, dh), lambda bi, hk, j: (bi, rowf(j), col0 + hk))
    ctx = lambda col0: pl.BlockSpec((1, nctx, dh), lambda bi, hk, j: (bi, 0, col0 + hk))
    same = lambda j: j
    return pl.pallas_call(
        _swa_attn_kernel,
        grid=(b, SWA_KV_HEADS, nb),
        in_specs=[pl.BlockSpec(memory_space=pltpu.SMEM),
                  pl.BlockSpec((1, blk, grp * dh), lambda bi, hk, j: (bi, j, hk)),
                  lat(kcol, prev), lat(kcol, same), lat(kcol, nxt),
                  lat(vcol, prev), lat(vcol, same), lat(vcol, nxt),
                  ctx(kcol), ctx(vcol)],
        out_specs=pl.BlockSpec((1, blk, grp * dh), lambda bi, hk, j: (bi, j, hk)),
        out_shape=jax.ShapeDtypeStruct((b, n, SWA_Q_HEADS * dh), BF16),
        compiler_params=_cparams(("parallel", "parallel", "arbitrary"), 32),
        name="swa_attn",
    )(sink, qkv_lat, qkv_lat, qkv_lat, qkv_lat, qkv_lat, qkv_lat, qkv_lat, qkv_ctx, qkv_ctx)


def kernel(x, c, ctx, c_ctx, mod_w, mod_b, norm_g, ffn_w_in, ffn_w_out, rglru_w_in, rglru_conv_w, rglru_conv_b, rglru_gate_w, rglru_gate_b, rglru_lambda, rglru_w_out, hgrn_w_in, hgrn_lb_logits, hgrn_gnorm_g, hgrn_w_out, mla_w_in, mla_q_norm_g, mla_kv_norm_g, mla_w_uq, mla_w_ukv, mla_qk_g, mla_w_out, swa_w_in, swa_qk_g, swa_sink, swa_w_out):
    b, n_lat, d = x.shape
    depth = mod_w.shape[0]
    assert depth == DEPTH and b + 1 <= 8
    bf = lambda a: a.astype(BF16)

    cond = jnp.concatenate([c, c_ctx[None, :], jnp.zeros((8 - b - 1, d), F32)], axis=0)
    mod = _modulation(cond, mod_w, mod_b).reshape(depth, 8, 6, d)

    x_lat, x_ctx = x, ctx
    for layer in range(depth):
        kind = layer % 4
        need_ctx = layer < depth - 1
        m_lat = [mod[layer, :b, k][:, None, :] for k in range(6)]
        m_ctx = [jnp.broadcast_to(mod[layer, b:b + 1, k][:, None, :], (b, 1, d)) for k in range(6)]
        g1 = norm_g[layer, 0][None, :]
        g2 = norm_g[layer, 1][None, :]
        y_ctx = None
        if kind == 0:
            w_in = bf(rglru_w_in[0])
            gr_lat = _inproj(x_lat, g1, m_lat[0], m_lat[1], w_in, tn=1024, name="rglru_inproj")
            gr_ctx = _inproj(x_ctx, g1, m_ctx[0], m_ctx[1], w_in, tn=1024, name="rglru_inproj_ctx")
            gw = rglru_gate_w[0]
            wg = [bf(jnp.concatenate([gw[dr, 0], gw[dr, 1]], axis=-1)) for dr in range(2)]
            cw, cb = rglru_conv_w[0], rglru_conv_b[0][None, :]
            scan = lambda dr, **kw: _rglru_scan(gr_lat, gr_ctx, wg[dr], rglru_gate_b[0, dr],
                                                rglru_lambda[0, dr][None, :], cw, cb, **kw)
            hf_lat, hf_ctx = scan(0)
            y_lat, y_ctx = scan(1, hf_lat=hf_lat, hf_ctx=hf_ctx)
            w_out = bf(rglru_w_out[0])
        elif kind == 1:
            w_in = bf(hgrn_w_in[0])
            p_lat = _inproj(x_lat, g1, m_lat[0], m_lat[1], w_in, tn=1024, head_major=True, name="hgrn_inproj")
            p_ctx = _inproj(x_ctx, g1, m_ctx[0], m_ctx[1], w_in, tn=1024, head_major=True, name="hgrn_inproj_ctx")
            of_lat, of_ctx = _hgrn_scan(p_lat, p_ctx, hgrn_lb_logits[:, 0, :], layer)
            y_lat, y_ctx = _hgrn_scan(p_lat, p_ctx, hgrn_lb_logits[:, 1, :], layer, gn=hgrn_gnorm_g[0][None, :],
                                      of_lat=of_lat, of_ctx=of_ctx)
            w_out = bf(hgrn_w_out[0])
        elif kind == 2:
            cw_real = mla_w_in.shape[2]
            cw_pad = -(-cw_real // LANES) * LANES
            w_in = bf(jnp.pad(mla_w_in[0], ((0, 0), (0, cw_pad - cw_real))))
            c_lat = _inproj(x_lat, g1, m_lat[0], m_lat[1], w_in, tn=cw_pad, name="mla_inproj")
            c_ctx_ = _inproj(x_ctx, g1, m_ctx[0], m_ctx[1], w_in, tn=cw_pad, name="mla_inproj_ctx")
            wq3 = mla_w_uq[0].reshape(MLA_Q_RANK, MLA_HEADS, MLA_NOPE + MLA_ROPE)
            wq = bf(jnp.concatenate([wq3[:, :, :MLA_NOPE].reshape(MLA_Q_RANK, -1),
                                     wq3[:, :, MLA_NOPE:].reshape(MLA_Q_RANK, -1)], axis=1))
            wkv3 = mla_w_ukv[0].reshape(MLA_KV_RANK, MLA_HEADS, MLA_NOPE + MLA_V)
            wk = bf(wkv3[:, :, :MLA_NOPE].reshape(MLA_KV_RANK, -1))
            wvt = bf(wkv3[:, :, MLA_NOPE:].reshape(MLA_KV_RANK, -1).T)
            qk_g = mla_qk_g[0]
            gqn, gkn = qk_g[0:1, :MLA_NOPE], qk_g[1:2, :MLA_NOPE]
            gqr = jnp.tile(qk_g[0:1, MLA_NOPE:], (1, LANES // MLA_ROPE))
            gkr = jnp.pad(qk_g[1:2, MLA_NOPE:], ((0, 0), (0, LANES - MLA_ROPE)))
            tab = _rope_tables(n_lat, MLA_ROPE, LANES // MLA_ROPE)
            small = (mla_q_norm_g[0][None, :], mla_kv_norm_g[0][None, :], wq, wk, wvt, gqn, gqr, gkn, gkr)
            q_l, k_l, vt_l = _mla_qkv(c_lat, *small, tab, min(n_lat, 512))
            q_c, k_c, vt_c = _mla_qkv(c_ctx_, *small, None, c_ctx_.shape[1])
            y_lat = _mla_attention(q_l, k_c, vt_c, k_l, vt_l)
            if need_ctx:
                y_ctx = _mla_attention(q_c, k_c, vt_c)
            w_out = bf(mla_w_out[0])
        else:
            w_in = bf(swa_w_in[0])
            gq = jnp.tile(swa_qk_g[0, 0] * (SWA_SCALE * LOG2E), SWA_Q_HEADS)
            gk = jnp.tile(swa_qk_g[0, 1], SWA_KV_HEADS)
            head_gain = jnp.concatenate([gq, gk, jnp.ones((SWA_KV_HEADS * SWA_HEAD_DIM,), F32)])[None, :]
            tab = _rope_tables(n_lat, SWA_HEAD_DIM, 1)
            qkv_lat = _swa_inproj(x_lat, g1, m_lat[0], m_lat[1], w_in, head_gain, tab)
            qkv_ctx = _swa_inproj(x_ctx, g1, m_ctx[0], m_ctx[1], w_in, head_gain, None)
            y_lat = _swa_attention(qkv_lat, qkv_ctx, swa_sink[0])
            assert not need_ctx
            w_out = bf(swa_w_out[0])

        w_ffn_in, w_ffn_out = bf(ffn_w_in[layer]), bf(ffn_w_out[layer])
        x_lat = _outproj(y_lat, w_out, x_lat, m_lat[2])
        x_lat = _ffn(x_lat, g2, m_lat[3], m_lat[4], m_lat[5], w_ffn_in, w_ffn_out)
        if need_ctx:
            x_ctx = _outproj(y_ctx, w_out, x_ctx, m_ctx[2], name="outproj_ctx")
            x_ctx = _ffn(x_ctx, g2, m_ctx[3], m_ctx[4], m_ctx[5], w_ffn_in, w_ffn_out, name="ffn_ctx")
    return x_lat
```

```python
import functools
import math

import jax
import jax.numpy as jnp
from jax import lax
from jax.experimental import pallas as pl
from jax.experimental.pallas import tpu as pltpu

F32 = jnp.float32
BF16 = jnp.bfloat16

DEPTH = 4
GRID_W = 64
NORM_EPS = 1e-6
ROPE_THETA = 10000.0
LOG2E = math.log2(math.e)

LRU_BLOCKS = 16
LRU_BLOCK = 128
LRU_C = 8.0
CONV_WIDTH = 4
CONV_PAD_LEFT = 2

HGRN_HEADS = 16
HGRN_HEAD_DIM = 128
GLA_CHUNK = 64

MLA_HEADS = 16
MLA_Q_RANK = 512
MLA_KV_RANK = 512
MLA_NOPE = 128
MLA_ROPE = 64
MLA_V = 128
MLA_SCALE = (MLA_NOPE + MLA_ROPE) ** -0.5
MLA_QK_PAD = 256
MLA_V_ROWS = MLA_V + 16

SWA_Q_HEADS = 16
SWA_KV_HEADS = 4
SWA_GROUP = SWA_Q_HEADS // SWA_KV_HEADS
SWA_HEAD_DIM = 128
SWA_WINDOW = 128
SWA_BLOCK = 128
SWA_SCALE = SWA_HEAD_DIM ** -0.5

LANES = 128
MIB = 1024 * 1024
MASK_VALUE = -1e30


def _cparams(semantics, vmem_mib):
    return pltpu.CompilerParams(dimension_semantics=semantics, vmem_limit_bytes=vmem_mib * MIB)


def _sigmoid(x):
    return jax.nn.sigmoid(x)


def _silu(x):
    return x * _sigmoid(x)


def _rms(x, eps=NORM_EPS):
    return x * lax.rsqrt(jnp.mean(x * x, axis=-1, keepdims=True) + eps)


def _adaln(x, g, shift, scale):
    return (_rms(x) * g) * (1.0 + scale) + shift


def _dot(a, b):
    return jnp.dot(a, b, preferred_element_type=F32)


def _dot_nt(a, b):
    return lax.dot_general(a, b, (((1,), (1,)), ((), ())), preferred_element_type=F32)


def _dot_tn(a, b):
    return lax.dot_general(a, b, (((0,), (0,)), ((), ())), preferred_element_type=F32)


def _mod_kernel(c_ref, w_ref, b_ref, o_ref):
    s = _silu(c_ref[...]).astype(BF16)
    o_ref[0] = _dot(s, w_ref[0].astype(BF16)) + b_ref[0]


def _modulation(cond, mod_w, mod_b):
    depth, d, n = mod_w.shape
    tn = 1024
    return pl.pallas_call(
        _mod_kernel,
        grid=(depth, n // tn),
        in_specs=[pl.BlockSpec((8, d), lambda l, j: (0, 0)),
                  pl.BlockSpec((1, d, tn), lambda l, j: (l, 0, j)),
                  pl.BlockSpec((1, 1, tn), lambda l, j: (l, 0, j))],
        out_specs=pl.BlockSpec((1, 8, tn), lambda l, j: (l, 0, j)),
        out_shape=jax.ShapeDtypeStruct((depth, 8, n), F32),
        compiler_params=_cparams(("parallel", "parallel"), 40),
        name="modulation",
    )(cond, mod_w, mod_b.reshape(depth, 1, n))


def _inproj_kernel(x_ref, g_ref, sh_ref, sc_ref, w_ref, o_ref, h_ref, *, head_major):
    @pl.when(pl.program_id(2) == 0)
    def _():
        h_ref[...] = _adaln(x_ref[0], g_ref[...], sh_ref[0], sc_ref[0]).astype(BF16)

    res = _dot(h_ref[...], w_ref[...])
    if head_major:
        for c in range(res.shape[1] // LANES):
            o_ref[0, c] = res[:, c * LANES:(c + 1) * LANES].astype(o_ref.dtype)
    else:
        o_ref[0] = res.astype(o_ref.dtype)


def _inproj(x, g, shift, scale, w, *, tn, head_major=False, name="inproj"):
    b, r, d = x.shape
    n = w.shape[1]
    tm = min(r, 512)
    assert r % tm == 0 and n % tn == 0
    if head_major:
        out_shape = jax.ShapeDtypeStruct((b, n // LANES, r, LANES), BF16)
        out_spec = pl.BlockSpec((1, tn // LANES, tm, LANES), lambda bi, i, j: (bi, j, i, 0))
    else:
        out_shape = jax.ShapeDtypeStruct((b, r, n), BF16)
        out_spec = pl.BlockSpec((1, tm, tn), lambda bi, i, j: (bi, i, j))
    vec = pl.BlockSpec((1, 1, d), lambda bi, i, j: (bi, 0, 0))
    return pl.pallas_call(
        functools.partial(_inproj_kernel, head_major=head_major),
        grid=(b, r // tm, n // tn),
        in_specs=[pl.BlockSpec((1, tm, d), lambda bi, i, j: (bi, i, 0)),
                  pl.BlockSpec((1, d), lambda bi, i, j: (0, 0)),
                  vec, vec,
                  pl.BlockSpec((d, tn), lambda bi, i, j: (0, j))],
        out_specs=out_spec,
        out_shape=out_shape,
        scratch_shapes=[pltpu.VMEM((tm, d), BF16)],
        compiler_params=_cparams(("parallel", "parallel", "arbitrary"), 48),
        name=name,
    )(x, g, shift, scale, w)


def _outproj_kernel(a_ref, w_ref, x_ref, gate_ref, o_ref):
    o_ref[0] = x_ref[0] + gate_ref[0] * _dot(a_ref[0], w_ref[...])


def _outproj(a, w, x, gate, name="outproj"):
    b, r, k = a.shape
    d = w.shape[1]
    tm = min(r, 512)
    return pl.pallas_call(
        _outproj_kernel,
        grid=(b, r // tm),
        in_specs=[pl.BlockSpec((1, tm, k), lambda bi, i: (bi, i, 0)),
                  pl.BlockSpec((k, d), lambda bi, i: (0, 0)),
                  pl.BlockSpec((1, tm, d), lambda bi, i: (bi, i, 0)),
                  pl.BlockSpec((1, 1, d), lambda bi, i: (bi, 0, 0))],
        out_specs=pl.BlockSpec((1, tm, d), lambda bi, i: (bi, i, 0)),
        out_shape=jax.ShapeDtypeStruct((b, r, d), F32),
        compiler_params=_cparams(("parallel", "parallel"), 48),
        name=name,
    )(a, w, x, gate)


def _ffn_kernel(x_ref, g_ref, sh_ref, sc_ref, gate_ref, wg_ref, wu_ref, wo_ref, o_ref, f_ref, acc_ref):
    j = pl.program_id(2)

    @pl.when(j == 0)
    def _():
        f_ref[...] = _adaln(x_ref[0], g_ref[...], sh_ref[0], sc_ref[0]).astype(BF16)
        acc_ref[...] = jnp.zeros_like(acc_ref)

    f = f_ref[...]
    gt = _dot(f, wg_ref[...])
    up = _dot(f, wu_ref[...])
    act = (_silu(gt) * up).astype(BF16)
    acc_ref[...] += _dot(act, wo_ref[...])

    @pl.when(j == pl.num_programs(2) - 1)
    def _():
        o_ref[0] = x_ref[0] + gate_ref[0] * acc_ref[...]


def _ffn(x, g, shift, scale, gate, w_in, w_out, name="ffn"):
    b, r, d = x.shape
    hidden = w_out.shape[0]
    tm = min(r, 512)
    tf = 512
    nf = hidden // tf
    assert hidden % tf == 0
    vec = pl.BlockSpec((1, 1, d), lambda bi, i, j: (bi, 0, 0))
    return pl.pallas_call(
        _ffn_kernel,
        grid=(b, r // tm, nf),
        in_specs=[pl.BlockSpec((1, tm, d), lambda bi, i, j: (bi, i, 0)),
                  pl.BlockSpec((1, d), lambda bi, i, j: (0, 0)),
                  vec, vec, vec,
                  pl.BlockSpec((d, tf), lambda bi, i, j: (0, j)),
                  pl.BlockSpec((d, tf), lambda bi, i, j: (0, j + nf)),
                  pl.BlockSpec((tf, d), lambda bi, i, j: (j, 0))],
        out_specs=pl.BlockSpec((1, tm, d), lambda bi, i, j: (bi, i, 0)),
        out_shape=jax.ShapeDtypeStruct((b, r, d), F32),
        scratch_shapes=[pltpu.VMEM((tm, d), BF16), pltpu.VMEM((tm, d), F32)],
        compiler_params=_cparams(("parallel", "parallel", "arbitrary"), 48),
        name=name,
    )(x, g, shift, scale, gate, w_in, w_in, w_out)


def _softplus(x):
    return jnp.maximum(x, 0.0) + jnp.log1p(jnp.exp(-jnp.abs(x)))


def _gelu_tanh(x):
    return 0.5 * x * (1.0 + jnp.tanh(math.sqrt(2.0 / math.pi) * (x + 0.044715 * (x * x * x))))


def _rglru_tile(rec, prev8, next8, wg_ref, gb_ref, lam_ref, cw_ref, cb_ref,
                ubuf, a_s, b_s, h_s, hcar, *, reverse):
    tt = rec.shape[0]
    ubuf[0:8, :] = prev8
    ubuf[8:8 + tt, :] = rec
    ubuf[8 + tt:16 + tt, :] = next8
    base = 8 - CONV_PAD_LEFT
    u = cb_ref[...] + cw_ref[0:1, :] * ubuf[base:base + tt, :]
    for j in range(1, CONV_WIDTH):
        u = u + cw_ref[j:j + 1, :] * ubuf[base + j:base + j + tt, :]
    sp = _softplus(-lam_ref[...])
    for k in range(LRU_BLOCKS):
        cs = slice(k * LRU_BLOCK, (k + 1) * LRU_BLOCK)
        uk = u[:, cs]
        gts = _dot(uk.astype(BF16), wg_ref[k])
        r = _sigmoid(gts[:, :LRU_BLOCK] + gb_ref[0:1, cs])
        i = _sigmoid(gts[:, LRU_BLOCK:] + gb_ref[1:2, cs])
        log_a = (-LRU_C) * r * sp[:, cs]
        a = jnp.exp(log_a)
        a_s[:, cs] = a
        b_s[:, cs] = jnp.sqrt(-jnp.tanh(log_a) * (a * a + 1.0)) * (i * uk)

    def body(t, h):
        row = (tt - 1 - t) if reverse else t
        h = a_s[pl.ds(row, 1), :] * h + b_s[pl.ds(row, 1), :]
        h_s[pl.ds(row, 1), :] = h
        return h

    hcar[...] = lax.fori_loop(0, tt, body, hcar[...], unroll=8)


def _rglru_kernel(*refs, reverse, tt, nt):
    if reverse:
        (gl_ref, rl_ref, pv_ref, nx_ref, gc_ref, rc_ref, hfl_ref, hfc_ref,
         wg_ref, gb_ref, lam_ref, cw_ref, cb_ref, ol_ref, oc_ref, ubuf, a_s, b_s, h_s, hcar) = refs
    else:
        (rl_ref, pv_ref, nx_ref, rc_ref,
         wg_ref, gb_ref, lam_ref, cw_ref, cb_ref, ol_ref, oc_ref, ubuf, a_s, b_s, h_s, hcar) = refs
    s = pl.program_id(1)
    tile = functools.partial(_rglru_tile, wg_ref=wg_ref, gb_ref=gb_ref, lam_ref=lam_ref, cw_ref=cw_ref,
                             cb_ref=cb_ref, ubuf=ubuf, a_s=a_s, b_s=b_s, h_s=h_s, hcar=hcar, reverse=reverse)
    zeros8 = jnp.zeros((8, rl_ref.shape[2]), F32)

    @pl.when(s == 0)
    def _():
        hcar[...] = jnp.zeros_like(hcar)
        tile(rc_ref[0].astype(F32), zeros8, zeros8)
        if reverse:
            oc_ref[0] = (_gelu_tanh(gc_ref[0].astype(F32)) * (hfc_ref[0].astype(F32) + h_s[...])).astype(oc_ref.dtype)
        else:
            oc_ref[0] = h_s[...].astype(oc_ref.dtype)

    @pl.when(s > 0)
    def _():
        tl = (nt - s) if reverse else (s - 1)
        has_prev = (tl > 0).astype(F32)
        has_next = (tl < nt - 1).astype(F32)
        prev8 = pv_ref[0].astype(F32)[8:16, :] * has_prev
        next8 = nx_ref[0].astype(F32)[0:8, :] * has_next
        tile(rl_ref[0].astype(F32), prev8, next8)
        if reverse:
            ol_ref[0] = (_gelu_tanh(gl_ref[0].astype(F32)) * (hfl_ref[0].astype(F32) + h_s[...])).astype(ol_ref.dtype)
        else:
            ol_ref[0] = h_s[...].astype(ol_ref.dtype)


def _rglru_scan(gr_lat, gr_ctx, wg, gb, lam, cw, cb, hf_lat=None, hf_ctx=None):
    reverse = hf_lat is not None
    b, s_len, w2 = gr_lat.shape
    w = w2 // 2
    tt = gr_ctx.shape[1]
    assert s_len % tt == 0 and tt % 16 == 0
    nt = s_len // tt
    hb = tt // 16

    def lat_tile(si):
        return (nt - jnp.maximum(si, 1)) if reverse else jnp.maximum(si - 1, 0)

    lat_rows = lambda col: pl.BlockSpec((1, tt, w), lambda bi, si: (bi, lat_tile(si), col))
    ctx_rows = lambda col: pl.BlockSpec((1, tt, w), lambda bi, si: (bi, 0, col))
    prev_spec = pl.BlockSpec((1, 16, w), lambda bi, si: (bi, jnp.maximum(lat_tile(si) * hb - 1, 0), 1))
    next_spec = pl.BlockSpec((1, 16, w), lambda bi, si: (bi, jnp.minimum((lat_tile(si) + 1) * hb, nt * hb - 1), 1))
    full = lambda shape: pl.BlockSpec(shape, lambda bi, si: (0,) * len(shape))
    params = [wg, gb, lam, cw, cb]
    param_specs = [full(wg.shape), full(gb.shape), full(lam.shape), full(cw.shape), full(cb.shape)]
    if reverse:
        args = [gr_lat, gr_lat, gr_lat, gr_lat, gr_ctx, gr_ctx, hf_lat, hf_ctx] + params
        in_specs = [lat_rows(0), lat_rows(1), prev_spec, next_spec, ctx_rows(0), ctx_rows(1),
                    lat_rows(0), ctx_rows(0)] + param_specs
    else:
        args = [gr_lat, gr_lat, gr_lat, gr_ctx] + params
        in_specs = [lat_rows(1), prev_spec, next_spec, ctx_rows(1)] + param_specs
    return pl.pallas_call(
        functools.partial(_rglru_kernel, reverse=reverse, tt=tt, nt=nt),
        grid=(b, nt + 1),
        in_specs=in_specs,
        out_specs=[lat_rows(0), ctx_rows(0)],
        out_shape=[jax.ShapeDtypeStruct((b, s_len, w), BF16), jax.ShapeDtypeStruct((b, tt, w), BF16)],
        scratch_shapes=[pltpu.VMEM((tt + 16, w), F32), pltpu.VMEM((tt, w), F32), pltpu.VMEM((tt, w), F32),
                        pltpu.VMEM((tt, w), F32), pltpu.VMEM((1, w), F32)],
        compiler_params=_cparams(("parallel", "arbitrary"), 48),
        name="rglru_bwd" if reverse else "rglru_fwd",
    )(*args)


def _hgrn_head(q, f, v, st_t, reverse):
    rows = q.shape[0]
    c = GLA_CHUNK
    half = c // 2
    nchunk = rows // c
    chunks = [slice(n * c, (n + 1) * c) for n in range(nchunk)]
    r_i = lax.broadcasted_iota(jnp.int32, (c, c), 0)
    c_i = lax.broadcasted_iota(jnp.int32, (c, c), 1)
    mask = (c_i >= r_i) if reverse else (c_i <= r_i)
    tri = jnp.where(mask, 1.0, 0.0).astype(BF16)
    g = jnp.log(f)
    k = 1.0 - f
    hi = g.astype(BF16)
    r1 = g - hi.astype(F32)
    mid = r1.astype(BF16)
    lo = (r1 - mid.astype(F32)).astype(BF16)
    g3 = jnp.concatenate([hi, mid, lo], axis=1)
    cums = []
    for rs in chunks:
        c3 = _dot(tri, g3[rs])
        cums.append((c3[:, :LANES] + c3[:, LANES:2 * LANES]) + c3[:, 2 * LANES:])
    if reverse:
        totals = [cm[0:1] for cm in cums]
        refs = [cm[half:half + 1] for cm in cums]
    else:
        totals = [cm[c - 1:c] for cm in cums]
        refs = [cm[half - 1:half] for cm in cums]
    cum = jnp.concatenate(cums, axis=0)
    ref_b = jnp.concatenate([jnp.broadcast_to(r, (c, LANES)) for r in refs], axis=0)
    qt = q * jnp.exp(cum - ref_b)
    kt = k * jnp.exp(ref_b - cum)
    qtb, ktb, vb = qt.astype(BF16), kt.astype(BF16), v.astype(BF16)
    outs, upds, decays = [], [], []
    for n, rs in enumerate(chunks):
        sc = jnp.where(mask, _dot_nt(qtb[rs], ktb[rs]), 0.0).astype(BF16)
        outs.append(_dot(sc, vb[rs]))
        kbar = (kt[rs] * jnp.exp(totals[n] - refs[n])).astype(BF16)
        upds.append(_dot_tn(vb[rs], kbar))
        decays.append(jnp.exp(totals[n]))
    for n in (range(nchunk - 1, -1, -1) if reverse else range(nchunk)):
        qi = (qt[chunks[n]] * jnp.exp(refs[n])).astype(BF16)
        outs[n] = outs[n] + _dot_nt(qi, st_t.astype(BF16))
        st_t = st_t * decays[n] + upds[n]
    return jnp.concatenate(outs, axis=0), st_t


HGRN_HEADS_PER_STEP = 4


def _hgrn_kernel(*refs, reverse, layer):
    if reverse:
        (ql, fl, il, gl, ofl, qc, fc, ic, gc, ofc, lg_ref, gn_ref, ol_ref, oc_ref, st_ref) = refs
    else:
        (ql, fl, il, qc, fc, ic, lg_ref, ol_ref, oc_ref, st_ref) = refs
    is_ctx = pl.program_id(2) == 0

    @pl.when(is_ctx)
    def _():
        st_ref[...] = jnp.zeros_like(st_ref)

    def pick(c_ref, l_ref, hh):
        return jnp.where(is_ctx, c_ref[0, hh].astype(F32), l_ref[0, hh].astype(F32))

    for hh in range(HGRN_HEADS_PER_STEP):
        cs = slice(hh * HGRN_HEAD_DIM, (hh + 1) * HGRN_HEAD_DIM)
        lg = lg_ref[:, cs]
        e = jnp.exp(lg - jnp.max(lg, axis=0, keepdims=True))
        sm = e / jnp.sum(e, axis=0, keepdims=True)
        lb = jnp.zeros((1, HGRN_HEAD_DIM), F32)
        for l in range(1, layer + 1):
            lb = lb + sm[l:l + 1]
        q = _silu(pick(qc, ql, hh))
        f = lb + (1.0 - lb) * _sigmoid(pick(fc, fl, hh))
        o, st_new = _hgrn_head(q, f, pick(ic, il, hh), st_ref[hh], reverse)
        st_ref[hh] = st_new
        if reverse:
            y = (_rms(pick(ofc, ofl, hh) + o) * gn_ref[...]) * _silu(pick(gc, gl, hh))
            y = y.astype(ol_ref.dtype)

            @pl.when(is_ctx)
            def _():
                oc_ref[0, :, cs] = y

            @pl.when(jnp.logical_not(is_ctx))
            def _():
                ol_ref[0, :, cs] = y
        else:
            ob = o.astype(ol_ref.dtype)

            @pl.when(is_ctx)
            def _():
                oc_ref[0, hh] = ob

            @pl.when(jnp.logical_not(is_ctx))
            def _():
                ol_ref[0, hh] = ob


def _hgrn_scan(p_lat, p_ctx, logits_d, layer, gn=None, of_lat=None, of_ctx=None):
    reverse = of_lat is not None
    b, _, s_len, hd = p_lat.shape
    h, hps = HGRN_HEADS, HGRN_HEADS_PER_STEP
    tt = p_ctx.shape[2]
    assert s_len % tt == 0 and tt % GLA_CHUNK == 0 and h % hps == 0
    nt = s_len // tt
    ng = h // hps
    fsel = 2 if reverse else 1

    def lat_tile(si):
        return (nt - jnp.maximum(si, 1)) if reverse else jnp.maximum(si - 1, 0)

    lat = lambda grp: pl.BlockSpec((1, hps, tt, hd), lambda bi, hi, si: (bi, grp * ng + hi, lat_tile(si), 0))
    ctx = lambda grp: pl.BlockSpec((1, hps, tt, hd), lambda bi, hi, si: (bi, grp * ng + hi, 0, 0))
    lg_spec = pl.BlockSpec((logits_d.shape[0], hps * hd), lambda bi, hi, si: (0, hi))
    if reverse:
        args = [p_lat, p_lat, p_lat, p_lat, of_lat, p_ctx, p_ctx, p_ctx, p_ctx, of_ctx, logits_d, gn]
        in_specs = [lat(0), lat(fsel), lat(3), lat(4), lat(0), ctx(0), ctx(fsel), ctx(3), ctx(4), ctx(0),
                    lg_spec, pl.BlockSpec((1, hd), lambda bi, hi, si: (0, 0))]
        out_specs = [pl.BlockSpec((1, tt, hps * hd), lambda bi, hi, si: (bi, lat_tile(si), hi)),
                     pl.BlockSpec((1, tt, hps * hd), lambda bi, hi, si: (bi, 0, hi))]
        out_shape = [jax.ShapeDtypeStruct((b, s_len, h * hd), BF16), jax.ShapeDtypeStruct((b, tt, h * hd), BF16)]
    else:
        args = [p_lat, p_lat, p_lat, p_ctx, p_ctx, p_ctx, logits_d]
        in_specs = [lat(0), lat(fsel), lat(3), ctx(0), ctx(fsel), ctx(3), lg_spec]
        out_specs = [lat(0), ctx(0)]
        out_shape = [jax.ShapeDtypeStruct((b, h, s_len, hd), BF16), jax.ShapeDtypeStruct((b, h, tt, hd), BF16)]
    return pl.pallas_call(
        functools.partial(_hgrn_kernel, reverse=reverse, layer=layer),
        grid=(b, ng, nt + 1),
        in_specs=in_specs,
        out_specs=out_specs,
        out_shape=out_shape,
        scratch_shapes=[pltpu.VMEM((hps, hd, hd), F32)],
        compiler_params=_cparams(("parallel", "parallel", "arbitrary"), 32),
        name="hgrn_bwd" if reverse else "hgrn_fwd",
    )(*args)


def _rope_tables(n_lat, rot_dim, reps):
    n_freq = rot_dim // 4
    t = jnp.arange(n_lat)
    inv = ROPE_THETA ** (-jnp.arange(n_freq, dtype=F32) / n_freq)
    ang_r = (t // GRID_W).astype(F32)[:, None] * inv[None, :]
    ang_c = (t % GRID_W).astype(F32)[:, None] * inv[None, :]
    cos = jnp.concatenate([jnp.cos(ang_r)] * 2 + [jnp.cos(ang_c)] * 2, axis=-1)
    sin = jnp.concatenate([-jnp.sin(ang_r), jnp.sin(ang_r), -jnp.sin(ang_c), jnp.sin(ang_c)], axis=-1)
    return jnp.concatenate([jnp.tile(cos, (1, reps)), jnp.tile(sin, (1, reps))], axis=-1)


def _rope(x, cos, sin, n_freq):
    width = x.shape[1]
    lane = lax.broadcasted_iota(jnp.int32, (1, width), 1)
    first = (lane % (2 * n_freq)) < n_freq
    partner = jnp.where(first, pltpu.roll(x, width - n_freq, 1), pltpu.roll(x, n_freq, 1))
    return x * cos + partner * sin


def _mla_qkv_kernel(*refs, use_rope):
    if use_rope:
        (c_ref, qg_ref, kvg_ref, wq_ref, wk_ref, wvt_ref, gqn_ref, gqr_ref, gkn_ref, gkr_ref, tab_ref,
         q_ref, k_ref, vt_ref) = refs
    else:
        (c_ref, qg_ref, kvg_ref, wq_ref, wk_ref, wvt_ref, gqn_ref, gqr_ref, gkn_ref, gkr_ref,
         q_ref, k_ref, vt_ref) = refs
    c = c_ref[0].astype(F32)
    cqn = (_rms(c[:, :MLA_Q_RANK]) * qg_ref[...]).astype(BF16)
    ckvn = (_rms(c[:, MLA_Q_RANK:MLA_Q_RANK + MLA_KV_RANK]) * kvg_ref[...]).astype(BF16)
    kr = c[:, MLA_Q_RANK + MLA_KV_RANK:]
    qa = _dot(cqn, wq_ref[...])
    ka = _dot(ckvn, wk_ref[...])
    vt = _dot_nt(wvt_ref[...], ckvn)
    lane = lax.broadcasted_iota(jnp.int32, (1, LANES), 1)
    low = lane < MLA_ROPE
    if use_rope:
        cos, sin = tab_ref[:, :LANES], tab_ref[:, LANES:]
    inv_rope = 1.0 / MLA_ROPE

    def halves_rms(x):
        sq = x * x
        ss_lo = jnp.sum(jnp.where(low, sq, 0.0), axis=-1, keepdims=True)
        ss_hi = jnp.sum(jnp.where(low, 0.0, sq), axis=-1, keepdims=True)
        return x * jnp.where(low, lax.rsqrt(ss_lo * inv_rope + NORM_EPS), lax.rsqrt(ss_hi * inv_rope + NORM_EPS))

    krn = halves_rms(kr) * gkr_ref[...]
    if use_rope:
        krn = _rope(krn, cos, sin, MLA_ROPE // 4)
    krn = jnp.where(low, krn, 0.0)
    nope_w = MLA_HEADS * MLA_NOPE
    qscale = MLA_SCALE * LOG2E
    for p in range(MLA_HEADS // 2):
        qr = halves_rms(qa[:, nope_w + p * LANES:nope_w + (p + 1) * LANES]) * gqr_ref[...]
        if use_rope:
            qr = _rope(qr, cos, sin, MLA_ROPE // 4)
        for e in range(2):
            h = 2 * p + e
            hs = slice(h * MLA_NOPE, (h + 1) * MLA_NOPE)
            qn = _rms(qa[:, hs]) * gqn_ref[...]
            rot = qr if e == 0 else pltpu.roll(qr, MLA_ROPE, 1)
            rot = jnp.where(low, rot, 0.0)
            q_ref[0, h] = (jnp.concatenate([qn, rot], axis=1) * qscale).astype(BF16)
            kn = _rms(ka[:, hs]) * gkn_ref[...]
            k_ref[0, h] = jnp.concatenate([kn, krn], axis=1).astype(BF16)
            vt_ref[0, h, 0, :MLA_V, :] = vt[hs, :].astype(BF16)
            vt_ref[0, h, 0, MLA_V:, :] = jnp.ones((MLA_V_ROWS - MLA_V, vt.shape[1]), BF16)


def _mla_qkv(c, qg, kvg, wq, wk, wvt, gqn, gqr, gkn, gkr, tab, tm):
    b, r, cw = c.shape
    assert r % tm == 0
    h = MLA_HEADS
    use_rope = tab is not None
    full = lambda a: pl.BlockSpec(a.shape, lambda bi, i: (0,) * a.ndim)
    args = [c, qg, kvg, wq, wk, wvt, gqn, gqr, gkn, gkr]
    in_specs = [pl.BlockSpec((1, tm, cw), lambda bi, i: (bi, i, 0))] + [full(a) for a in args[1:]]
    if use_rope:
        args.append(tab)
        in_specs.append(pl.BlockSpec((tm, tab.shape[1]), lambda bi, i: (i, 0)))
    return pl.pallas_call(
        functools.partial(_mla_qkv_kernel, use_rope=use_rope),
        grid=(b, r // tm),
        in_specs=in_specs,
        out_specs=[pl.BlockSpec((1, h, tm, MLA_QK_PAD), lambda bi, i: (bi, 0, i, 0)),
                   pl.BlockSpec((1, h, tm, MLA_QK_PAD), lambda bi, i: (bi, 0, i, 0)),
                   pl.BlockSpec((1, h, 1, MLA_V_ROWS, tm), lambda bi, i: (bi, 0, i, 0, 0))],
        out_shape=[jax.ShapeDtypeStruct((b, h, r, MLA_QK_PAD), BF16),
                   jax.ShapeDtypeStruct((b, h, r, MLA_QK_PAD), BF16),
                   jax.ShapeDtypeStruct((b, h, r // tm, MLA_V_ROWS, tm), BF16)],
        compiler_params=_cparams(("parallel", "parallel"), 48),
        name="mla_qkv_rope" if use_rope else "mla_qkv",
    )(*args)


MLA_KV_TILES_PER_TRIP = 4


def _attn_update(s, vt, m, acc):
    m_new = jnp.maximum(m, jnp.max(s, axis=0, keepdims=True))
    alpha = jnp.exp2(m - m_new)
    p = jnp.exp2(s - m_new).astype(BF16)
    return m_new, alpha * acc + _dot(vt, p)


def _mla_attn_kernel(*refs, with_lat):
    if with_lat:
        q_ref, kc_ref, vc_ref, kl_ref, vl_ref, o_ref = refs
    else:
        q_ref, kc_ref, vc_ref, o_ref = refs
    q = q_ref[0, 0]
    tq = q.shape[0]
    m = jnp.full((1, tq), -jnp.inf, F32)
    acc = jnp.zeros((vc_ref.shape[3], tq), F32)
    s_ctx = _dot_nt(kc_ref[0, 0], q)
    if with_lat:
        nk, tk = vl_ref.shape[2], vl_ref.shape[4]

        def scores(j):
            return _dot_nt(kl_ref[0, 0, pl.ds(pl.multiple_of(j * tk, tk), tk), :], q)

        unroll = min(MLA_KV_TILES_PER_TRIP, nk)
        s_cur = scores(0)
        m, acc = _attn_update(s_ctx, vc_ref[0, 0, 0], m, acc)

        def body(i, carry):
            s_cur, m, acc = carry
            for u in range(unroll):
                t = unroll * i + u
                s_next = scores(t + 1)
                m, acc = _attn_update(s_cur, vl_ref[0, 0, t], m, acc)
                s_cur = s_next
            return s_cur, m, acc

        s_cur, m, acc = lax.fori_loop(0, nk // unroll - 1, body, (s_cur, m, acc))
        for t in range(nk - unroll, nk):
            s_next = scores(t + 1) if t + 1 < nk else None
            m, acc = _attn_update(s_cur, vl_ref[0, 0, t], m, acc)
            s_cur = s_next
    else:
        m, acc = _attn_update(s_ctx, vc_ref[0, 0, 0], m, acc)
    o = acc[:MLA_V] / acc[MLA_V:MLA_V + 1]
    o_ref[0] = o.T.astype(o_ref.dtype)


def _mla_attention(q, k_ctx, vt_ctx, k_lat=None, vt_lat=None):
    b, h, n, dq = q.shape
    with_lat = k_lat is not None
    tq = min(n, 512)
    rc = k_ctx.shape[2]
    args = [q, k_ctx, vt_ctx]
    in_specs = [pl.BlockSpec((1, 1, tq, dq), lambda bi, hi, i: (bi, hi, i, 0)),
                pl.BlockSpec((1, 1, rc, dq), lambda bi, hi, i: (bi, hi, 0, 0)),
                pl.BlockSpec((1, 1, 1, MLA_V_ROWS, rc), lambda bi, hi, i: (bi, hi, 0, 0, 0))]
    if with_lat:
        nk, tk = vt_lat.shape[2], vt_lat.shape[4]
        assert nk % min(MLA_KV_TILES_PER_TRIP, nk) == 0
        args += [k_lat, vt_lat]
        in_specs += [pl.BlockSpec((1, 1, nk * tk, dq), lambda bi, hi, i: (bi, hi, 0, 0)),
                     pl.BlockSpec((1, 1, nk, MLA_V_ROWS, tk), lambda bi, hi, i: (bi, hi, 0, 0, 0))]
    return pl.pallas_call(
        functools.partial(_mla_attn_kernel, with_lat=with_lat),
        grid=(b, h, n // tq),
        in_specs=in_specs,
        out_specs=pl.BlockSpec((1, tq, MLA_V), lambda bi, hi, i: (bi, i, hi)),
        out_shape=jax.ShapeDtypeStruct((b, n, h * MLA_V), BF16),
        compiler_params=_cparams(("parallel", "parallel", "arbitrary"), 48),
        name="mla_attn" if with_lat else "mla_attn_ctx",
    )(*args)


def _swa_inproj_kernel(*refs, use_rope, n_norm_tiles):
    if use_rope:
        x_ref, g_ref, sh_ref, sc_ref, w_ref, hg_ref, tab_ref, o_ref, h_ref = refs
    else:
        x_ref, g_ref, sh_ref, sc_ref, w_ref, hg_ref, o_ref, h_ref = refs
    j = pl.program_id(2)

    @pl.when(j == 0)
    def _():
        h_ref[...] = _adaln(x_ref[0], g_ref[...], sh_ref[0], sc_ref[0]).astype(BF16)

    res = _dot(h_ref[...], w_ref[...])

    @pl.when(j < n_norm_tiles)
    def _():
        for e in range(res.shape[1] // SWA_HEAD_DIM):
            cs = slice(e * SWA_HEAD_DIM, (e + 1) * SWA_HEAD_DIM)
            xh = _rms(res[:, cs]) * hg_ref[:, cs]
            if use_rope:
                xh = _rope(xh, tab_ref[:, :SWA_HEAD_DIM], tab_ref[:, SWA_HEAD_DIM:], SWA_HEAD_DIM // 4)
            o_ref[0, :, cs] = xh.astype(o_ref.dtype)

    @pl.when(j >= n_norm_tiles)
    def _():
        o_ref[0] = res.astype(o_ref.dtype)


def _swa_inproj(x, g, shift, scale, w, head_gain, tab):
    b, r, d = x.shape
    n = w.shape[1]
    tm = min(r, 512)
    tn = 512
    use_rope = tab is not None
    n_norm_tiles = (SWA_Q_HEADS + SWA_KV_HEADS) * SWA_HEAD_DIM // tn
    vec = pl.BlockSpec((1, 1, d), lambda bi, i, j: (bi, 0, 0))
    args = [x, g, shift, scale, w, head_gain]
    in_specs = [pl.BlockSpec((1, tm, d), lambda bi, i, j: (bi, i, 0)),
                pl.BlockSpec((1, d), lambda bi, i, j: (0, 0)),
                vec, vec,
                pl.BlockSpec((d, tn), lambda bi, i, j: (0, j)),
                pl.BlockSpec((1, tn), lambda bi, i, j: (0, j))]
    if use_rope:
        args.append(tab)
        in_specs.append(pl.BlockSpec((tm, tab.shape[1]), lambda bi, i, j: (i, 0)))
    return pl.pallas_call(
        functools.partial(_swa_inproj_kernel, use_rope=use_rope, n_norm_tiles=n_norm_tiles),
        grid=(b, r // tm, n // tn),
        in_specs=in_specs,
        out_specs=pl.BlockSpec((1, tm, tn), lambda bi, i, j: (bi, i, j)),
        out_shape=jax.ShapeDtypeStruct((b, r, n), BF16),
        scratch_shapes=[pltpu.VMEM((tm, d), BF16)],
        compiler_params=_cparams(("parallel", "parallel", "arbitrary"), 48),
        name="swa_inproj_rope" if use_rope else "swa_inproj",
    )(*args)


def _swa_attn_kernel(sink_ref, q_ref, kp_ref, kc_ref, kn_ref, vp_ref, vc_ref, vn_ref, kx_ref, vx_ref, o_ref):
    hk = pl.program_id(1)
    j = pl.program_id(2)
    nb = pl.num_programs(2)
    blk, grp = SWA_BLOCK, SWA_GROUP
    q = q_ref[0]
    qs = jnp.concatenate([q[:, e * SWA_HEAD_DIM:(e + 1) * SWA_HEAD_DIM] for e in range(grp)], axis=0)
    nctx = kx_ref.shape[1]
    keys = jnp.concatenate([kx_ref[0], kp_ref[0], kc_ref[0], kn_ref[0]], axis=0)
    vals = jnp.concatenate([vx_ref[0], vp_ref[0], vc_ref[0], vn_ref[0]], axis=0)
    s = _dot_nt(qs, keys)
    rows = s.shape[0]
    r_i = lax.broadcasted_iota(jnp.int32, (rows, blk), 0) % blk
    c_i = lax.broadcasted_iota(jnp.int32, (rows, blk), 1)
    off_prev = jnp.where(j > 0, 0, 2 * blk)
    off_next = jnp.where(j < nb - 1, 0, 2 * blk)
    s_prev = jnp.where(c_i >= r_i + off_prev, s[:, nctx:nctx + blk], MASK_VALUE)
    s_next = jnp.where(c_i + off_next <= r_i, s[:, nctx + 2 * blk:], MASK_VALUE)
    s = jnp.concatenate([s[:, :nctx], s_prev, s[:, nctx + blk:nctx + 2 * blk], s_next], axis=1)
    row1 = lax.broadcasted_iota(jnp.int32, (rows, 1), 0)
    sink = jnp.zeros((rows, 1), F32)
    for e in range(grp):
        sink = jnp.where(row1 // blk == e, sink_ref[hk * grp + e] * LOG2E, sink)
    m = jnp.maximum(jnp.max(s, axis=1, keepdims=True), sink)
    p = jnp.exp2(s - m)
    l = jnp.sum(p, axis=1, keepdims=True) + jnp.exp2(sink - m)
    o = _dot(p.astype(BF16), vals) / l
    o_ref[0] = jnp.concatenate([o[e * blk:(e + 1) * blk] for e in range(grp)], axis=1).astype(o_ref.dtype)


def _swa_attention(qkv_lat, qkv_ctx, sink):
    b, n, _ = qkv_lat.shape
    nctx = qkv_ctx.shape[1]
    blk, dh, grp = SWA_BLOCK, SWA_HEAD_DIM, SWA_GROUP
    nb = n // blk
    kcol = SWA_Q_HEADS
    vcol = SWA_Q_HEADS + SWA_KV_HEADS
    prev = lambda j: jnp.maximum(j - 1, 0)
    nxt = lambda j: jnp.minimum(j + 1, nb - 1)
    lat = lambda col0, rowf: pl.BlockSpec((1, blk, dh), lambda bi, hk, j: (bi, rowf(j), col0 + hk))
    ctx = lambda col0: pl.BlockSpec((1, nctx, dh), lambda bi, hk, j: (bi, 0, col0 + hk))
    same = lambda j: j
    return pl.pallas_call(
        _swa_attn_kernel,
        grid=(b, SWA_KV_HEADS, nb),
        in_specs=[pl.BlockSpec(memory_space=pltpu.SMEM),
                  pl.BlockSpec((1, blk, grp * dh), lambda bi, hk, j: (bi, j, hk)),
                  lat(kcol, prev), lat(kcol, same), lat(kcol, nxt),
                  lat(vcol, prev), lat(vcol, same), lat(vcol, nxt),
                  ctx(kcol), ctx(vcol)],
        out_specs=pl.BlockSpec((1, blk, grp * dh), lambda bi, hk, j: (bi, j, hk)),
        out_shape=jax.ShapeDtypeStruct((b, n, SWA_Q_HEADS * dh), BF16),
        compiler_params=_cparams(("parallel", "parallel", "arbitrary"), 32),
        name="swa_attn",
    )(sink, qkv_lat, qkv_lat, qkv_lat, qkv_lat, qkv_lat, qkv_lat, qkv_lat, qkv_ctx, qkv_ctx)


def kernel(x, c, ctx, c_ctx, mod_w, mod_b, norm_g, ffn_w_in, ffn_w_out, rglru_w_in, rglru_conv_w, rglru_conv_b, rglru_gate_w, rglru_gate_b, rglru_lambda, rglru_w_out, hgrn_w_in, hgrn_lb_logits, hgrn_gnorm_g, hgrn_w_out, mla_w_in, mla_q_norm_g, mla_kv_norm_g, mla_w_uq, mla_w_ukv, mla_qk_g, mla_w_out, swa_w_in, swa_qk_g, swa_sink, swa_w_out):
    b, n_lat, d = x.shape
    depth = mod_w.shape[0]
    assert depth == DEPTH and b + 1 <= 8
    bf = lambda a: a.astype(BF16)

    cond = jnp.concatenate([c, c_ctx[None, :], jnp.zeros((8 - b - 1, d), F32)], axis=0)
    mod = _modulation(cond, mod_w, mod_b).reshape(depth, 8, 6, d)

    x_lat, x_ctx = x, ctx
    for layer in range(depth):
        kind = layer % 4
        need_ctx = layer < depth - 1
        m_lat = [mod[layer, :b, k][:, None, :] for k in range(6)]
        m_ctx = [jnp.broadcast_to(mod[layer, b:b + 1, k][:, None, :], (b, 1, d)) for k in range(6)]
        g1 = norm_g[layer, 0][None, :]
        g2 = norm_g[layer, 1][None, :]
        y_ctx = None
        if kind == 0:
            w_in = bf(rglru_w_in[0])
            gr_lat = _inproj(x_lat, g1, m_lat[0], m_lat[1], w_in, tn=1024, name="rglru_inproj")
            gr_ctx = _inproj(x_ctx, g1, m_ctx[0], m_ctx[1], w_in, tn=1024, name="rglru_inproj_ctx")
            gw = rglru_gate_w[0]
            wg = [bf(jnp.concatenate([gw[dr, 0], gw[dr, 1]], axis=-1)) for dr in range(2)]
            cw, cb = rglru_conv_w[0], rglru_conv_b[0][None, :]
            scan = lambda dr, **kw: _rglru_scan(gr_lat, gr_ctx, wg[dr], rglru_gate_b[0, dr],
                                                rglru_lambda[0, dr][None, :], cw, cb, **kw)
            hf_lat, hf_ctx = scan(0)
            y_lat, y_ctx = scan(1, hf_lat=hf_lat, hf_ctx=hf_ctx)
            w_out = bf(rglru_w_out[0])
        elif kind == 1:
            w_in = bf(hgrn_w_in[0])
            p_lat = _inproj(x_lat, g1, m_lat[0], m_lat[1], w_in, tn=1024, head_major=True, name="hgrn_inproj")
            p_ctx = _inproj(x_ctx, g1, m_ctx[0], m_ctx[1], w_in, tn=1024, head_major=True, name="hgrn_inproj_ctx")
            of_lat, of_ctx = _hgrn_scan(p_lat, p_ctx, hgrn_lb_logits[:, 0, :], layer)
            y_lat, y_ctx = _hgrn_scan(p_lat, p_ctx, hgrn_lb_logits[:, 1, :], layer, gn=hgrn_gnorm_g[0][None, :],
                                      of_lat=of_lat, of_ctx=of_ctx)
            w_out = bf(hgrn_w_out[0])
        elif kind == 2:
            cw_real = mla_w_in.shape[2]
            cw_pad = -(-cw_real // LANES) * LANES
            w_in = bf(jnp.pad(mla_w_in[0], ((0, 0), (0, cw_pad - cw_real))))
            c_lat = _inproj(x_lat, g1, m_lat[0], m_lat[1], w_in, tn=cw_pad, name="mla_inproj")
            c_ctx_ = _inproj(x_ctx, g1, m_ctx[0], m_ctx[1], w_in, tn=cw_pad, name="mla_inproj_ctx")
            wq3 = mla_w_uq[0].reshape(MLA_Q_RANK, MLA_HEADS, MLA_NOPE + MLA_ROPE)
            wq = bf(jnp.concatenate([wq3[:, :, :MLA_NOPE].reshape(MLA_Q_RANK, -1),
                                     wq3[:, :, MLA_NOPE:].reshape(MLA_Q_RANK, -1)], axis=1))
            wkv3 = mla_w_ukv[0].reshape(MLA_KV_RANK, MLA_HEADS, MLA_NOPE + MLA_V)
            wk = bf(wkv3[:, :, :MLA_NOPE].reshape(MLA_KV_RANK, -1))
            wvt = bf(wkv3[:, :, MLA_NOPE:].reshape(MLA_KV_RANK, -1).T)
            qk_g = mla_qk_g[0]
            gqn, gkn = qk_g[0:1, :MLA_NOPE], qk_g[1:2, :MLA_NOPE]
            gqr = jnp.tile(qk_g[0:1, MLA_NOPE:], (1, LANES // MLA_ROPE))
            gkr = jnp.pad(qk_g[1:2, MLA_NOPE:], ((0, 0), (0, LANES - MLA_ROPE)))
            tab = _rope_tables(n_lat, MLA_ROPE, LANES // MLA_ROPE)
            small = (mla_q_norm_g[0][None, :], mla_kv_norm_g[0][None, :], wq, wk, wvt, gqn, gqr, gkn, gkr)
            q_l, k_l, vt_l = _mla_qkv(c_lat, *small, tab, min(n_lat, 512))
            q_c, k_c, vt_c = _mla_qkv(c_ctx_, *small, None, c_ctx_.shape[1])
            y_lat = _mla_attention(q_l, k_c, vt_c, k_l, vt_l)
            if need_ctx:
                y_ctx = _mla_attention(q_c, k_c, vt_c)
            w_out = bf(mla_w_out[0])
        else:
            w_in = bf(swa_w_in[0])
            gq = jnp.tile(swa_qk_g[0, 0] * (SWA_SCALE * LOG2E), SWA_Q_HEADS)
            gk = jnp.tile(swa_qk_g[0, 1], SWA_KV_HEADS)
            head_gain = jnp.concatenate([gq, gk, jnp.ones((SWA_KV_HEADS * SWA_HEAD_DIM,), F32)])[None, :]
            tab = _rope_tables(n_lat, SWA_HEAD_DIM, 1)
            qkv_lat = _swa_inproj(x_lat, g1, m_lat[0], m_lat[1], w_in, head_gain, tab)
            qkv_ctx = _swa_inproj(x_ctx, g1, m_ctx[0], m_ctx[1], w_in, head_gain, None)
            y_lat = _swa_attention(qkv_lat, qkv_ctx, swa_sink[0])
            assert not need_ctx
            w_out = bf(swa_w_out[0])

        w_ffn_in, w_ffn_out = bf(ffn_w_in[layer]), bf(ffn_w_out[layer])
        x_lat = _outproj(y_lat, w_out, x_lat, m_lat[2])
        x_lat = _ffn(x_lat, g2, m_lat[3], m_lat[4], m_lat[5], w_ffn_in, w_ffn_out)
        if need_ctx:
            x_ctx = _outproj(y_ctx, w_out, x_ctx, m_ctx[2], name="outproj_ctx")
            x_ctx = _ffn(x_ctx, g2, m_ctx[3], m_ctx[4], m_ctx[5], w_ffn_in, w_ffn_out, name="ffn_ctx")
    return x_lat
```

```python
import functools
import math

import jax
import jax.numpy as jnp
from jax import lax
from jax.experimental import pallas as pl
from jax.experimental.pallas import tpu as pltpu

F32 = jnp.float32
BF16 = jnp.bfloat16

DEPTH = 4
GRID_W = 64
NORM_EPS = 1e-6
ROPE_THETA = 10000.0
LOG2E = math.log2(math.e)

LRU_BLOCKS = 16
LRU_BLOCK = 128
LRU_C = 8.0
CONV_WIDTH = 4
CONV_PAD_LEFT = 2

HGRN_HEADS = 16
HGRN_HEAD_DIM = 128
GLA_CHUNK = 64

MLA_HEADS = 16
MLA_Q_RANK = 512
MLA_KV_RANK = 512
MLA_NOPE = 128
MLA_ROPE = 64
MLA_V = 128
MLA_SCALE = (MLA_NOPE + MLA_ROPE) ** -0.5
MLA_QK_PAD = 256
MLA_V_ROWS = MLA_V + 16

SWA_Q_HEADS = 16
SWA_KV_HEADS = 4
SWA_GROUP = SWA_Q_HEADS // SWA_KV_HEADS
SWA_HEAD_DIM = 128
SWA_WINDOW = 128
SWA_BLOCK = 128
SWA_SCALE = SWA_HEAD_DIM ** -0.5

LANES = 128
MIB = 1024 * 1024
MASK_VALUE = -1e30


def _cparams(semantics, vmem_mib):
    return pltpu.CompilerParams(dimension_semantics=semantics, vmem_limit_bytes=vmem_mib * MIB)


def _sigmoid(x):
    return jax.nn.sigmoid(x)


def _silu(x):
    return x * _sigmoid(x)


def _rms(x, eps=NORM_EPS):
    return x * lax.rsqrt(jnp.mean(x * x, axis=-1, keepdims=True) + eps)


def _adaln(x, g, shift, scale):
    return (_rms(x) * g) * (1.0 + scale) + shift


def _dot(a, b):
    return jnp.dot(a, b, preferred_element_type=F32)


def _dot_nt(a, b):
    return lax.dot_general(a, b, (((1,), (1,)), ((), ())), preferred_element_type=F32)


def _dot_tn(a, b):
    return lax.dot_general(a, b, (((0,), (0,)), ((), ())), preferred_element_type=F32)


def _mod_kernel(c_ref, w_ref, b_ref, o_ref):
    s = _silu(c_ref[...]).astype(BF16)
    o_ref[0] = _dot(s, w_ref[0].astype(BF16)) + b_ref[0]


def _modulation(cond, mod_w, mod_b):
    depth, d, n = mod_w.shape
    tn = 1024
    return pl.pallas_call(
        _mod_kernel,
        grid=(depth, n // tn),
        in_specs=[pl.BlockSpec((8, d), lambda l, j: (0, 0)),
                  pl.BlockSpec((1, d, tn), lambda l, j: (l, 0, j)),
                  pl.BlockSpec((1, 1, tn), lambda l, j: (l, 0, j))],
        out_specs=pl.BlockSpec((1, 8, tn), lambda l, j: (l, 0, j)),
        out_shape=jax.ShapeDtypeStruct((depth, 8, n), F32),
        compiler_params=_cparams(("parallel", "parallel"), 40),
        name="modulation",
    )(cond, mod_w, mod_b.reshape(depth, 1, n))


def _inproj_kernel(x_ref, g_ref, sh_ref, sc_ref, w_ref, o_ref, h_ref, *, head_major):
    @pl.when(pl.program_id(2) == 0)
    def _():
        h_ref[...] = _adaln(x_ref[0], g_ref[...], sh_ref[0], sc_ref[0]).astype(BF16)

    res = _dot(h_ref[...], w_ref[...])
    if head_major:
        for c in range(res.shape[1] // LANES):
            o_ref[0, c] = res[:, c * LANES:(c + 1) * LANES].astype(o_ref.dtype)
    else:
        o_ref[0] = res.astype(o_ref.dtype)


def _inproj(x, g, shift, scale, w, *, tn, head_major=False, name="inproj"):
    b, r, d = x.shape
    n = w.shape[1]
    tm = min(r, 512)
    assert r % tm == 0 and n % tn == 0
    if head_major:
        out_shape = jax.ShapeDtypeStruct((b, n // LANES, r, LANES), BF16)
        out_spec = pl.BlockSpec((1, tn // LANES, tm, LANES), lambda bi, i, j: (bi, j, i, 0))
    else:
        out_shape = jax.ShapeDtypeStruct((b, r, n), BF16)
        out_spec = pl.BlockSpec((1, tm, tn), lambda bi, i, j: (bi, i, j))
    vec = pl.BlockSpec((1, 1, d), lambda bi, i, j: (bi, 0, 0))
    return pl.pallas_call(
        functools.partial(_inproj_kernel, head_major=head_major),
        grid=(b, r // tm, n // tn),
        in_specs=[pl.BlockSpec((1, tm, d), lambda bi, i, j: (bi, i, 0)),
                  pl.BlockSpec((1, d), lambda bi, i, j: (0, 0)),
                  vec, vec,
                  pl.BlockSpec((d, tn), lambda bi, i, j: (0, j))],
        out_specs=out_spec,
        out_shape=out_shape,
        scratch_shapes=[pltpu.VMEM((tm, d), BF16)],
        compiler_params=_cparams(("parallel", "parallel", "arbitrary"), 48),
        name=name,
    )(x, g, shift, scale, w)


def _outproj_kernel(a_ref, w_ref, x_ref, gate_ref, o_ref):
    o_ref[0] = x_ref[0] + gate_ref[0] * _dot(a_ref[0], w_ref[...])


def _outproj(a, w, x, gate, name="outproj"):
    b, r, k = a.shape
    d = w.shape[1]
    tm = min(r, 512)
    return pl.pallas_call(
        _outproj_kernel,
        grid=(b, r // tm),
        in_specs=[pl.BlockSpec((1, tm, k), lambda bi, i: (bi, i, 0)),
                  pl.BlockSpec((k, d), lambda bi, i: (0, 0)),
                  pl.BlockSpec((1, tm, d), lambda bi, i: (bi, i, 0)),
                  pl.BlockSpec((1, 1, d), lambda bi, i: (bi, 0, 0))],
        out_specs=pl.BlockSpec((1, tm, d), lambda bi, i: (bi, i, 0)),
        out_shape=jax.ShapeDtypeStruct((b, r, d), F32),
        compiler_params=_cparams(("parallel", "parallel"), 48),
        name=name,
    )(a, w, x, gate)


def _ffn_kernel(x_ref, g_ref, sh_ref, sc_ref, gate_ref, wg_ref, wu_ref, wo_ref, o_ref, f_ref, acc_ref):
    j = pl.program_id(2)

    @pl.when(j == 0)
    def _():
        f_ref[...] = _adaln(x_ref[0], g_ref[...], sh_ref[0], sc_ref[0]).astype(BF16)
        acc_ref[...] = jnp.zeros_like(acc_ref)

    f = f_ref[...]
    gt = _dot(f, wg_ref[...])
    up = _dot(f, wu_ref[...])
    act = (_silu(gt) * up).astype(BF16)
    acc_ref[...] += _dot(act, wo_ref[...])

    @pl.when(j == pl.num_programs(2) - 1)
    def _():
        o_ref[0] = x_ref[0] + gate_ref[0] * acc_ref[...]


def _ffn(x, g, shift, scale, gate, w_in, w_out, name="ffn"):
    b, r, d = x.shape
    hidden = w_out.shape[0]
    tm = min(r, 512)
    tf = 512
    nf = hidden // tf
    assert hidden % tf == 0
    vec = pl.BlockSpec((1, 1, d), lambda bi, i, j: (bi, 0, 0))
    return pl.pallas_call(
        _ffn_kernel,
        grid=(b, r // tm, nf),
        in_specs=[pl.BlockSpec((1, tm, d), lambda bi, i, j: (bi, i, 0)),
                  pl.BlockSpec((1, d), lambda bi, i, j: (0, 0)),
                  vec, vec, vec,
                  pl.BlockSpec((d, tf), lambda bi, i, j: (0, j)),
                  pl.BlockSpec((d, tf), lambda bi, i, j: (0, j + nf)),
                  pl.BlockSpec((tf, d), lambda bi, i, j: (j, 0))],
        out_specs=pl.BlockSpec((1, tm, d), lambda bi, i, j: (bi, i, 0)),
        out_shape=jax.ShapeDtypeStruct((b, r, d), F32),
        scratch_shapes=[pltpu.VMEM((tm, d), BF16), pltpu.VMEM((tm, d), F32)],
        compiler_params=_cparams(("parallel", "parallel", "arbitrary"), 48),
        name=name,
    )(x, g, shift, scale, gate, w_in, w_in, w_out)


def _softplus(x):
    return jnp.maximum(x, 0.0) + jnp.log1p(jnp.exp(-jnp.abs(x)))


def _gelu_tanh(x):
    return 0.5 * x * (1.0 + jnp.tanh(math.sqrt(2.0 / math.pi) * (x + 0.044715 * (x * x * x))))


def _rglru_tile(rec, prev8, next8, wg_ref, gb_ref, lam_ref, cw_ref, cb_ref,
                ubuf, a_s, b_s, h_s, hcar, *, reverse):
    tt = rec.shape[0]
    ubuf[0:8, :] = prev8
    ubuf[8:8 + tt, :] = rec
    ubuf[8 + tt:16 + tt, :] = next8
    base = 8 - CONV_PAD_LEFT
    u = cb_ref[...] + cw_ref[0:1, :] * ubuf[base:base + tt, :]
    for j in range(1, CONV_WIDTH):
        u = u + cw_ref[j:j + 1, :] * ubuf[base + j:base + j + tt, :]
    sp = _softplus(-lam_ref[...])
    for k in range(LRU_BLOCKS):
        cs = slice(k * LRU_BLOCK, (k + 1) * LRU_BLOCK)
        uk = u[:, cs]
        gts = _dot(uk.astype(BF16), wg_ref[k])
        r = _sigmoid(gts[:, :LRU_BLOCK] + gb_ref[0:1, cs])
        i = _sigmoid(gts[:, LRU_BLOCK:] + gb_ref[1:2, cs])
        log_a = (-LRU_C) * r * sp[:, cs]
        a = jnp.exp(log_a)
        a_s[:, cs] = a
        b_s[:, cs] = jnp.sqrt(-jnp.tanh(log_a) * (a * a + 1.0)) * (i * uk)

    def body(t, h):
        row = (tt - 1 - t) if reverse else t
        h = a_s[pl.ds(row, 1), :] * h + b_s[pl.ds(row, 1), :]
        h_s[pl.ds(row, 1), :] = h
        return h

    hcar[...] = lax.fori_loop(0, tt, body, hcar[...], unroll=8)


def _rglru_kernel(*refs, reverse, tt, nt):
    if reverse:
        (gl_ref, rl_ref, pv_ref, nx_ref, gc_ref, rc_ref, hfl_ref, hfc_ref,
         wg_ref, gb_ref, lam_ref, cw_ref, cb_ref, ol_ref, oc_ref, ubuf, a_s, b_s, h_s, hcar) = refs
    else:
        (rl_ref, pv_ref, nx_ref, rc_ref,
         wg_ref, gb_ref, lam_ref, cw_ref, cb_ref, ol_ref, oc_ref, ubuf, a_s, b_s, h_s, hcar) = refs
    s = pl.program_id(1)
    tile = functools.partial(_rglru_tile, wg_ref=wg_ref, gb_ref=gb_ref, lam_ref=lam_ref, cw_ref=cw_ref,
                             cb_ref=cb_ref, ubuf=ubuf, a_s=a_s, b_s=b_s, h_s=h_s, hcar=hcar, reverse=reverse)
    zeros8 = jnp.zeros((8, rl_ref.shape[2]), F32)

    @pl.when(s == 0)
    def _():
        hcar[...] = jnp.zeros_like(hcar)
        tile(rc_ref[0].astype(F32), zeros8, zeros8)
        if reverse:
            oc_ref[0] = (_gelu_tanh(gc_ref[0].astype(F32)) * (hfc_ref[0].astype(F32) + h_s[...])).astype(oc_ref.dtype)
        else:
            oc_ref[0] = h_s[...].astype(oc_ref.dtype)

    @pl.when(s > 0)
    def _():
        tl = (nt - s) if reverse else (s - 1)
        has_prev = (tl > 0).astype(F32)
        has_next = (tl < nt - 1).astype(F32)
        prev8 = pv_ref[0].astype(F32)[8:16, :] * has_prev
        next8 = nx_ref[0].astype(F32)[0:8, :] * has_next
        tile(rl_ref[0].astype(F32), prev8, next8)
        if reverse:
            ol_ref[0] = (_gelu_tanh(gl_ref[0].astype(F32)) * (hfl_ref[0].astype(F32) + h_s[...])).astype(ol_ref.dtype)
        else:
            ol_ref[0] = h_s[...].astype(ol_ref.dtype)


def _rglru_scan(gr_lat, gr_ctx, wg, gb, lam, cw, cb, hf_lat=None, hf_ctx=None):
    reverse = hf_lat is not None
    b, s_len, w2 = gr_lat.shape
    w = w2 // 2
    tt = gr_ctx.shape[1]
    assert s_len % tt == 0 and tt % 16 == 0
    nt = s_len // tt
    hb = tt // 16

    def lat_tile(si):
        return (nt - jnp.maximum(si, 1)) if reverse else jnp.maximum(si - 1, 0)

    lat_rows = lambda col: pl.BlockSpec((1, tt, w), lambda bi, si: (bi, lat_tile(si), col))
    ctx_rows = lambda col: pl.BlockSpec((1, tt, w), lambda bi, si: (bi, 0, col))
    prev_spec = pl.BlockSpec((1, 16, w), lambda bi, si: (bi, jnp.maximum(lat_tile(si) * hb - 1, 0), 1))
    next_spec = pl.BlockSpec((1, 16, w), lambda bi, si: (bi, jnp.minimum((lat_tile(si) + 1) * hb, nt * hb - 1), 1))
    full = lambda shape: pl.BlockSpec(shape, lambda bi, si: (0,) * len(shape))
    params = [wg, gb, lam, cw, cb]
    param_specs = [full(wg.shape), full(gb.shape), full(lam.shape), full(cw.shape), full(cb.shape)]
    if reverse:
        args = [gr_lat, gr_lat, gr_lat, gr_lat, gr_ctx, gr_ctx, hf_lat, hf_ctx] + params
        in_specs = [lat_rows(0), lat_rows(1), prev_spec, next_spec, ctx_rows(0), ctx_rows(1),
                    lat_rows(0), ctx_rows(0)] + param_specs
    else:
        args = [gr_lat, gr_lat, gr_lat, gr_ctx] + params
        in_specs = [lat_rows(1), prev_spec, next_spec, ctx_rows(1)] + param_specs
    return pl.pallas_call(
        functools.partial(_rglru_kernel, reverse=reverse, tt=tt, nt=nt),
        grid=(b, nt + 1),
        in_specs=in_specs,
        out_specs=[lat_rows(0), ctx_rows(0)],
        out_shape=[jax.ShapeDtypeStruct((b, s_len, w), BF16), jax.ShapeDtypeStruct((b, tt, w), BF16)],
        scratch_shapes=[pltpu.VMEM((tt + 16, w), F32), pltpu.VMEM((tt, w), F32), pltpu.VMEM((tt, w), F32),
                        pltpu.VMEM((tt, w), F32), pltpu.VMEM((1, w), F32)],
        compiler_params=_cparams(("parallel", "arbitrary"), 48),
        name="rglru_bwd" if reverse else "rglru_fwd",
    )(*args)


def _hgrn_head(q, f, v, st_t, reverse):
    rows = q.shape[0]
    c = GLA_CHUNK
    half = c // 2
    nchunk = rows // c
    chunks = [slice(n * c, (n + 1) * c) for n in range(nchunk)]
    r_i = lax.broadcasted_iota(jnp.int32, (c, c), 0)
    c_i = lax.broadcasted_iota(jnp.int32, (c, c), 1)
    mask = (c_i >= r_i) if reverse else (c_i <= r_i)
    tri = jnp.where(mask, 1.0, 0.0).astype(BF16)
    g = jnp.log(f)
    k = 1.0 - f
    hi = g.astype(BF16)
    r1 = g - hi.astype(F32)
    mid = r1.astype(BF16)
    lo = (r1 - mid.astype(F32)).astype(BF16)
    g3 = jnp.concatenate([hi, mid, lo], axis=1)
    cums = []
    for rs in chunks:
        c3 = _dot(tri, g3[rs])
        cums.append((c3[:, :LANES] + c3[:, LANES:2 * LANES]) + c3[:, 2 * LANES:])
    if reverse:
        totals = [cm[0:1] for cm in cums]
        refs = [cm[half:half + 1] for cm in cums]
    else:
        totals = [cm[c - 1:c] for cm in cums]
        refs = [cm[half - 1:half] for cm in cums]
    cum = jnp.concatenate(cums, axis=0)
    ref_b = jnp.concatenate([jnp.broadcast_to(r, (c, LANES)) for r in refs], axis=0)
    qt = q * jnp.exp(cum - ref_b)
    kt = k * jnp.exp(ref_b - cum)
    qtb, ktb, vb = qt.astype(BF16), kt.astype(BF16), v.astype(BF16)
    outs, upds, decays = [], [], []
    for n, rs in enumerate(chunks):
        sc = jnp.where(mask, _dot_nt(qtb[rs], ktb[rs]), 0.0).astype(BF16)
        outs.append(_dot(sc, vb[rs]))
        kbar = (kt[rs] * jnp.exp(totals[n] - refs[n])).astype(BF16)
        upds.append(_dot_tn(vb[rs], kbar))
        decays.append(jnp.exp(totals[n]))
    for n in (range(nchunk - 1, -1, -1) if reverse else range(nchunk)):
        qi = (qt[chunks[n]] * jnp.exp(refs[n])).astype(BF16)
        outs[n] = outs[n] + _dot_nt(qi, st_t.astype(BF16))
        st_t = st_t * decays[n] + upds[n]
    return jnp.concatenate(outs, axis=0), st_t


HGRN_HEADS_PER_STEP = 4


def _hgrn_kernel(*refs, reverse, layer):
    if reverse:
        (ql, fl, il, gl, ofl, qc, fc, ic, gc, ofc, lg_ref, gn_ref, ol_ref, oc_ref, st_ref) = refs
    else:
        (ql, fl, il, qc, fc, ic, lg_ref, ol_ref, oc_ref, st_ref) = refs
    is_ctx = pl.program_id(2) == 0

    @pl.when(is_ctx)
    def _():
        st_ref[...] = jnp.zeros_like(st_ref)

    def pick(c_ref, l_ref, hh):
        return jnp.where(is_ctx, c_ref[0, hh].astype(F32), l_ref[0, hh].astype(F32))

    for hh in range(HGRN_HEADS_PER_STEP):
        cs = slice(hh * HGRN_HEAD_DIM, (hh + 1) * HGRN_HEAD_DIM)
        lg = lg_ref[:, cs]
        e = jnp.exp(lg - jnp.max(lg, axis=0, keepdims=True))
        sm = e / jnp.sum(e, axis=0, keepdims=True)
        lb = jnp.zeros((1, HGRN_HEAD_DIM), F32)
        for l in range(1, layer + 1):
            lb = lb + sm[l:l + 1]
        q = _silu(pick(qc, ql, hh))
        f = lb + (1.0 - lb) * _sigmoid(pick(fc, fl, hh))
        o, st_new = _hgrn_head(q, f, pick(ic, il, hh), st_ref[hh], reverse)
        st_ref[hh] = st_new
        if reverse:
            y = (_rms(pick(ofc, ofl, hh) + o) * gn_ref[...]) * _silu(pick(gc, gl, hh))
            y = y.astype(ol_ref.dtype)

            @pl.when(is_ctx)
            def _():
                oc_ref[0, :, cs] = y

            @pl.when(jnp.logical_not(is_ctx))
            def _():
                ol_ref[0, :, cs] = y
        else:
            ob = o.astype(ol_ref.dtype)

            @pl.when(is_ctx)
            def _():
                oc_ref[0, hh] = ob

            @pl.when(jnp.logical_not(is_ctx))
            def _():
                ol_ref[0, hh] = ob


def _hgrn_scan(p_lat, p_ctx, logits_d, layer, gn=None, of_lat=None, of_ctx=None):
    reverse = of_lat is not None
    b, _, s_len, hd = p_lat.shape
    h, hps = HGRN_HEADS, HGRN_HEADS_PER_STEP
    tt = p_ctx.shape[2]
    assert s_len % tt == 0 and tt % GLA_CHUNK == 0 and h % hps == 0
    nt = s_len // tt
    ng = h // hps
    fsel = 2 if reverse else 1

    def lat_tile(si):
        return (nt - jnp.maximum(si, 1)) if reverse else jnp.maximum(si - 1, 0)

    lat = lambda grp: pl.BlockSpec((1, hps, tt, hd), lambda bi, hi, si: (bi, grp * ng + hi, lat_tile(si), 0))
    ctx = lambda grp: pl.BlockSpec((1, hps, tt, hd), lambda bi, hi, si: (bi, grp * ng + hi, 0, 0))
    lg_spec = pl.BlockSpec((logits_d.shape[0], hps * hd), lambda bi, hi, si: (0, hi))
    if reverse:
        args = [p_lat, p_lat, p_lat, p_lat, of_lat, p_ctx, p_ctx, p_ctx, p_ctx, of_ctx, logits_d, gn]
        in_specs = [lat(0), lat(fsel), lat(3), lat(4), lat(0), ctx(0), ctx(fsel), ctx(3), ctx(4), ctx(0),
                    lg_spec, pl.BlockSpec((1, hd), lambda bi, hi, si: (0, 0))]
        out_specs = [pl.BlockSpec((1, tt, hps * hd), lambda bi, hi, si: (bi, lat_tile(si), hi)),
                     pl.BlockSpec((1, tt, hps * hd), lambda bi, hi, si: (bi, 0, hi))]
        out_shape = [jax.ShapeDtypeStruct((b, s_len, h * hd), BF16), jax.ShapeDtypeStruct((b, tt, h * hd), BF16)]
    else:
        args = [p_lat, p_lat, p_lat, p_ctx, p_ctx, p_ctx, logits_d]
        in_specs = [lat(0), lat(fsel), lat(3), ctx(0), ctx(fsel), ctx(3), lg_spec]
        out_specs = [lat(0), ctx(0)]
        out_shape = [jax.ShapeDtypeStruct((b, h, s_len, hd), BF16), jax.ShapeDtypeStruct((b, h, tt, hd), BF16)]
    return pl.pallas_call(
        functools.partial(_hgrn_kernel, reverse=reverse, layer=layer),
        grid=(b, ng, nt + 1),
        in_specs=in_specs,
        out_specs=out_specs,
        out_shape=out_shape,
        scratch_shapes=[pltpu.VMEM((hps, hd, hd), F32)],
        compiler_params=_cparams(("parallel", "parallel", "arbitrary"), 32),
        name="hgrn_bwd" if reverse else "hgrn_fwd",
    )(*args)


def _rope_tables(n_lat, rot_dim, reps):
    n_freq = rot_dim // 4
    t = jnp.arange(n_lat)
    inv = ROPE_THETA ** (-jnp.arange(n_freq, dtype=F32) / n_freq)
    ang_r = (t // GRID_W).astype(F32)[:, None] * inv[None, :]
    ang_c = (t % GRID_W).astype(F32)[:, None] * inv[None, :]
    cos = jnp.concatenate([jnp.cos(ang_r)] * 2 + [jnp.cos(ang_c)] * 2, axis=-1)
    sin = jnp.concatenate([-jnp.sin(ang_r), jnp.sin(ang_r), -jnp.sin(ang_c), jnp.sin(ang_c)], axis=-1)
    return jnp.concatenate([jnp.tile(cos, (1, reps)), jnp.tile(sin, (1, reps))], axis=-1)


def _rope(x, cos, sin, n_freq):
    width = x.shape[1]
    lane = lax.broadcasted_iota(jnp.int32, (1, width), 1)
    first = (lane % (2 * n_freq)) < n_freq
    partner = jnp.where(first, pltpu.roll(x, width - n_freq, 1), pltpu.roll(x, n_freq, 1))
    return x * cos + partner * sin


def _mla_qkv_kernel(*refs, use_rope):
    if use_rope:
        (c_ref, qg_ref, kvg_ref, wq_ref, wk_ref, wvt_ref, gqn_ref, gqr_ref, gkn_ref, gkr_ref, tab_ref,
         q_ref, k_ref, vt_ref) = refs
    else:
        (c_ref, qg_ref, kvg_ref, wq_ref, wk_ref, wvt_ref, gqn_ref, gqr_ref, gkn_ref, gkr_ref,
         q_ref, k_ref, vt_ref) = refs
    c = c_ref[0].astype(F32)
    cqn = (_rms(c[:, :MLA_Q_RANK]) * qg_ref[...]).astype(BF16)
    ckvn = (_rms(c[:, MLA_Q_RANK:MLA_Q_RANK + MLA_KV_RANK]) * kvg_ref[...]).astype(BF16)
    kr = c[:, MLA_Q_RANK + MLA_KV_RANK:]
    qa = _dot(cqn, wq_ref[...])
    ka = _dot(ckvn, wk_ref[...])
    vt = _dot_nt(wvt_ref[...], ckvn)
    lane = lax.broadcasted_iota(jnp.int32, (1, LANES), 1)
    low = lane < MLA_ROPE
    if use_rope:
        cos, sin = tab_ref[:, :LANES], tab_ref[:, LANES:]
    inv_rope = 1.0 / MLA_ROPE

    def halves_rms(x):
        sq = x * x
        ss_lo = jnp.sum(jnp.where(low, sq, 0.0), axis=-1, keepdims=True)
        ss_hi = jnp.sum(jnp.where(low, 0.0, sq), axis=-1, keepdims=True)
        return x * jnp.where(low, lax.rsqrt(ss_lo * inv_rope + NORM_EPS), lax.rsqrt(ss_hi * inv_rope + NORM_EPS))

    krn = halves_rms(kr) * gkr_ref[...]
    if use_rope:
        krn = _rope(krn, cos, sin, MLA_ROPE // 4)
    krn = jnp.where(low, krn, 0.0)
    nope_w = MLA_HEADS * MLA_NOPE
    qscale = MLA_SCALE * LOG2E
    for p in range(MLA_HEADS // 2):
        qr = halves_rms(qa[:, nope_w + p * LANES:nope_w + (p + 1) * LANES]) * gqr_ref[...]
        if use_rope:
            qr = _rope(qr, cos, sin, MLA_ROPE // 4)
        for e in range(2):
            h = 2 * p + e
            hs = slice(h * MLA_NOPE, (h + 1) * MLA_NOPE)
            qn = _rms(qa[:, hs]) * gqn_ref[...]
            rot = qr if e == 0 else pltpu.roll(qr, MLA_ROPE, 1)
            rot = jnp.where(low, rot, 0.0)
            q_ref[0, h] = (jnp.concatenate([qn, rot], axis=1) * qscale).astype(BF16)
            kn = _rms(ka[:, hs]) * gkn_ref[...]
            k_ref[0, h] = jnp.concatenate([kn, krn], axis=1).astype(BF16)
            vt_ref[0, h, 0, :MLA_V, :] = vt[hs, :].astype(BF16)
            vt_ref[0, h, 0, MLA_V:, :] = jnp.ones((MLA_V_ROWS - MLA_V, vt.shape[1]), BF16)


def _mla_qkv(c, qg, kvg, wq, wk, wvt, gqn, gqr, gkn, gkr, tab, tm):
    b, r, cw = c.shape
    assert r % tm == 0
    h = MLA_HEADS
    use_rope = tab is not None
    full = lambda a: pl.BlockSpec(a.shape, lambda bi, i: (0,) * a.ndim)
    args = [c, qg, kvg, wq, wk, wvt, gqn, gqr, gkn, gkr]
    in_specs = [pl.BlockSpec((1, tm, cw), lambda bi, i: (bi, i, 0))] + [full(a) for a in args[1:]]
    if use_rope:
        args.append(tab)
        in_specs.append(pl.BlockSpec((tm, tab.shape[1]), lambda bi, i: (i, 0)))
    return pl.pallas_call(
        functools.partial(_mla_qkv_kernel, use_rope=use_rope),
        grid=(b, r // tm),
        in_specs=in_specs,
        out_specs=[pl.BlockSpec((1, h, tm, MLA_QK_PAD), lambda bi, i: (bi, 0, i, 0)),
                   pl.BlockSpec((1, h, tm, MLA_QK_PAD), lambda bi, i: (bi, 0, i, 0)),
                   pl.BlockSpec((1, h, 1, MLA_V_ROWS, tm), lambda bi, i: (bi, 0, i, 0, 0))],
        out_shape=[jax.ShapeDtypeStruct((b, h, r, MLA_QK_PAD), BF16),
                   jax.ShapeDtypeStruct((b, h, r, MLA_QK_PAD), BF16),
                   jax.ShapeDtypeStruct((b, h, r // tm, MLA_V_ROWS, tm), BF16)],
        compiler_params=_cparams(("parallel", "parallel"), 48),
        name="mla_qkv_rope" if use_rope else "mla_qkv",
    )(*args)


MLA_KV_TILES_PER_TRIP = 4
MLA_Q_CHUNK = 512
MLA_Q_CHUNKS_PER_STEP = 2


def _attn_update(s, vt, m, acc):
    m_new = jnp.maximum(m, jnp.max(s, axis=0, keepdims=True))
    alpha = jnp.exp2(m - m_new)
    p = jnp.exp2(s - m_new).astype(BF16)
    return m_new, alpha * acc + _dot(vt, p)


MLA_STALE_MARGIN = 100.0
MLA_STALE_BOUND = (126.0 + MLA_STALE_MARGIN) / 2.0


def _attn_update_stale(s, vt, m, acc):
    p = jnp.exp2(s - m).astype(BF16)
    m_new = jnp.maximum(m, jnp.max(s, axis=0, keepdims=True))
    return m_new, (acc + _dot(vt, p)) * jnp.exp2(m - m_new)


def _mla_attn_stale(qs, kc_ref, vc_ref, kl_ref, vl_ref, floor):
    nk, tk = vl_ref.shape[2], vl_ref.shape[4]
    unroll = min(MLA_KV_TILES_PER_TRIP, nk)
    ms, accs = [], []
    for q in qs:
        m = jnp.full((1, q.shape[0]), floor, F32)
        acc = jnp.zeros((vc_ref.shape[3], q.shape[0]), F32)
        m, acc = _attn_update(_dot_nt(kc_ref[0, 0], q), vc_ref[0, 0, 0], m, acc)
        ms.append(m)
        accs.append(acc)

    def body(i, carry):
        ms, accs = list(carry[0]), list(carry[1])
        for u in range(unroll):
            t = unroll * i + u
            k = kl_ref[0, 0, pl.ds(pl.multiple_of(t * tk, tk), tk), :]
            vt = vl_ref[0, 0, t]
            ss = [_dot_nt(k, q) for q in qs]
            for c in range(len(qs)):
                ms[c], accs[c] = _attn_update_stale(ss[c], vt, ms[c], accs[c])
        return tuple(ms), tuple(accs)

    return lax.fori_loop(0, nk // unroll, body, (tuple(ms), tuple(accs)))[1]


def _mla_attn_online(q, kc_ref, vc_ref, kl_ref, vl_ref):
    nk, tk = vl_ref.shape[2], vl_ref.shape[4]
    unroll = min(MLA_KV_TILES_PER_TRIP, nk)
    m = jnp.full((1, q.shape[0]), -jnp.inf, F32)
    acc = jnp.zeros((vc_ref.shape[3], q.shape[0]), F32)

    def scores(j):
        return _dot_nt(kl_ref[0, 0, pl.ds(pl.multiple_of(j * tk, tk), tk), :], q)

    s_cur = scores(0)
    m, acc = _attn_update(_dot_nt(kc_ref[0, 0], q), vc_ref[0, 0, 0], m, acc)

    def body(i, carry):
        s_cur, m, acc = carry
        for u in range(unroll):
            t = unroll * i + u
            s_next = scores(t + 1)
            m, acc = _attn_update(s_cur, vl_ref[0, 0, t], m, acc)
            s_cur = s_next
        return s_cur, m, acc

    s_cur, m, acc = lax.fori_loop(0, nk // unroll - 1, body, (s_cur, m, acc))
    for t in range(nk - unroll, nk):
        s_next = scores(t + 1) if t + 1 < nk else None
        m, acc = _attn_update(s_cur, vl_ref[0, 0, t], m, acc)
        s_cur = s_next
    return acc


def _mla_attn_kernel(*refs, with_lat, chunk):
    if with_lat:
        bound_ref, q_ref, kc_ref, vc_ref, kl_ref, vl_ref, o_ref = refs
    else:
        q_ref, kc_ref, vc_ref, o_ref = refs
    tq = q_ref.shape[2]
    rows = [slice(c * chunk, (c + 1) * chunk) for c in range(tq // chunk)]

    def finish(acc, rs):
        o_ref[0, rs, :] = (acc[:MLA_V] / acc[MLA_V:MLA_V + 1]).T.astype(o_ref.dtype)

    if with_lat:
        bound = bound_ref[0]

        @pl.when(bound < MLA_STALE_BOUND)
        def _():
            accs = _mla_attn_stale([q_ref[0, 0, rs, :] for rs in rows], kc_ref, vc_ref, kl_ref, vl_ref,
                                   bound - MLA_STALE_MARGIN)
            for acc, rs in zip(accs, rows):
                finish(acc, rs)

        @pl.when(bound >= MLA_STALE_BOUND)
        def _():
            for rs in rows:
                finish(_mla_attn_online(q_ref[0, 0, rs, :], kc_ref, vc_ref, kl_ref, vl_ref), rs)
    else:
        for rs in rows:
            q = q_ref[0, 0, rs, :]
            m = jnp.full((1, chunk), -jnp.inf, F32)
            acc = jnp.zeros((vc_ref.shape[3], chunk), F32)
            finish(_attn_update(_dot_nt(kc_ref[0, 0], q), vc_ref[0, 0, 0], m, acc)[1], rs)


def _mla_attention(q, k_ctx, vt_ctx, k_lat=None, vt_lat=None, bound=None):
    b, h, n, dq = q.shape
    with_lat = k_lat is not None
    chunk = min(n, MLA_Q_CHUNK)
    tq = min(n, MLA_Q_CHUNKS_PER_STEP * chunk)
    assert n % tq == 0
    rc = k_ctx.shape[2]
    args = [q, k_ctx, vt_ctx]
    in_specs = [pl.BlockSpec((1, 1, tq, dq), lambda bi, hi, i: (bi, hi, i, 0)),
                pl.BlockSpec((1, 1, rc, dq), lambda bi, hi, i: (bi, hi, 0, 0)),
                pl.BlockSpec((1, 1, 1, MLA_V_ROWS, rc), lambda bi, hi, i: (bi, hi, 0, 0, 0))]
    if with_lat:
        nk, tk = vt_lat.shape[2], vt_lat.shape[4]
        assert nk % min(MLA_KV_TILES_PER_TRIP, nk) == 0
        args = [bound] + args + [k_lat, vt_lat]
        in_specs = [pl.BlockSpec(memory_space=pltpu.SMEM)] + in_specs
        in_specs += [pl.BlockSpec((1, 1, nk * tk, dq), lambda bi, hi, i: (bi, hi, 0, 0)),
                     pl.BlockSpec((1, 1, nk, MLA_V_ROWS, tk), lambda bi, hi, i: (bi, hi, 0, 0, 0))]
    return pl.pallas_call(
        functools.partial(_mla_attn_kernel, with_lat=with_lat, chunk=chunk),
        grid=(b, h, n // tq),
        in_specs=in_specs,
        out_specs=pl.BlockSpec((1, tq, MLA_V), lambda bi, hi, i: (bi, i, hi)),
        out_shape=jax.ShapeDtypeStruct((b, n, h * MLA_V), BF16),
        compiler_params=_cparams(("parallel", "parallel", "arbitrary"), 48),
        name="mla_attn" if with_lat else "mla_attn_ctx",
    )(*args)


def _swa_inproj_kernel(*refs, use_rope, n_norm_tiles):
    if use_rope:
        x_ref, g_ref, sh_ref, sc_ref, w_ref, hg_ref, tab_ref, o_ref, h_ref = refs
    else:
        x_ref, g_ref, sh_ref, sc_ref, w_ref, hg_ref, o_ref, h_ref = refs
    j = pl.program_id(2)

    @pl.when(j == 0)
    def _():
        h_ref[...] = _adaln(x_ref[0], g_ref[...], sh_ref[0], sc_ref[0]).astype(BF16)

    res = _dot(h_ref[...], w_ref[...])

    @pl.when(j < n_norm_tiles)
    def _():
        for e in range(res.shape[1] // SWA_HEAD_DIM):
            cs = slice(e * SWA_HEAD_DIM, (e + 1) * SWA_HEAD_DIM)
            xh = _rms(res[:, cs]) * hg_ref[:, cs]
            if use_rope:
                xh = _rope(xh, tab_ref[:, :SWA_HEAD_DIM], tab_ref[:, SWA_HEAD_DIM:], SWA_HEAD_DIM // 4)
            o_ref[0, :, cs] = xh.astype(o_ref.dtype)

    @pl.when(j >= n_norm_tiles)
    def _():
        o_ref[0] = res.astype(o_ref.dtype)


def _swa_inproj(x, g, shift, scale, w, head_gain, tab):
    b, r, d = x.shape
    n = w.shape[1]
    tm = min(r, 512)
    tn = 512
    use_rope = tab is not None
    n_norm_tiles = (SWA_Q_HEADS + SWA_KV_HEADS) * SWA_HEAD_DIM // tn
    vec = pl.BlockSpec((1, 1, d), lambda bi, i, j: (bi, 0, 0))
    args = [x, g, shift, scale, w, head_gain]
    in_specs = [pl.BlockSpec((1, tm, d), lambda bi, i, j: (bi, i, 0)),
                pl.BlockSpec((1, d), lambda bi, i, j: (0, 0)),
                vec, vec,
                pl.BlockSpec((d, tn), lambda bi, i, j: (0, j)),
                pl.BlockSpec((1, tn), lambda bi, i, j: (0, j))]
    if use_rope:
        args.append(tab)
        in_specs.append(pl.BlockSpec((tm, tab.shape[1]), lambda bi, i, j: (i, 0)))
    return pl.pallas_call(
        functools.partial(_swa_inproj_kernel, use_rope=use_rope, n_norm_tiles=n_norm_tiles),
        grid=(b, r // tm, n // tn),
        in_specs=in_specs,
        out_specs=pl.BlockSpec((1, tm, tn), lambda bi, i, j: (bi, i, j)),
        out_shape=jax.ShapeDtypeStruct((b, r, n), BF16),
        scratch_shapes=[pltpu.VMEM((tm, d), BF16)],
        compiler_params=_cparams(("parallel", "parallel", "arbitrary"), 48),
        name="swa_inproj_rope" if use_rope else "swa_inproj",
    )(*args)


def _swa_attn_kernel(sink_ref, q_ref, kp_ref, kc_ref, kn_ref, vp_ref, vc_ref, vn_ref, kx_ref, vx_ref, o_ref):
    hk = pl.program_id(1)
    j = pl.program_id(2)
    nb = pl.num_programs(2)
    blk, grp = SWA_BLOCK, SWA_GROUP
    q = q_ref[0]
    qs = jnp.concatenate([q[:, e * SWA_HEAD_DIM:(e + 1) * SWA_HEAD_DIM] for e in range(grp)], axis=0)
    nctx = kx_ref.shape[1]
    keys = jnp.concatenate([kx_ref[0], kp_ref[0], kc_ref[0], kn_ref[0]], axis=0)
    vals = jnp.concatenate([vx_ref[0], vp_ref[0], vc_ref[0], vn_ref[0]], axis=0)
    s = _dot_nt(qs, keys)
    rows = s.shape[0]
    r_i = lax.broadcasted_iota(jnp.int32, (rows, blk), 0) % blk
    c_i = lax.broadcasted_iota(jnp.int32, (rows, blk), 1)
    off_prev = jnp.where(j > 0, 0, 2 * blk)
    off_next = jnp.where(j < nb - 1, 0, 2 * blk)
    s_prev = jnp.where(c_i >= r_i + off_prev, s[:, nctx:nctx + blk], MASK_VALUE)
    s_next = jnp.where(c_i + off_next <= r_i, s[:, nctx + 2 * blk:], MASK_VALUE)
    s = jnp.concatenate([s[:, :nctx], s_prev, s[:, nctx + blk:nctx + 2 * blk], s_next], axis=1)
    row1 = lax.broadcasted_iota(jnp.int32, (rows, 1), 0)
    sink = jnp.zeros((rows, 1), F32)
    for e in range(grp):
        sink = jnp.where(row1 // blk == e, sink_ref[hk * grp + e] * LOG2E, sink)
    m = jnp.maximum(jnp.max(s, axis=1, keepdims=True), sink)
    p = jnp.exp2(s - m)
    l = jnp.sum(p, axis=1, keepdims=True) + jnp.exp2(sink - m)
    o = _dot(p.astype(BF16), vals) / l
    o_ref[0] = jnp.concatenate([o[e * blk:(e + 1) * blk] for e in range(grp)], axis=1).astype(o_ref.dtype)


def _swa_attention(qkv_lat, qkv_ctx, sink):
    b, n, _ = qkv_lat.shape
    nctx = qkv_ctx.shape[1]
    blk, dh, grp = SWA_BLOCK, SWA_HEAD_DIM, SWA_GROUP
    nb = n // blk
    kcol = SWA_Q_HEADS
    vcol = SWA_Q_HEADS + SWA_KV_HEADS
    prev = lambda j: jnp.maximum(j - 1, 0)
    nxt = lambda j: jnp.minimum(j + 1, nb - 1)
    lat = lambda col0, rowf: pl.BlockSpec((1, blk, dh), lambda bi, hk, j: (bi, rowf(j), col0 + hk))
    ctx = lambda col0: pl.BlockSpec((1, nctx, dh), lambda bi, hk, j: (bi, 0, col0 + hk))
    same = lambda j: j
    return pl.pallas_call(
        _swa_attn_kernel,
        grid=(b, SWA_KV_HEADS, nb),
        in_specs=[pl.BlockSpec(memory_space=pltpu.SMEM),
                  pl.BlockSpec((1, blk, grp * dh), lambda bi, hk, j: (bi, j, hk)),
                  lat(kcol, prev), lat(kcol, same), lat(kcol, nxt),
                  lat(vcol, prev), lat(vcol, same), lat(vcol, nxt),
                  ctx(kcol), ctx(vcol)],
        out_specs=pl.BlockSpec((1, blk, grp * dh), lambda bi, hk, j: (bi, j, hk)),
        out_shape=jax.ShapeDtypeStruct((b, n, SWA_Q_HEADS * dh), BF16),
        compiler_params=_cparams(("parallel", "parallel", "arbitrary"), 32),
        name="swa_attn",
    )(sink, qkv_lat, qkv_lat, qkv_lat, qkv_lat, qkv_lat, qkv_lat, qkv_lat, qkv_ctx, qkv_ctx)


def kernel(x, c, ctx, c_ctx, mod_w, mod_b, norm_g, ffn_w_in, ffn_w_out, rglru_w_in, rglru_conv_w, rglru_conv_b, rglru_gate_w, rglru_gate_b, rglru_lambda, rglru_w_out, hgrn_w_in, hgrn_lb_logits, hgrn_gnorm_g, hgrn_w_out, mla_w_in, mla_q_norm_g, mla_kv_norm_g, mla_w_uq, mla_w_ukv, mla_qk_g, mla_w_out, swa_w_in, swa_qk_g, swa_sink, swa_w_out):
    b, n_lat, d = x.shape
    depth = mod_w.shape[0]
    assert depth == DEPTH and b + 1 <= 8
    bf = lambda a: a.astype(BF16)

    cond = jnp.concatenate([c, c_ctx[None, :], jnp.zeros((8 - b - 1, d), F32)], axis=0)
    mod = _modulation(cond, mod_w, mod_b).reshape(depth, 8, 6, d)

    x_lat, x_ctx = x, ctx
    for layer in range(depth):
        kind = layer % 4
        need_ctx = layer < depth - 1
        m_lat = [mod[layer, :b, k][:, None, :] for k in range(6)]
        m_ctx = [jnp.broadcast_to(mod[layer, b:b + 1, k][:, None, :], (b, 1, d)) for k in range(6)]
        g1 = norm_g[layer, 0][None, :]
        g2 = norm_g[layer, 1][None, :]
        y_ctx = None
        if kind == 0:
            w_in = bf(rglru_w_in[0])
            gr_lat = _inproj(x_lat, g1, m_lat[0], m_lat[1], w_in, tn=1024, name="rglru_inproj")
            gr_ctx = _inproj(x_ctx, g1, m_ctx[0], m_ctx[1], w_in, tn=1024, name="rglru_inproj_ctx")
            gw = rglru_gate_w[0]
            wg = [bf(jnp.concatenate([gw[dr, 0], gw[dr, 1]], axis=-1)) for dr in range(2)]
            cw, cb = rglru_conv_w[0], rglru_conv_b[0][None, :]
            scan = lambda dr, **kw: _rglru_scan(gr_lat, gr_ctx, wg[dr], rglru_gate_b[0, dr],
                                                rglru_lambda[0, dr][None, :], cw, cb, **kw)
            hf_lat, hf_ctx = scan(0)
            y_lat, y_ctx = scan(1, hf_lat=hf_lat, hf_ctx=hf_ctx)
            w_out = bf(rglru_w_out[0])
        elif kind == 1:
            w_in = bf(hgrn_w_in[0])
            p_lat = _inproj(x_lat, g1, m_lat[0], m_lat[1], w_in, tn=1024, head_major=True, name="hgrn_inproj")
            p_ctx = _inproj(x_ctx, g1, m_ctx[0], m_ctx[1], w_in, tn=1024, head_major=True, name="hgrn_inproj_ctx")
            of_lat, of_ctx = _hgrn_scan(p_lat, p_ctx, hgrn_lb_logits[:, 0, :], layer)
            y_lat, y_ctx = _hgrn_scan(p_lat, p_ctx, hgrn_lb_logits[:, 1, :], layer, gn=hgrn_gnorm_g[0][None, :],
                                      of_lat=of_lat, of_ctx=of_ctx)
            w_out = bf(hgrn_w_out[0])
        elif kind == 2:
            cw_real = mla_w_in.shape[2]
            cw_pad = -(-cw_real // LANES) * LANES
            w_in = bf(jnp.pad(mla_w_in[0], ((0, 0), (0, cw_pad - cw_real))))
            c_lat = _inproj(x_lat, g1, m_lat[0], m_lat[1], w_in, tn=cw_pad, name="mla_inproj")
            c_ctx_ = _inproj(x_ctx, g1, m_ctx[0], m_ctx[1], w_in, tn=cw_pad, name="mla_inproj_ctx")
            wq3 = mla_w_uq[0].reshape(MLA_Q_RANK, MLA_HEADS, MLA_NOPE + MLA_ROPE)
            wq = bf(jnp.concatenate([wq3[:, :, :MLA_NOPE].reshape(MLA_Q_RANK, -1),
                                     wq3[:, :, MLA_NOPE:].reshape(MLA_Q_RANK, -1)], axis=1))
            wkv3 = mla_w_ukv[0].reshape(MLA_KV_RANK, MLA_HEADS, MLA_NOPE + MLA_V)
            wk = bf(wkv3[:, :, :MLA_NOPE].reshape(MLA_KV_RANK, -1))
            wvt = bf(wkv3[:, :, MLA_NOPE:].reshape(MLA_KV_RANK, -1).T)
            qk_g = mla_qk_g[0]
            gqn, gkn = qk_g[0:1, :MLA_NOPE], qk_g[1:2, :MLA_NOPE]
            gqr = jnp.tile(qk_g[0:1, MLA_NOPE:], (1, LANES // MLA_ROPE))
            gkr = jnp.pad(qk_g[1:2, MLA_NOPE:], ((0, 0), (0, LANES - MLA_ROPE)))
            tab = _rope_tables(n_lat, MLA_ROPE, LANES // MLA_ROPE)
            small = (mla_q_norm_g[0][None, :], mla_kv_norm_g[0][None, :], wq, wk, wvt, gqn, gqr, gkn, gkr)
            q_l, k_l, vt_l = _mla_qkv(c_lat, *small, tab, min(n_lat, 512))
            q_c, k_c, vt_c = _mla_qkv(c_ctx_, *small, None, c_ctx_.shape[1])
            gmax2 = lambda g: jnp.max(g * g)
            qk_norm2 = ((MLA_NOPE * gmax2(gqn) + MLA_ROPE * gmax2(gqr))
                        * (MLA_NOPE * gmax2(gkn) + MLA_ROPE * gmax2(gkr)))
            bound = (1.05 * MLA_SCALE * LOG2E) * jnp.sqrt(qk_norm2).reshape(1)
            y_lat = _mla_attention(q_l, k_c, vt_c, k_l, vt_l, bound)
            if need_ctx:
                y_ctx = _mla_attention(q_c, k_c, vt_c)
            w_out = bf(mla_w_out[0])
        else:
            w_in = bf(swa_w_in[0])
            gq = jnp.tile(swa_qk_g[0, 0] * (SWA_SCALE * LOG2E), SWA_Q_HEADS)
            gk = jnp.tile(swa_qk_g[0, 1], SWA_KV_HEADS)
            head_gain = jnp.concatenate([gq, gk, jnp.ones((SWA_KV_HEADS * SWA_HEAD_DIM,), F32)])[None, :]
            tab = _rope_tables(n_lat, SWA_HEAD_DIM, 1)
            qkv_lat = _swa_inproj(x_lat, g1, m_lat[0], m_lat[1], w_in, head_gain, tab)
            qkv_ctx = _swa_inproj(x_ctx, g1, m_ctx[0], m_ctx[1], w_in, head_gain, None)
            y_lat = _swa_attention(qkv_lat, qkv_ctx, swa_sink[0])
            assert not need_ctx
            w_out = bf(swa_w_out[0])

        w_ffn_in, w_ffn_out = bf(ffn_w_in[layer]), bf(ffn_w_out[layer])
        x_lat = _outproj(y_lat, w_out, x_lat, m_lat[2])
        x_lat = _ffn(x_lat, g2, m_lat[3], m_lat[4], m_lat[5], w_ffn_in, w_ffn_out)
        if need_ctx:
            x_ctx = _outproj(y_ctx, w_out, x_ctx, m_ctx[2], name="outproj_ctx")
            x_ctx = _ffn(x_ctx, g2, m_ctx[3], m_ctx[4], m_ctx[5], w_ffn_in, w_ffn_out, name="ffn_ctx")
    return x_lat
```

```python
import functools
import math

import jax
import jax.numpy as jnp
from jax import lax
from jax.experimental import pallas as pl
from jax.experimental.pallas import tpu as pltpu

F32 = jnp.float32
BF16 = jnp.bfloat16

DEPTH = 4
GRID_W = 64
NORM_EPS = 1e-6
ROPE_THETA = 10000.0
LOG2E = math.log2(math.e)

LRU_BLOCKS = 16
LRU_BLOCK = 128
LRU_C = 8.0
CONV_WIDTH = 4
CONV_PAD_LEFT = 2

HGRN_HEADS = 16
HGRN_HEAD_DIM = 128
GLA_CHUNK = 64

MLA_HEADS = 16
MLA_Q_RANK = 512
MLA_KV_RANK = 512
MLA_NOPE = 128
MLA_ROPE = 64
MLA_V = 128
MLA_SCALE = (MLA_NOPE + MLA_ROPE) ** -0.5
MLA_QK_PAD = 256
MLA_V_ROWS = MLA_V + 16

SWA_Q_HEADS = 16
SWA_KV_HEADS = 4
SWA_GROUP = SWA_Q_HEADS // SWA_KV_HEADS
SWA_HEAD_DIM = 128
SWA_WINDOW = 128
SWA_BLOCK = 128
SWA_SCALE = SWA_HEAD_DIM ** -0.5

LANES = 128
MIB = 1024 * 1024
MASK_VALUE = -1e30


def _cparams(semantics, vmem_mib):
    return pltpu.CompilerParams(dimension_semantics=semantics, vmem_limit_bytes=vmem_mib * MIB)


def _sigmoid(x):
    return jax.nn.sigmoid(x)


def _silu(x):
    return x * _sigmoid(x)


def _rms(x, eps=NORM_EPS):
    return x * lax.rsqrt(jnp.mean(x * x, axis=-1, keepdims=True) + eps)


def _adaln(x, g, shift, scale):
    return (_rms(x) * g) * (1.0 + scale) + shift


def _dot(a, b):
    return jnp.dot(a, b, preferred_element_type=F32)


def _dot_nt(a, b):
    return lax.dot_general(a, b, (((1,), (1,)), ((), ())), preferred_element_type=F32)


def _dot_tn(a, b):
    return lax.dot_general(a, b, (((0,), (0,)), ((), ())), preferred_element_type=F32)


def _mod_kernel(c_ref, w_ref, b_ref, o_ref):
    s = _silu(c_ref[...]).astype(BF16)
    o_ref[0] = _dot(s, w_ref[0].astype(BF16)) + b_ref[0]


def _modulation(cond, mod_w, mod_b):
    depth, d, n = mod_w.shape
    tn = 1024
    return pl.pallas_call(
        _mod_kernel,
        grid=(depth, n // tn),
        in_specs=[pl.BlockSpec((8, d), lambda l, j: (0, 0)),
                  pl.BlockSpec((1, d, tn), lambda l, j: (l, 0, j)),
                  pl.BlockSpec((1, 1, tn), lambda l, j: (l, 0, j))],
        out_specs=pl.BlockSpec((1, 8, tn), lambda l, j: (l, 0, j)),
        out_shape=jax.ShapeDtypeStruct((depth, 8, n), F32),
        compiler_params=_cparams(("parallel", "parallel"), 40),
        name="modulation",
    )(cond, mod_w, mod_b.reshape(depth, 1, n))


def _inproj_kernel(x_ref, g_ref, sh_ref, sc_ref, w_ref, o_ref, h_ref, *, head_major):
    def project(h):
        res = _dot(h, w_ref[...])
        if head_major:
            for c in range(res.shape[1] // LANES):
                o_ref[0, c] = res[:, c * LANES:(c + 1) * LANES].astype(o_ref.dtype)
        else:
            o_ref[0] = res.astype(o_ref.dtype)

    @pl.when(pl.program_id(2) == 0)
    def _():
        h = _adaln(x_ref[0], g_ref[...], sh_ref[0], sc_ref[0]).astype(BF16)
        h_ref[...] = h
        project(h)

    @pl.when(pl.program_id(2) > 0)
    def _():
        project(h_ref[...])


def _inproj(x, g, shift, scale, w, *, tn, head_major=False, name="inproj"):
    b, r, d = x.shape
    n = w.shape[1]
    tm = min(r, 512)
    assert r % tm == 0 and n % tn == 0
    if head_major:
        out_shape = jax.ShapeDtypeStruct((b, n // LANES, r, LANES), BF16)
        out_spec = pl.BlockSpec((1, tn // LANES, tm, LANES), lambda bi, i, j: (bi, j, i, 0))
    else:
        out_shape = jax.ShapeDtypeStruct((b, r, n), BF16)
        out_spec = pl.BlockSpec((1, tm, tn), lambda bi, i, j: (bi, i, j))
    vec = pl.BlockSpec((1, 1, d), lambda bi, i, j: (bi, 0, 0))
    return pl.pallas_call(
        functools.partial(_inproj_kernel, head_major=head_major),
        grid=(b, r // tm, n // tn),
        in_specs=[pl.BlockSpec((1, tm, d), lambda bi, i, j: (bi, i, 0)),
                  pl.BlockSpec((1, d), lambda bi, i, j: (0, 0)),
                  vec, vec,
                  pl.BlockSpec((d, tn), lambda bi, i, j: (0, j))],
        out_specs=out_spec,
        out_shape=out_shape,
        scratch_shapes=[pltpu.VMEM((tm, d), BF16)],
        compiler_params=_cparams(("parallel", "parallel", "arbitrary"), 48),
        name=name,
    )(x, g, shift, scale, w)


def _outproj_kernel(a_ref, w_ref, x_ref, gate_ref, o_ref):
    o_ref[0] = x_ref[0] + gate_ref[0] * _dot(a_ref[0], w_ref[...])


def _outproj(a, w, x, gate, name="outproj"):
    b, r, k = a.shape
    d = w.shape[1]
    tm = min(r, 512)
    return pl.pallas_call(
        _outproj_kernel,
        grid=(b, r // tm),
        in_specs=[pl.BlockSpec((1, tm, k), lambda bi, i: (bi, i, 0)),
                  pl.BlockSpec((k, d), lambda bi, i: (0, 0)),
                  pl.BlockSpec((1, tm, d), lambda bi, i: (bi, i, 0)),
                  pl.BlockSpec((1, 1, d), lambda bi, i: (bi, 0, 0))],
        out_specs=pl.BlockSpec((1, tm, d), lambda bi, i: (bi, i, 0)),
        out_shape=jax.ShapeDtypeStruct((b, r, d), F32),
        compiler_params=_cparams(("parallel", "parallel"), 48),
        name=name,
    )(a, w, x, gate)


def _ffn_kernel(x_ref, g_ref, sh_ref, sc_ref, gate_ref, wg_ref, wu_ref, wo_ref, o_ref, f_ref, acc_ref):
    j = pl.program_id(2)
    last = pl.num_programs(2) - 1

    def partial_out(f):
        gt = _dot(f, wg_ref[...])
        up = _dot(f, wu_ref[...])
        act = (_silu(gt) * up).astype(BF16)
        return _dot(act, wo_ref[...])

    @pl.when(j == 0)
    def _():
        f = _adaln(x_ref[0], g_ref[...], sh_ref[0], sc_ref[0]).astype(BF16)
        f_ref[...] = f
        acc_ref[...] = partial_out(f)

    @pl.when((j > 0) & (j < last))
    def _():
        acc_ref[...] += partial_out(f_ref[...])

    @pl.when(j == last)
    def _():
        o_ref[0] = x_ref[0] + gate_ref[0] * (acc_ref[...] + partial_out(f_ref[...]))


def _ffn(x, g, shift, scale, gate, w_in, w_out, name="ffn"):
    b, r, d = x.shape
    hidden = w_out.shape[0]
    tm = min(r, 512)
    tf = 512
    nf = hidden // tf
    assert hidden % tf == 0 and nf >= 2
    vec = pl.BlockSpec((1, 1, d), lambda bi, i, j: (bi, 0, 0))
    return pl.pallas_call(
        _ffn_kernel,
        grid=(b, r // tm, nf),
        in_specs=[pl.BlockSpec((1, tm, d), lambda bi, i, j: (bi, i, 0)),
                  pl.BlockSpec((1, d), lambda bi, i, j: (0, 0)),
                  vec, vec, vec,
                  pl.BlockSpec((d, tf), lambda bi, i, j: (0, j)),
                  pl.BlockSpec((d, tf), lambda bi, i, j: (0, j + nf)),
                  pl.BlockSpec((tf, d), lambda bi, i, j: (j, 0))],
        out_specs=pl.BlockSpec((1, tm, d), lambda bi, i, j: (bi, i, 0)),
        out_shape=jax.ShapeDtypeStruct((b, r, d), F32),
        scratch_shapes=[pltpu.VMEM((tm, d), BF16), pltpu.VMEM((tm, d), F32)],
        compiler_params=_cparams(("parallel", "parallel", "arbitrary"), 48),
        name=name,
    )(x, g, shift, scale, gate, w_in, w_in, w_out)


def _softplus(x):
    return jnp.maximum(x, 0.0) + jnp.log1p(jnp.exp(-jnp.abs(x)))


def _gelu_tanh(x):
    return 0.5 * x * (1.0 + jnp.tanh(math.sqrt(2.0 / math.pi) * (x + 0.044715 * (x * x * x))))


def _rglru_tile(rec, prev8, next8, wg_ref, gb_ref, lam_ref, cw_ref, cb_ref,
                ubuf, a_s, b_s, h_s, hcar, *, reverse):
    tt = rec.shape[0]
    ubuf[0:8, :] = prev8
    ubuf[8:8 + tt, :] = rec
    ubuf[8 + tt:16 + tt, :] = next8
    base = 8 - CONV_PAD_LEFT
    u = cb_ref[...] + cw_ref[0:1, :] * ubuf[base:base + tt, :]
    for j in range(1, CONV_WIDTH):
        u = u + cw_ref[j:j + 1, :] * ubuf[base + j:base + j + tt, :]
    sp = _softplus(-lam_ref[...])
    for k in range(LRU_BLOCKS):
        cs = slice(k * LRU_BLOCK, (k + 1) * LRU_BLOCK)
        uk = u[:, cs]
        gts = _dot(uk.astype(BF16), wg_ref[k])
        r = _sigmoid(gts[:, :LRU_BLOCK] + gb_ref[0:1, cs])
        i = _sigmoid(gts[:, LRU_BLOCK:] + gb_ref[1:2, cs])
        log_a = (-LRU_C) * r * sp[:, cs]
        a = jnp.exp(log_a)
        a_s[:, cs] = a
        b_s[:, cs] = jnp.sqrt(-jnp.tanh(log_a) * (a * a + 1.0)) * (i * uk)

    def body(t, h):
        row = (tt - 1 - t) if reverse else t
        h = a_s[pl.ds(row, 1), :] * h + b_s[pl.ds(row, 1), :]
        h_s[pl.ds(row, 1), :] = h
        return h

    hcar[...] = lax.fori_loop(0, tt, body, hcar[...], unroll=8)


def _rglru_kernel(*refs, reverse, tt, nt):
    if reverse:
        (gl_ref, rl_ref, pv_ref, nx_ref, gc_ref, rc_ref, hfl_ref, hfc_ref,
         wg_ref, gb_ref, lam_ref, cw_ref, cb_ref, ol_ref, oc_ref, ubuf, a_s, b_s, h_s, hcar) = refs
    else:
        (rl_ref, pv_ref, nx_ref, rc_ref,
         wg_ref, gb_ref, lam_ref, cw_ref, cb_ref, ol_ref, oc_ref, ubuf, a_s, b_s, h_s, hcar) = refs
    s = pl.program_id(1)
    tile = functools.partial(_rglru_tile, wg_ref=wg_ref, gb_ref=gb_ref, lam_ref=lam_ref, cw_ref=cw_ref,
                             cb_ref=cb_ref, ubuf=ubuf, a_s=a_s, b_s=b_s, h_s=h_s, hcar=hcar, reverse=reverse)
    zeros8 = jnp.zeros((8, rl_ref.shape[2]), F32)

    @pl.when(s == 0)
    def _():
        hcar[...] = jnp.zeros_like(hcar)
        tile(rc_ref[0].astype(F32), zeros8, zeros8)
        if reverse:
            oc_ref[0] = (_gelu_tanh(gc_ref[0].astype(F32)) * (hfc_ref[0].astype(F32) + h_s[...])).astype(oc_ref.dtype)
        else:
            oc_ref[0] = h_s[...].astype(oc_ref.dtype)

    @pl.when(s > 0)
    def _():
        tl = (nt - s) if reverse else (s - 1)
        has_prev = (tl > 0).astype(F32)
        has_next = (tl < nt - 1).astype(F32)
        prev8 = pv_ref[0].astype(F32)[8:16, :] * has_prev
        next8 = nx_ref[0].astype(F32)[0:8, :] * has_next
        tile(rl_ref[0].astype(F32), prev8, next8)
        if reverse:
            ol_ref[0] = (_gelu_tanh(gl_ref[0].astype(F32)) * (hfl_ref[0].astype(F32) + h_s[...])).astype(ol_ref.dtype)
        else:
            ol_ref[0] = h_s[...].astype(ol_ref.dtype)


def _rglru_scan(gr_lat, gr_ctx, wg, gb, lam, cw, cb, hf_lat=None, hf_ctx=None):
    reverse = hf_lat is not None
    b, s_len, w2 = gr_lat.shape
    w = w2 // 2
    tt = gr_ctx.shape[1]
    assert s_len % tt == 0 and tt % 16 == 0
    nt = s_len // tt
    hb = tt // 16

    def lat_tile(si):
        return (nt - jnp.maximum(si, 1)) if reverse else jnp.maximum(si - 1, 0)

    lat_rows = lambda col: pl.BlockSpec((1, tt, w), lambda bi, si: (bi, lat_tile(si), col))
    ctx_rows = lambda col: pl.BlockSpec((1, tt, w), lambda bi, si: (bi, 0, col))
    prev_spec = pl.BlockSpec((1, 16, w), lambda bi, si: (bi, jnp.maximum(lat_tile(si) * hb - 1, 0), 1))
    next_spec = pl.BlockSpec((1, 16, w), lambda bi, si: (bi, jnp.minimum((lat_tile(si) + 1) * hb, nt * hb - 1), 1))
    full = lambda shape: pl.BlockSpec(shape, lambda bi, si: (0,) * len(shape))
    params = [wg, gb, lam, cw, cb]
    param_specs = [full(wg.shape), full(gb.shape), full(lam.shape), full(cw.shape), full(cb.shape)]
    if reverse:
        args = [gr_lat, gr_lat, gr_lat, gr_lat, gr_ctx, gr_ctx, hf_lat, hf_ctx] + params
        in_specs = [lat_rows(0), lat_rows(1), prev_spec, next_spec, ctx_rows(0), ctx_rows(1),
                    lat_rows(0), ctx_rows(0)] + param_specs
    else:
        args = [gr_lat, gr_lat, gr_lat, gr_ctx] + params
        in_specs = [lat_rows(1), prev_spec, next_spec, ctx_rows(1)] + param_specs
    return pl.pallas_call(
        functools.partial(_rglru_kernel, reverse=reverse, tt=tt, nt=nt),
        grid=(b, nt + 1),
        in_specs=in_specs,
        out_specs=[lat_rows(0), ctx_rows(0)],
        out_shape=[jax.ShapeDtypeStruct((b, s_len, w), BF16), jax.ShapeDtypeStruct((b, tt, w), BF16)],
        scratch_shapes=[pltpu.VMEM((tt + 16, w), F32), pltpu.VMEM((tt, w), F32), pltpu.VMEM((tt, w), F32),
                        pltpu.VMEM((tt, w), F32), pltpu.VMEM((1, w), F32)],
        compiler_params=_cparams(("parallel", "arbitrary"), 48),
        name="rglru_bwd" if reverse else "rglru_fwd",
    )(*args)


def _gla_mask(reverse):
    r_i = lax.broadcasted_iota(jnp.int32, (GLA_CHUNK, GLA_CHUNK), 0)
    c_i = lax.broadcasted_iota(jnp.int32, (GLA_CHUNK, GLA_CHUNK), 1)
    return (c_i >= r_i) if reverse else (c_i <= r_i)


def _hgrn_head(q, f, v, st_t, mask, reverse):
    rows = q.shape[0]
    c = GLA_CHUNK
    half = c // 2
    nchunk = rows // c
    chunks = [slice(n * c, (n + 1) * c) for n in range(nchunk)]
    tri = jnp.where(mask, 1.0, 0.0).astype(BF16)
    g = jnp.log(f)
    k = 1.0 - f
    hi = g.astype(BF16)
    r1 = g - hi.astype(F32)
    mid = r1.astype(BF16)
    lo = (r1 - mid.astype(F32)).astype(BF16)
    g3 = jnp.concatenate([hi, mid, lo], axis=1)
    cums = []
    for rs in chunks:
        c3 = _dot(tri, g3[rs])
        cums.append((c3[:, :LANES] + c3[:, LANES:2 * LANES]) + c3[:, 2 * LANES:])
    if reverse:
        totals = [cm[0:1] for cm in cums]
        refs = [cm[half:half + 1] for cm in cums]
    else:
        totals = [cm[c - 1:c] for cm in cums]
        refs = [cm[half - 1:half] for cm in cums]
    cum = jnp.concatenate(cums, axis=0)
    ref_b = jnp.concatenate([jnp.broadcast_to(r, (c, LANES)) for r in refs], axis=0)
    qt = q * jnp.exp(cum - ref_b)
    kt = k * jnp.exp(ref_b - cum)
    qtb, ktb, vb = qt.astype(BF16), kt.astype(BF16), v.astype(BF16)
    outs, upds, decays = [], [], []
    for n, rs in enumerate(chunks):
        sc = jnp.where(mask, _dot_nt(qtb[rs], ktb[rs]), 0.0).astype(BF16)
        outs.append(_dot(sc, vb[rs]))
        kbar = (kt[rs] * jnp.exp(totals[n] - refs[n])).astype(BF16)
        upds.append(_dot_tn(vb[rs], kbar))
        decays.append(jnp.exp(totals[n]))
    for n in (range(nchunk - 1, -1, -1) if reverse else range(nchunk)):
        qi = (qt[chunks[n]] * jnp.exp(refs[n])).astype(BF16)
        outs[n] = outs[n] + _dot_nt(qi, st_t.astype(BF16))
        st_t = st_t * decays[n] + upds[n]
    return jnp.concatenate(outs, axis=0), st_t


HGRN_HEADS_PER_STEP = 8


def _hgrn_kernel(*refs, reverse, layer):
    if reverse:
        (ql, fl, il, gl, ofl, qc, fc, ic, gc, ofc, lg_ref, gn_ref, ol_ref, oc_ref, st_ref) = refs
    else:
        (ql, fl, il, qc, fc, ic, lg_ref, ol_ref, oc_ref, st_ref) = refs
    is_ctx = pl.program_id(2) == 0

    @pl.when(is_ctx)
    def _():
        st_ref[...] = jnp.zeros_like(st_ref)

    def pick(c_ref, l_ref, hh):
        return jnp.where(is_ctx, c_ref[0, hh].astype(F32), l_ref[0, hh].astype(F32))

    outs = []
    mask = _gla_mask(reverse)
    for hh in range(HGRN_HEADS_PER_STEP):
        cs = slice(hh * HGRN_HEAD_DIM, (hh + 1) * HGRN_HEAD_DIM)
        lg = lg_ref[:, cs]
        e = jnp.exp(lg - jnp.max(lg, axis=0, keepdims=True))
        sm = e / jnp.sum(e, axis=0, keepdims=True)
        lb = jnp.zeros((1, HGRN_HEAD_DIM), F32)
        for l in range(1, layer + 1):
            lb = lb + sm[l:l + 1]
        q = _silu(pick(qc, ql, hh))
        f = lb + (1.0 - lb) * _sigmoid(pick(fc, fl, hh))
        o, st_new = _hgrn_head(q, f, pick(ic, il, hh), st_ref[hh], mask, reverse)
        st_ref[hh] = st_new
        if reverse:
            o = (_rms(pick(ofc, ofl, hh) + o) * gn_ref[...]) * _silu(pick(gc, gl, hh))
        outs.append(o.astype(ol_ref.dtype))

    def store(out_ref):
        for hh, ob in enumerate(outs):
            if reverse:
                out_ref[0, :, hh * HGRN_HEAD_DIM:(hh + 1) * HGRN_HEAD_DIM] = ob
            else:
                out_ref[0, hh] = ob

    @pl.when(is_ctx)
    def _():
        store(oc_ref)

    @pl.when(jnp.logical_not(is_ctx))
    def _():
        store(ol_ref)


def _hgrn_scan(p_lat, p_ctx, logits_d, layer, gn=None, of_lat=None, of_ctx=None):
    reverse = of_lat is not None
    b, _, s_len, hd = p_lat.shape
    h, hps = HGRN_HEADS, HGRN_HEADS_PER_STEP
    tt = p_ctx.shape[2]
    assert s_len % tt == 0 and tt % GLA_CHUNK == 0 and h % hps == 0
    nt = s_len // tt
    ng = h // hps
    fsel = 2 if reverse else 1

    def lat_tile(si):
        return (nt - jnp.maximum(si, 1)) if reverse else jnp.maximum(si - 1, 0)

    lat = lambda grp: pl.BlockSpec((1, hps, tt, hd), lambda bi, hi, si: (bi, grp * ng + hi, lat_tile(si), 0))
    ctx = lambda grp: pl.BlockSpec((1, hps, tt, hd), lambda bi, hi, si: (bi, grp * ng + hi, 0, 0))
    lg_spec = pl.BlockSpec((logits_d.shape[0], hps * hd), lambda bi, hi, si: (0, hi))
    if reverse:
        args = [p_lat, p_lat, p_lat, p_lat, of_lat, p_ctx, p_ctx, p_ctx, p_ctx, of_ctx, logits_d, gn]
        in_specs = [lat(0), lat(fsel), lat(3), lat(4), lat(0), ctx(0), ctx(fsel), ctx(3), ctx(4), ctx(0),
                    lg_spec, pl.BlockSpec((1, hd), lambda bi, hi, si: (0, 0))]
        out_specs = [pl.BlockSpec((1, tt, hps * hd), lambda bi, hi, si: (bi, lat_tile(si), hi)),
                     pl.BlockSpec((1, tt, hps * hd), lambda bi, hi, si: (bi, 0, hi))]
        out_shape = [jax.ShapeDtypeStruct((b, s_len, h * hd), BF16), jax.ShapeDtypeStruct((b, tt, h * hd), BF16)]
    else:
        args = [p_lat, p_lat, p_lat, p_ctx, p_ctx, p_ctx, logits_d]
        in_specs = [lat(0), lat(fsel), lat(3), ctx(0), ctx(fsel), ctx(3), lg_spec]
        out_specs = [lat(0), ctx(0)]
        out_shape = [jax.ShapeDtypeStruct((b, h, s_len, hd), BF16), jax.ShapeDtypeStruct((b, h, tt, hd), BF16)]
    return pl.pallas_call(
        functools.partial(_hgrn_kernel, reverse=reverse, layer=layer),
        grid=(b, ng, nt + 1),
        in_specs=in_specs,
        out_specs=out_specs,
        out_shape=out_shape,
        scratch_shapes=[pltpu.VMEM((hps, hd, hd), F32)],
        compiler_params=_cparams(("parallel", "parallel", "arbitrary"), 32),
        name="hgrn_bwd" if reverse else "hgrn_fwd",
    )(*args)


def _rope_tables(n_lat, rot_dim, reps):
    n_freq = rot_dim // 4
    t = jnp.arange(n_lat)
    inv = ROPE_THETA ** (-jnp.arange(n_freq, dtype=F32) / n_freq)
    ang_r = (t // GRID_W).astype(F32)[:, None] * inv[None, :]
    ang_c = (t % GRID_W).astype(F32)[:, None] * inv[None, :]
    cos = jnp.concatenate([jnp.cos(ang_r)] * 2 + [jnp.cos(ang_c)] * 2, axis=-1)
    sin = jnp.concatenate([-jnp.sin(ang_r), jnp.sin(ang_r), -jnp.sin(ang_c), jnp.sin(ang_c)], axis=-1)
    return jnp.concatenate([jnp.tile(cos, (1, reps)), jnp.tile(sin, (1, reps))], axis=-1)


def _rope(x, cos, sin, n_freq):
    width = x.shape[1]
    lane = lax.broadcasted_iota(jnp.int32, (1, width), 1)
    first = (lane % (2 * n_freq)) < n_freq
    partner = jnp.where(first, pltpu.roll(x, width - n_freq, 1), pltpu.roll(x, n_freq, 1))
    return x * cos + partner * sin


def _mla_qkv_kernel(*refs, use_rope):
    if use_rope:
        (c_ref, qg_ref, kvg_ref, wq_ref, wk_ref, wvt_ref, gqn_ref, gqr_ref, gkn_ref, gkr_ref, tab_ref,
         q_ref, k_ref, vt_ref) = refs
    else:
        (c_ref, qg_ref, kvg_ref, wq_ref, wk_ref, wvt_ref, gqn_ref, gqr_ref, gkn_ref, gkr_ref,
         q_ref, k_ref, vt_ref) = refs
    c = c_ref[0].astype(F32)
    cqn = (_rms(c[:, :MLA_Q_RANK]) * qg_ref[...]).astype(BF16)
    ckvn = (_rms(c[:, MLA_Q_RANK:MLA_Q_RANK + MLA_KV_RANK]) * kvg_ref[...]).astype(BF16)
    kr = c[:, MLA_Q_RANK + MLA_KV_RANK:]
    qa = _dot(cqn, wq_ref[...])
    ka = _dot(ckvn, wk_ref[...])
    vt = _dot_nt(wvt_ref[...], ckvn)
    lane = lax.broadcasted_iota(jnp.int32, (1, LANES), 1)
    low = lane < MLA_ROPE
    if use_rope:
        cos, sin = tab_ref[:, :LANES], tab_ref[:, LANES:]
    inv_rope = 1.0 / MLA_ROPE

    def halves_rms(x):
        sq = x * x
        ss_lo = jnp.sum(jnp.where(low, sq, 0.0), axis=-1, keepdims=True)
        ss_hi = jnp.sum(jnp.where(low, 0.0, sq), axis=-1, keepdims=True)
        return x * jnp.where(low, lax.rsqrt(ss_lo * inv_rope + NORM_EPS), lax.rsqrt(ss_hi * inv_rope + NORM_EPS))

    krn = halves_rms(kr) * gkr_ref[...]
    if use_rope:
        krn = _rope(krn, cos, sin, MLA_ROPE // 4)
    krn = jnp.where(low, krn, 0.0)
    nope_w = MLA_HEADS * MLA_NOPE
    qscale = MLA_SCALE * LOG2E
    for p in range(MLA_HEADS // 2):
        qr = halves_rms(qa[:, nope_w + p * LANES:nope_w + (p + 1) * LANES]) * gqr_ref[...]
        if use_rope:
            qr = _rope(qr, cos, sin, MLA_ROPE // 4)
        for e in range(2):
            h = 2 * p + e
            hs = slice(h * MLA_NOPE, (h + 1) * MLA_NOPE)
            qn = _rms(qa[:, hs]) * gqn_ref[...]
            rot = qr if e == 0 else pltpu.roll(qr, MLA_ROPE, 1)
            rot = jnp.where(low, rot, 0.0)
            q_ref[0, h] = (jnp.concatenate([qn, rot], axis=1) * qscale).astype(BF16)
            kn = _rms(ka[:, hs]) * gkn_ref[...]
            k_ref[0, h] = jnp.concatenate([kn, krn], axis=1).astype(BF16)
            vt_ref[0, h, 0, :MLA_V, :] = vt[hs, :].astype(BF16)
            vt_ref[0, h, 0, MLA_V:, :] = jnp.ones((MLA_V_ROWS - MLA_V, vt.shape[1]), BF16)


def _mla_qkv(c, qg, kvg, wq, wk, wvt, gqn, gqr, gkn, gkr, tab, tm):
    b, r, cw = c.shape
    assert r % tm == 0
    h = MLA_HEADS
    use_rope = tab is not None
    full = lambda a: pl.BlockSpec(a.shape, lambda bi, i: (0,) * a.ndim)
    args = [c, qg, kvg, wq, wk, wvt, gqn, gqr, gkn, gkr]
    in_specs = [pl.BlockSpec((1, tm, cw), lambda bi, i: (bi, i, 0))] + [full(a) for a in args[1:]]
    if use_rope:
        args.append(tab)
        in_specs.append(pl.BlockSpec((tm, tab.shape[1]), lambda bi, i: (i, 0)))
    return pl.pallas_call(
        functools.partial(_mla_qkv_kernel, use_rope=use_rope),
        grid=(b, r // tm),
        in_specs=in_specs,
        out_specs=[pl.BlockSpec((1, h, tm, MLA_QK_PAD), lambda bi, i: (bi, 0, i, 0)),
                   pl.BlockSpec((1, h, tm, MLA_QK_PAD), lambda bi, i: (bi, 0, i, 0)),
                   pl.BlockSpec((1, h, 1, MLA_V_ROWS, tm), lambda bi, i: (bi, 0, i, 0, 0))],
        out_shape=[jax.ShapeDtypeStruct((b, h, r, MLA_QK_PAD), BF16),
                   jax.ShapeDtypeStruct((b, h, r, MLA_QK_PAD), BF16),
                   jax.ShapeDtypeStruct((b, h, r // tm, MLA_V_ROWS, tm), BF16)],
        compiler_params=_cparams(("parallel", "parallel"), 48),
        name="mla_qkv_rope" if use_rope else "mla_qkv",
    )(*args)


MLA_KV_TILES_PER_TRIP = 4
MLA_Q_CHUNK = 512
MLA_Q_CHUNKS_PER_STEP = 2


def _attn_update(s, vt, m, acc):
    m_new = jnp.maximum(m, jnp.max(s, axis=0, keepdims=True))
    alpha = jnp.exp2(m - m_new)
    p = jnp.exp2(s - m_new).astype(BF16)
    return m_new, alpha * acc + _dot(vt, p)


MLA_STALE_MARGIN = 100.0
MLA_STALE_BOUND = (126.0 + MLA_STALE_MARGIN) / 2.0


def _attn_update_stale(s, vt, m, acc):
    p = jnp.exp2(s - m).astype(BF16)
    m_new = jnp.maximum(m, jnp.max(s, axis=0, keepdims=True))
    return m_new, (acc + _dot(vt, p)) * jnp.exp2(m - m_new)


def _mla_attn_stale(qs, kc_ref, vc_ref, kl_ref, vl_ref, floor):
    nk, tk = vl_ref.shape[2], vl_ref.shape[4]
    unroll = min(MLA_KV_TILES_PER_TRIP, nk)
    ms, accs = [], []
    for q in qs:
        m = jnp.full((1, q.shape[0]), floor, F32)
        acc = jnp.zeros((vc_ref.shape[3], q.shape[0]), F32)
        m, acc = _attn_update(_dot_nt(kc_ref[0, 0], q), vc_ref[0, 0, 0], m, acc)
        ms.append(m)
        accs.append(acc)

    def body(i, carry):
        ms, accs = list(carry[0]), list(carry[1])
        for u in range(unroll):
            t = unroll * i + u
            k = kl_ref[0, 0, pl.ds(pl.multiple_of(t * tk, tk), tk), :]
            vt = vl_ref[0, 0, t]
            ss = [_dot_nt(k, q) for q in qs]
            for c in range(len(qs)):
                ms[c], accs[c] = _attn_update_stale(ss[c], vt, ms[c], accs[c])
        return tuple(ms), tuple(accs)

    return lax.fori_loop(0, nk // unroll, body, (tuple(ms), tuple(accs)))[1]


def _mla_attn_online(q, kc_ref, vc_ref, kl_ref, vl_ref):
    nk, tk = vl_ref.shape[2], vl_ref.shape[4]
    unroll = min(MLA_KV_TILES_PER_TRIP, nk)
    m = jnp.full((1, q.shape[0]), -jnp.inf, F32)
    acc = jnp.zeros((vc_ref.shape[3], q.shape[0]), F32)

    def scores(j):
        return _dot_nt(kl_ref[0, 0, pl.ds(pl.multiple_of(j * tk, tk), tk), :], q)

    s_cur = scores(0)
    m, acc = _attn_update(_dot_nt(kc_ref[0, 0], q), vc_ref[0, 0, 0], m, acc)

    def body(i, carry):
        s_cur, m, acc = carry
        for u in range(unroll):
            t = unroll * i + u
            s_next = scores(t + 1)
            m, acc = _attn_update(s_cur, vl_ref[0, 0, t], m, acc)
            s_cur = s_next
        return s_cur, m, acc

    s_cur, m, acc = lax.fori_loop(0, nk // unroll - 1, body, (s_cur, m, acc))
    for t in range(nk - unroll, nk):
        s_next = scores(t + 1) if t + 1 < nk else None
        m, acc = _attn_update(s_cur, vl_ref[0, 0, t], m, acc)
        s_cur = s_next
    return acc


def _mla_attn_kernel(*refs, with_lat, chunk):
    if with_lat:
        bound_ref, q_ref, kc_ref, vc_ref, kl_ref, vl_ref, o_ref = refs
    else:
        q_ref, kc_ref, vc_ref, o_ref = refs
    tq = q_ref.shape[2]
    rows = [slice(c * chunk, (c + 1) * chunk) for c in range(tq // chunk)]

    def finish(acc, rs):
        o_ref[0, rs, :] = (acc[:MLA_V] / acc[MLA_V:MLA_V + 1]).T.astype(o_ref.dtype)

    if with_lat:
        bound = bound_ref[0]

        @pl.when(bound < MLA_STALE_BOUND)
        def _():
            accs = _mla_attn_stale([q_ref[0, 0, rs, :] for rs in rows], kc_ref, vc_ref, kl_ref, vl_ref,
                                   bound - MLA_STALE_MARGIN)
            for acc, rs in zip(accs, rows):
                finish(acc, rs)

        @pl.when(bound >= MLA_STALE_BOUND)
        def _():
            for rs in rows:
                finish(_mla_attn_online(q_ref[0, 0, rs, :], kc_ref, vc_ref, kl_ref, vl_ref), rs)
    else:
        for rs in rows:
            q = q_ref[0, 0, rs, :]
            m = jnp.full((1, chunk), -jnp.inf, F32)
            acc = jnp.zeros((vc_ref.shape[3], chunk), F32)
            finish(_attn_update(_dot_nt(kc_ref[0, 0], q), vc_ref[0, 0, 0], m, acc)[1], rs)


def _mla_attention(q, k_ctx, vt_ctx, k_lat=None, vt_lat=None, bound=None):
    b, h, n, dq = q.shape
    with_lat = k_lat is not None
    chunk = min(n, MLA_Q_CHUNK)
    tq = min(n, MLA_Q_CHUNKS_PER_STEP * chunk)
    assert n % tq == 0
    rc = k_ctx.shape[2]
    args = [q, k_ctx, vt_ctx]
    in_specs = [pl.BlockSpec((1, 1, tq, dq), lambda bi, hi, i: (bi, hi, i, 0)),
                pl.BlockSpec((1, 1, rc, dq), lambda bi, hi, i: (bi, hi, 0, 0)),
                pl.BlockSpec((1, 1, 1, MLA_V_ROWS, rc), lambda bi, hi, i: (bi, hi, 0, 0, 0))]
    if with_lat:
        nk, tk = vt_lat.shape[2], vt_lat.shape[4]
        assert nk % min(MLA_KV_TILES_PER_TRIP, nk) == 0
        args = [bound] + args + [k_lat, vt_lat]
        in_specs = [pl.BlockSpec(memory_space=pltpu.SMEM)] + in_specs
        in_specs += [pl.BlockSpec((1, 1, nk * tk, dq), lambda bi, hi, i: (bi, hi, 0, 0)),
                     pl.BlockSpec((1, 1, nk, MLA_V_ROWS, tk), lambda bi, hi, i: (bi, hi, 0, 0, 0))]
    return pl.pallas_call(
        functools.partial(_mla_attn_kernel, with_lat=with_lat, chunk=chunk),
        grid=(b, h, n // tq),
        in_specs=in_specs,
        out_specs=pl.BlockSpec((1, tq, MLA_V), lambda bi, hi, i: (bi, i, hi)),
        out_shape=jax.ShapeDtypeStruct((b, n, h * MLA_V), BF16),
        compiler_params=_cparams(("parallel", "parallel", "arbitrary"), 48),
        name="mla_attn" if with_lat else "mla_attn_ctx",
    )(*args)


def _swa_inproj_kernel(*refs, use_rope, n_norm_tiles):
    if use_rope:
        x_ref, g_ref, sh_ref, sc_ref, w_ref, hg_ref, tab_ref, o_ref, h_ref = refs
    else:
        x_ref, g_ref, sh_ref, sc_ref, w_ref, hg_ref, o_ref, h_ref = refs
    j = pl.program_id(2)

    def project(h, normed):
        res = _dot(h, w_ref[...])
        if not normed:
            o_ref[0] = res.astype(o_ref.dtype)
            return
        for e in range(res.shape[1] // SWA_HEAD_DIM):
            cs = slice(e * SWA_HEAD_DIM, (e + 1) * SWA_HEAD_DIM)
            xh = _rms(res[:, cs]) * hg_ref[:, cs]
            if use_rope:
                xh = _rope(xh, tab_ref[:, :SWA_HEAD_DIM], tab_ref[:, SWA_HEAD_DIM:], SWA_HEAD_DIM // 4)
            o_ref[0, :, cs] = xh.astype(o_ref.dtype)

    @pl.when(j == 0)
    def _():
        h = _adaln(x_ref[0], g_ref[...], sh_ref[0], sc_ref[0]).astype(BF16)
        h_ref[...] = h
        project(h, True)

    @pl.when((j > 0) & (j < n_norm_tiles))
    def _():
        project(h_ref[...], True)

    @pl.when(j >= n_norm_tiles)
    def _():
        project(h_ref[...], False)


def _swa_inproj(x, g, shift, scale, w, head_gain, tab):
    b, r, d = x.shape
    n = w.shape[1]
    tm = min(r, 512)
    tn = 512
    use_rope = tab is not None
    n_norm_tiles = (SWA_Q_HEADS + SWA_KV_HEADS) * SWA_HEAD_DIM // tn
    vec = pl.BlockSpec((1, 1, d), lambda bi, i, j: (bi, 0, 0))
    args = [x, g, shift, scale, w, head_gain]
    in_specs = [pl.BlockSpec((1, tm, d), lambda bi, i, j: (bi, i, 0)),
                pl.BlockSpec((1, d), lambda bi, i, j: (0, 0)),
                vec, vec,
                pl.BlockSpec((d, tn), lambda bi, i, j: (0, j)),
                pl.BlockSpec((1, tn), lambda bi, i, j: (0, j))]
    if use_rope:
        args.append(tab)
        in_specs.append(pl.BlockSpec((tm, tab.shape[1]), lambda bi, i, j: (i, 0)))
    return pl.pallas_call(
        functools.partial(_swa_inproj_kernel, use_rope=use_rope, n_norm_tiles=n_norm_tiles),
        grid=(b, r // tm, n // tn),
        in_specs=in_specs,
        out_specs=pl.BlockSpec((1, tm, tn), lambda bi, i, j: (bi, i, j)),
        out_shape=jax.ShapeDtypeStruct((b, r, n), BF16),
        scratch_shapes=[pltpu.VMEM((tm, d), BF16)],
        compiler_params=_cparams(("parallel", "parallel", "arbitrary"), 48),
        name="swa_inproj_rope" if use_rope else "swa_inproj",
    )(*args)


def _swa_block(q, keys, vals, sinks, nctx, prev_ok, next_ok):
    blk, grp = SWA_BLOCK, SWA_GROUP
    qs = jnp.concatenate([q[:, e * SWA_HEAD_DIM:(e + 1) * SWA_HEAD_DIM] for e in range(grp)], axis=0)
    s = _dot_nt(qs, jnp.concatenate(keys, axis=0))
    rows = s.shape[0]
    r_i = lax.broadcasted_iota(jnp.int32, (rows, blk), 0) % blk
    c_i = lax.broadcasted_iota(jnp.int32, (rows, blk), 1)
    off_prev = jnp.where(prev_ok, 0, 2 * blk)
    off_next = jnp.where(next_ok, 0, 2 * blk)
    s_prev = jnp.where(c_i >= r_i + off_prev, s[:, nctx:nctx + blk], MASK_VALUE)
    s_next = jnp.where(c_i + off_next <= r_i, s[:, nctx + 2 * blk:], MASK_VALUE)
    s = jnp.concatenate([s[:, :nctx], s_prev, s[:, nctx + blk:nctx + 2 * blk], s_next], axis=1)
    row1 = lax.broadcasted_iota(jnp.int32, (rows, 1), 0)
    sink = jnp.zeros((rows, 1), F32)
    for e in range(grp):
        sink = jnp.where(row1 // blk == e, sinks[e], sink)
    m = jnp.maximum(jnp.max(s, axis=1, keepdims=True), sink)
    p = jnp.exp2(s - m)
    l = jnp.sum(p, axis=1, keepdims=True) + jnp.exp2(sink - m)
    o = _dot(p.astype(BF16), jnp.concatenate(vals, axis=0)) / l
    return jnp.concatenate([o[e * blk:(e + 1) * blk] for e in range(grp)], axis=1)


def _swa_attn_kernel(sink_ref, q_ref, kp_ref, kc_ref, kn_ref, vp_ref, vc_ref, vn_ref, kx_ref, vx_ref, o_ref):
    hk = pl.program_id(1)
    i = pl.program_id(2)
    last = pl.num_programs(2) - 1
    blk = SWA_BLOCK
    nctx = kx_ref.shape[1]
    sinks = [sink_ref[hk * SWA_GROUP + e] * LOG2E for e in range(SWA_GROUP)]
    k0, k1 = kc_ref[0, :blk, :], kc_ref[0, blk:, :]
    v0, v1 = vc_ref[0, :blk, :], vc_ref[0, blk:, :]
    o_a = _swa_block(q_ref[0, :blk, :], [kx_ref[0], kp_ref[0], k0, k1], [vx_ref[0], vp_ref[0], v0, v1],
                     sinks, nctx, i > 0, True)
    o_b = _swa_block(q_ref[0, blk:, :], [kx_ref[0], k0, k1, kn_ref[0]], [vx_ref[0], v0, v1, vn_ref[0]],
                     sinks, nctx, True, i < last)
    o_ref[0, :blk, :] = o_a.astype(o_ref.dtype)
    o_ref[0, blk:, :] = o_b.astype(o_ref.dtype)


def _swa_attention(qkv_lat, qkv_ctx, sink):
    b, n, _ = qkv_lat.shape
    nctx = qkv_ctx.shape[1]
    blk, dh, grp = SWA_BLOCK, SWA_HEAD_DIM, SWA_GROUP
    nb = n // blk
    assert nb % 2 == 0
    kcol = SWA_Q_HEADS
    vcol = SWA_Q_HEADS + SWA_KV_HEADS
    prev = lambda col0: pl.BlockSpec((1, blk, dh), lambda bi, hk, i: (bi, jnp.maximum(2 * i - 1, 0), col0 + hk))
    nxt = lambda col0: pl.BlockSpec((1, blk, dh), lambda bi, hk, i: (bi, jnp.minimum(2 * i + 2, nb - 1), col0 + hk))
    pair = lambda col0: pl.BlockSpec((1, 2 * blk, dh), lambda bi, hk, i: (bi, i, col0 + hk))
    ctx = lambda col0: pl.BlockSpec((1, nctx, dh), lambda bi, hk, i: (bi, 0, col0 + hk))
    return pl.pallas_call(
        _swa_attn_kernel,
        grid=(b, SWA_KV_HEADS, nb // 2),
        in_specs=[pl.BlockSpec(memory_space=pltpu.SMEM),
                  pl.BlockSpec((1, 2 * blk, grp * dh), lambda bi, hk, i: (bi, i, hk)),
                  prev(kcol), pair(kcol), nxt(kcol),
                  prev(vcol), pair(vcol), nxt(vcol),
                  ctx(kcol), ctx(vcol)],
        out_specs=pl.BlockSpec((1, 2 * blk, grp * dh), lambda bi, hk, i: (bi, i, hk)),
        out_shape=jax.ShapeDtypeStruct((b, n, SWA_Q_HEADS * dh), BF16),
        compiler_params=_cparams(("parallel", "parallel", "arbitrary"), 32),
        name="swa_attn",
    )(sink, qkv_lat, qkv_lat, qkv_lat, qkv_lat, qkv_lat, qkv_lat, qkv_lat, qkv_ctx, qkv_ctx)


def kernel(x, c, ctx, c_ctx, mod_w, mod_b, norm_g, ffn_w_in, ffn_w_out, rglru_w_in, rglru_conv_w, rglru_conv_b, rglru_gate_w, rglru_gate_b, rglru_lambda, rglru_w_out, hgrn_w_in, hgrn_lb_logits, hgrn_gnorm_g, hgrn_w_out, mla_w_in, mla_q_norm_g, mla_kv_norm_g, mla_w_uq, mla_w_ukv, mla_qk_g, mla_w_out, swa_w_in, swa_qk_g, swa_sink, swa_w_out):
    b, n_lat, d = x.shape
    depth = mod_w.shape[0]
    assert depth == DEPTH and b + 1 <= 8
    bf = lambda a: a.astype(BF16)

    cond = jnp.concatenate([c, c_ctx[None, :], jnp.zeros((8 - b - 1, d), F32)], axis=0)
    mod = _modulation(cond, mod_w, mod_b).reshape(depth, 8, 6, d)

    x_lat, x_ctx = x, ctx
    for layer in range(depth):
        kind = layer % 4
        need_ctx = layer < depth - 1
        m_lat = [mod[layer, :b, k][:, None, :] for k in range(6)]
        m_ctx = [jnp.broadcast_to(mod[layer, b:b + 1, k][:, None, :], (b, 1, d)) for k in range(6)]
        g1 = norm_g[layer, 0][None, :]
        g2 = norm_g[layer, 1][None, :]
        y_ctx = None
        if kind == 0:
            w_in = bf(rglru_w_in[0])
            gr_lat = _inproj(x_lat, g1, m_lat[0], m_lat[1], w_in, tn=1024, name="rglru_inproj")
            gr_ctx = _inproj(x_ctx, g1, m_ctx[0], m_ctx[1], w_in, tn=1024, name="rglru_inproj_ctx")
            gw = rglru_gate_w[0]
            wg = [bf(jnp.concatenate([gw[dr, 0], gw[dr, 1]], axis=-1)) for dr in range(2)]
            cw, cb = rglru_conv_w[0], rglru_conv_b[0][None, :]
            scan = lambda dr, **kw: _rglru_scan(gr_lat, gr_ctx, wg[dr], rglru_gate_b[0, dr],
                                                rglru_lambda[0, dr][None, :], cw, cb, **kw)
            hf_lat, hf_ctx = scan(0)
            y_lat, y_ctx = scan(1, hf_lat=hf_lat, hf_ctx=hf_ctx)
            w_out = bf(rglru_w_out[0])
        elif kind == 1:
            w_in = bf(hgrn_w_in[0])
            p_lat = _inproj(x_lat, g1, m_lat[0], m_lat[1], w_in, tn=1024, head_major=True, name="hgrn_inproj")
            p_ctx = _inproj(x_ctx, g1, m_ctx[0], m_ctx[1], w_in, tn=1024, head_major=True, name="hgrn_inproj_ctx")
            of_lat, of_ctx = _hgrn_scan(p_lat, p_ctx, hgrn_lb_logits[:, 0, :], layer)
            y_lat, y_ctx = _hgrn_scan(p_lat, p_ctx, hgrn_lb_logits[:, 1, :], layer, gn=hgrn_gnorm_g[0][None, :],
                                      of_lat=of_lat, of_ctx=of_ctx)
            w_out = bf(hgrn_w_out[0])
        elif kind == 2:
            cw_real = mla_w_in.shape[2]
            cw_pad = -(-cw_real // LANES) * LANES
            w_in = bf(jnp.pad(mla_w_in[0], ((0, 0), (0, cw_pad - cw_real))))
            c_lat = _inproj(x_lat, g1, m_lat[0], m_lat[1], w_in, tn=cw_pad, name="mla_inproj")
            c_ctx_ = _inproj(x_ctx, g1, m_ctx[0], m_ctx[1], w_in, tn=cw_pad, name="mla_inproj_ctx")
            wq3 = mla_w_uq[0].reshape(MLA_Q_RANK, MLA_HEADS, MLA_NOPE + MLA_ROPE)
            wq = bf(jnp.concatenate([wq3[:, :, :MLA_NOPE].reshape(MLA_Q_RANK, -1),
                                     wq3[:, :, MLA_NOPE:].reshape(MLA_Q_RANK, -1)], axis=1))
            wkv3 = mla_w_ukv[0].reshape(MLA_KV_RANK, MLA_HEADS, MLA_NOPE + MLA_V)
            wk = bf(wkv3[:, :, :MLA_NOPE].reshape(MLA_KV_RANK, -1))
            wvt = bf(wkv3[:, :, MLA_NOPE:].reshape(MLA_KV_RANK, -1).T)
            qk_g = mla_qk_g[0]
            gqn, gkn = qk_g[0:1, :MLA_NOPE], qk_g[1:2, :MLA_NOPE]
            gqr = jnp.tile(qk_g[0:1, MLA_NOPE:], (1, LANES // MLA_ROPE))
            gkr = jnp.pad(qk_g[1:2, MLA_NOPE:], ((0, 0), (0, LANES - MLA_ROPE)))
            tab = _rope_tables(n_lat, MLA_ROPE, LANES // MLA_ROPE)
            small = (mla_q_norm_g[0][None, :], mla_kv_norm_g[0][None, :], wq, wk, wvt, gqn, gqr, gkn, gkr)
            q_l, k_l, vt_l = _mla_qkv(c_lat, *small, tab, min(n_lat, 512))
            q_c, k_c, vt_c = _mla_qkv(c_ctx_, *small, None, c_ctx_.shape[1])
            gmax2 = lambda g: jnp.max(g * g)
            qk_norm2 = ((MLA_NOPE * gmax2(gqn) + MLA_ROPE * gmax2(gqr))
                        * (MLA_NOPE * gmax2(gkn) + MLA_ROPE * gmax2(gkr)))
            bound = (1.05 * MLA_SCALE * LOG2E) * jnp.sqrt(qk_norm2).reshape(1)
            y_lat = _mla_attention(q_l, k_c, vt_c, k_l, vt_l, bound)
            if need_ctx:
                y_ctx = _mla_attention(q_c, k_c, vt_c)
            w_out = bf(mla_w_out[0])
        else:
            w_in = bf(swa_w_in[0])
            gq = jnp.tile(swa_qk_g[0, 0] * (SWA_SCALE * LOG2E), SWA_Q_HEADS)
            gk = jnp.tile(swa_qk_g[0, 1], SWA_KV_HEADS)
            head_gain = jnp.concatenate([gq, gk, jnp.ones((SWA_KV_HEADS * SWA_HEAD_DIM,), F32)])[None, :]
            tab = _rope_tables(n_lat, SWA_HEAD_DIM, 1)
            qkv_lat = _swa_inproj(x_lat, g1, m_lat[0], m_lat[1], w_in, head_gain, tab)
            qkv_ctx = _swa_inproj(x_ctx, g1, m_ctx[0], m_ctx[1], w_in, head_gain, None)
            y_lat = _swa_attention(qkv_lat, qkv_ctx, swa_sink[0])
            assert not need_ctx
            w_out = bf(swa_w_out[0])

        w_ffn_in, w_ffn_out = bf(ffn_w_in[layer]), bf(ffn_w_out[layer])
        x_lat = _outproj(y_lat, w_out, x_lat, m_lat[2])
        x_lat = _ffn(x_lat, g2, m_lat[3], m_lat[4], m_lat[5], w_ffn_in, w_ffn_out)
        if need_ctx:
            x_ctx = _outproj(y_ctx, w_out, x_ctx, m_ctx[2], name="outproj_ctx")
            x_ctx = _ffn(x_ctx, g2, m_ctx[3], m_ctx[4], m_ctx[5], w_ffn_in, w_ffn_out, name="ffn_ctx")
    return x_lat
```

```python
import functools
import math

import jax
import jax.numpy as jnp
from jax import lax
from jax.experimental import pallas as pl
from jax.experimental.pallas import tpu as pltpu

F32 = jnp.float32
BF16 = jnp.bfloat16

DEPTH = 4
GRID_W = 64
NORM_EPS = 1e-6
ROPE_THETA = 10000.0
LOG2E = math.log2(math.e)

LRU_BLOCKS = 16
LRU_BLOCK = 128
LRU_C = 8.0
CONV_WIDTH = 4
CONV_PAD_LEFT = 2

HGRN_HEADS = 16
HGRN_HEAD_DIM = 128
GLA_CHUNK = 64

MLA_HEADS = 16
MLA_Q_RANK = 512
MLA_KV_RANK = 512
MLA_NOPE = 128
MLA_ROPE = 64
MLA_V = 128
MLA_SCALE = (MLA_NOPE + MLA_ROPE) ** -0.5
MLA_QK_PAD = 256
MLA_V_ROWS = MLA_V + 16

SWA_Q_HEADS = 16
SWA_KV_HEADS = 4
SWA_GROUP = SWA_Q_HEADS // SWA_KV_HEADS
SWA_HEAD_DIM = 128
SWA_WINDOW = 128
SWA_BLOCK = 128
SWA_SCALE = SWA_HEAD_DIM ** -0.5

LANES = 128
MIB = 1024 * 1024
MASK_VALUE = -1e30


def _cparams(semantics, vmem_mib):
    return pltpu.CompilerParams(dimension_semantics=semantics, vmem_limit_bytes=vmem_mib * MIB)


def _sigmoid(x):
    return jax.nn.sigmoid(x)


def _silu(x):
    return x * _sigmoid(x)


def _rms(x, eps=NORM_EPS):
    return x * lax.rsqrt(jnp.mean(x * x, axis=-1, keepdims=True) + eps)


def _adaln(x, g, shift, scale):
    return (_rms(x) * g) * (1.0 + scale) + shift


def _dot(a, b):
    return jnp.dot(a, b, preferred_element_type=F32)


def _dot_nt(a, b):
    return lax.dot_general(a, b, (((1,), (1,)), ((), ())), preferred_element_type=F32)


def _dot_tn(a, b):
    return lax.dot_general(a, b, (((0,), (0,)), ((), ())), preferred_element_type=F32)


def _mod_kernel(c_ref, w_ref, b_ref, o_ref):
    s = _silu(c_ref[...]).astype(BF16)
    o_ref[0] = _dot(s, w_ref[0].astype(BF16)) + b_ref[0]


def _modulation(cond, mod_w, mod_b):
    depth, d, n = mod_w.shape
    tn = 1024
    return pl.pallas_call(
        _mod_kernel,
        grid=(depth, n // tn),
        in_specs=[pl.BlockSpec((8, d), lambda l, j: (0, 0)),
                  pl.BlockSpec((1, d, tn), lambda l, j: (l, 0, j)),
                  pl.BlockSpec((1, 1, tn), lambda l, j: (l, 0, j))],
        out_specs=pl.BlockSpec((1, 8, tn), lambda l, j: (l, 0, j)),
        out_shape=jax.ShapeDtypeStruct((depth, 8, n), F32),
        compiler_params=_cparams(("parallel", "parallel"), 40),
        name="modulation",
    )(cond, mod_w, mod_b.reshape(depth, 1, n))


def _inproj_kernel(x_ref, g_ref, sh_ref, sc_ref, w_ref, o_ref, h_ref, *, head_major):
    def project(h):
        res = _dot(h, w_ref[...])
        if head_major:
            for c in range(res.shape[1] // LANES):
                o_ref[0, c] = res[:, c * LANES:(c + 1) * LANES].astype(o_ref.dtype)
        else:
            o_ref[0] = res.astype(o_ref.dtype)

    @pl.when(pl.program_id(2) == 0)
    def _():
        h = _adaln(x_ref[0], g_ref[...], sh_ref[0], sc_ref[0]).astype(BF16)
        h_ref[...] = h
        project(h)

    @pl.when(pl.program_id(2) > 0)
    def _():
        project(h_ref[...])


def _inproj(x, g, shift, scale, w, *, tn, head_major=False, name="inproj"):
    b, r, d = x.shape
    n = w.shape[1]
    tm = min(r, 512)
    assert r % tm == 0 and n % tn == 0
    if head_major:
        out_shape = jax.ShapeDtypeStruct((b, n // LANES, r, LANES), BF16)
        out_spec = pl.BlockSpec((1, tn // LANES, tm, LANES), lambda bi, i, j: (bi, j, i, 0))
    else:
        out_shape = jax.ShapeDtypeStruct((b, r, n), BF16)
        out_spec = pl.BlockSpec((1, tm, tn), lambda bi, i, j: (bi, i, j))
    vec = pl.BlockSpec((1, 1, d), lambda bi, i, j: (bi, 0, 0))
    return pl.pallas_call(
        functools.partial(_inproj_kernel, head_major=head_major),
        grid=(b, r // tm, n // tn),
        in_specs=[pl.BlockSpec((1, tm, d), lambda bi, i, j: (bi, i, 0)),
                  pl.BlockSpec((1, d), lambda bi, i, j: (0, 0)),
                  vec, vec,
                  pl.BlockSpec((d, tn), lambda bi, i, j: (0, j))],
        out_specs=out_spec,
        out_shape=out_shape,
        scratch_shapes=[pltpu.VMEM((tm, d), BF16)],
        compiler_params=_cparams(("parallel", "parallel", "arbitrary"), 48),
        name=name,
    )(x, g, shift, scale, w)


def _outproj_kernel(a_ref, w_ref, x_ref, gate_ref, o_ref):
    o_ref[0] = x_ref[0] + gate_ref[0] * _dot(a_ref[0], w_ref[...])


def _outproj(a, w, x, gate, name="outproj"):
    b, r, k = a.shape
    d = w.shape[1]
    tm = min(r, 512)
    return pl.pallas_call(
        _outproj_kernel,
        grid=(b, r // tm),
        in_specs=[pl.BlockSpec((1, tm, k), lambda bi, i: (bi, i, 0)),
                  pl.BlockSpec((k, d), lambda bi, i: (0, 0)),
                  pl.BlockSpec((1, tm, d), lambda bi, i: (bi, i, 0)),
                  pl.BlockSpec((1, 1, d), lambda bi, i: (bi, 0, 0))],
        out_specs=pl.BlockSpec((1, tm, d), lambda bi, i: (bi, i, 0)),
        out_shape=jax.ShapeDtypeStruct((b, r, d), F32),
        compiler_params=_cparams(("parallel", "parallel"), 48),
        name=name,
    )(a, w, x, gate)


def _ffn_kernel(x_ref, g_ref, sh_ref, sc_ref, gate_ref, wg_ref, wu_ref, wo_ref, o_ref, f_ref, acc_ref):
    j = pl.program_id(2)
    last = pl.num_programs(2) - 1

    def partial_out(f):
        half = wg_ref.shape[1] // 2
        halves = (slice(0, half), slice(half, 2 * half))
        pre = [(_dot(f, wg_ref[:, cs]), _dot(f, wu_ref[:, cs])) for cs in halves]
        out = None
        for (gt, up), cs in zip(pre, halves):
            part = _dot((_silu(gt) * up).astype(BF16), wo_ref[cs, :])
            out = part if out is None else out + part
        return out

    @pl.when(j == 0)
    def _():
        f = _adaln(x_ref[0], g_ref[...], sh_ref[0], sc_ref[0]).astype(BF16)
        f_ref[...] = f
        acc_ref[...] = partial_out(f)

    @pl.when((j > 0) & (j < last))
    def _():
        acc_ref[...] += partial_out(f_ref[...])

    @pl.when(j == last)
    def _():
        o_ref[0] = x_ref[0] + gate_ref[0] * (acc_ref[...] + partial_out(f_ref[...]))


def _ffn(x, g, shift, scale, gate, w_in, w_out, name="ffn"):
    b, r, d = x.shape
    hidden = w_out.shape[0]
    tm = min(r, 512)
    tf = 512
    nf = hidden // tf
    assert hidden % tf == 0 and nf >= 2
    vec = pl.BlockSpec((1, 1, d), lambda bi, i, j: (bi, 0, 0))
    return pl.pallas_call(
        _ffn_kernel,
        grid=(b, r // tm, nf),
        in_specs=[pl.BlockSpec((1, tm, d), lambda bi, i, j: (bi, i, 0)),
                  pl.BlockSpec((1, d), lambda bi, i, j: (0, 0)),
                  vec, vec, vec,
                  pl.BlockSpec((d, tf), lambda bi, i, j: (0, j)),
                  pl.BlockSpec((d, tf), lambda bi, i, j: (0, j + nf)),
                  pl.BlockSpec((tf, d), lambda bi, i, j: (j, 0))],
        out_specs=pl.BlockSpec((1, tm, d), lambda bi, i, j: (bi, i, 0)),
        out_shape=jax.ShapeDtypeStruct((b, r, d), F32),
        scratch_shapes=[pltpu.VMEM((tm, d), BF16), pltpu.VMEM((tm, d), F32)],
        compiler_params=_cparams(("parallel", "parallel", "arbitrary"), 48),
        name=name,
    )(x, g, shift, scale, gate, w_in, w_in, w_out)


def _softplus(x):
    return jnp.maximum(x, 0.0) + jnp.log1p(jnp.exp(-jnp.abs(x)))


def _gelu_tanh(x):
    return 0.5 * x * (1.0 + jnp.tanh(math.sqrt(2.0 / math.pi) * (x + 0.044715 * (x * x * x))))


def _rglru_tile(rec, prev8, next8, wg_ref, gb_ref, lam_ref, cw_ref, cb_ref,
                ubuf, a_s, b_s, h_s, hcar, *, reverse):
    tt = rec.shape[0]
    ubuf[0:8, :] = prev8
    ubuf[8:8 + tt, :] = rec
    ubuf[8 + tt:16 + tt, :] = next8
    base = 8 - CONV_PAD_LEFT
    u = cb_ref[...] + cw_ref[0:1, :] * ubuf[base:base + tt, :]
    for j in range(1, CONV_WIDTH):
        u = u + cw_ref[j:j + 1, :] * ubuf[base + j:base + j + tt, :]
    sp = _softplus(-lam_ref[...])
    for k in range(LRU_BLOCKS):
        cs = slice(k * LRU_BLOCK, (k + 1) * LRU_BLOCK)
        uk = u[:, cs]
        gts = _dot(uk.astype(BF16), wg_ref[k])
        r = _sigmoid(gts[:, :LRU_BLOCK] + gb_ref[0:1, cs])
        i = _sigmoid(gts[:, LRU_BLOCK:] + gb_ref[1:2, cs])
        log_a = (-LRU_C) * r * sp[:, cs]
        a = jnp.exp(log_a)
        a_s[:, cs] = a
        b_s[:, cs] = jnp.sqrt(-jnp.tanh(log_a) * (a * a + 1.0)) * (i * uk)

    def body(t, h):
        row = (tt - 1 - t) if reverse else t
        h = a_s[pl.ds(row, 1), :] * h + b_s[pl.ds(row, 1), :]
        h_s[pl.ds(row, 1), :] = h
        return h

    hcar[...] = lax.fori_loop(0, tt, body, hcar[...], unroll=8)


def _rglru_kernel(*refs, reverse, tt, nt):
    if reverse:
        (gl_ref, rl_ref, pv_ref, nx_ref, gc_ref, rc_ref, hfl_ref, hfc_ref,
         wg_ref, gb_ref, lam_ref, cw_ref, cb_ref, ol_ref, oc_ref, ubuf, a_s, b_s, h_s, hcar) = refs
    else:
        (rl_ref, pv_ref, nx_ref, rc_ref,
         wg_ref, gb_ref, lam_ref, cw_ref, cb_ref, ol_ref, oc_ref, ubuf, a_s, b_s, h_s, hcar) = refs
    s = pl.program_id(1)
    tile = functools.partial(_rglru_tile, wg_ref=wg_ref, gb_ref=gb_ref, lam_ref=lam_ref, cw_ref=cw_ref,
                             cb_ref=cb_ref, ubuf=ubuf, a_s=a_s, b_s=b_s, h_s=h_s, hcar=hcar, reverse=reverse)
    zeros8 = jnp.zeros((8, rl_ref.shape[2]), F32)

    @pl.when(s == 0)
    def _():
        hcar[...] = jnp.zeros_like(hcar)
        tile(rc_ref[0].astype(F32), zeros8, zeros8)
        if reverse:
            oc_ref[0] = (_gelu_tanh(gc_ref[0].astype(F32)) * (hfc_ref[0].astype(F32) + h_s[...])).astype(oc_ref.dtype)
        else:
            oc_ref[0] = h_s[...].astype(oc_ref.dtype)

    @pl.when(s > 0)
    def _():
        tl = (nt - s) if reverse else (s - 1)
        has_prev = (tl > 0).astype(F32)
        has_next = (tl < nt - 1).astype(F32)
        prev8 = pv_ref[0].astype(F32)[8:16, :] * has_prev
        next8 = nx_ref[0].astype(F32)[0:8, :] * has_next
        tile(rl_ref[0].astype(F32), prev8, next8)
        if reverse:
            ol_ref[0] = (_gelu_tanh(gl_ref[0].astype(F32)) * (hfl_ref[0].astype(F32) + h_s[...])).astype(ol_ref.dtype)
        else:
            ol_ref[0] = h_s[...].astype(ol_ref.dtype)


def _rglru_scan(gr_lat, gr_ctx, wg, gb, lam, cw, cb, hf_lat=None, hf_ctx=None):
    reverse = hf_lat is not None
    b, s_len, w2 = gr_lat.shape
    w = w2 // 2
    tt = gr_ctx.shape[1]
    assert s_len % tt == 0 and tt % 16 == 0
    nt = s_len // tt
    hb = tt // 16

    def lat_tile(si):
        return (nt - jnp.maximum(si, 1)) if reverse else jnp.maximum(si - 1, 0)

    lat_rows = lambda col: pl.BlockSpec((1, tt, w), lambda bi, si: (bi, lat_tile(si), col))
    ctx_rows = lambda col: pl.BlockSpec((1, tt, w), lambda bi, si: (bi, 0, col))
    prev_spec = pl.BlockSpec((1, 16, w), lambda bi, si: (bi, jnp.maximum(lat_tile(si) * hb - 1, 0), 1))
    next_spec = pl.BlockSpec((1, 16, w), lambda bi, si: (bi, jnp.minimum((lat_tile(si) + 1) * hb, nt * hb - 1), 1))
    full = lambda shape: pl.BlockSpec(shape, lambda bi, si: (0,) * len(shape))
    params = [wg, gb, lam, cw, cb]
    param_specs = [full(wg.shape), full(gb.shape), full(lam.shape), full(cw.shape), full(cb.shape)]
    if reverse:
        args = [gr_lat, gr_lat, gr_lat, gr_lat, gr_ctx, gr_ctx, hf_lat, hf_ctx] + params
        in_specs = [lat_rows(0), lat_rows(1), prev_spec, next_spec, ctx_rows(0), ctx_rows(1),
                    lat_rows(0), ctx_rows(0)] + param_specs
    else:
        args = [gr_lat, gr_lat, gr_lat, gr_ctx] + params
        in_specs = [lat_rows(1), prev_spec, next_spec, ctx_rows(1)] + param_specs
    return pl.pallas_call(
        functools.partial(_rglru_kernel, reverse=reverse, tt=tt, nt=nt),
        grid=(b, nt + 1),
        in_specs=in_specs,
        out_specs=[lat_rows(0), ctx_rows(0)],
        out_shape=[jax.ShapeDtypeStruct((b, s_len, w), BF16), jax.ShapeDtypeStruct((b, tt, w), BF16)],
        scratch_shapes=[pltpu.VMEM((tt + 16, w), F32), pltpu.VMEM((tt, w), F32), pltpu.VMEM((tt, w), F32),
                        pltpu.VMEM((tt, w), F32), pltpu.VMEM((1, w), F32)],
        compiler_params=_cparams(("parallel", "arbitrary"), 48),
        name="rglru_bwd" if reverse else "rglru_fwd",
    )(*args)


def _gla_mask(reverse):
    r_i = lax.broadcasted_iota(jnp.int32, (GLA_CHUNK, GLA_CHUNK), 0)
    c_i = lax.broadcasted_iota(jnp.int32, (GLA_CHUNK, GLA_CHUNK), 1)
    return (c_i >= r_i) if reverse else (c_i <= r_i)


def _hgrn_head(q, f, v, st_t, mask, reverse):
    rows = q.shape[0]
    c = GLA_CHUNK
    half = c // 2
    nchunk = rows // c
    chunks = [slice(n * c, (n + 1) * c) for n in range(nchunk)]
    tri = jnp.where(mask, 1.0, 0.0).astype(BF16)
    g = jnp.log(f)
    k = 1.0 - f
    hi = g.astype(BF16)
    r1 = g - hi.astype(F32)
    mid = r1.astype(BF16)
    lo = (r1 - mid.astype(F32)).astype(BF16)
    g3 = jnp.concatenate([hi, mid, lo], axis=1)
    c3s = [_dot(tri, g3[rs]) for rs in chunks]
    yield
    cums = [(c3[:, :LANES] + c3[:, LANES:2 * LANES]) + c3[:, 2 * LANES:] for c3 in c3s]
    if reverse:
        totals = [cm[0:1] for cm in cums]
        refs = [cm[half:half + 1] for cm in cums]
    else:
        totals = [cm[c - 1:c] for cm in cums]
        refs = [cm[half - 1:half] for cm in cums]
    cum = jnp.concatenate(cums, axis=0)
    ref_b = jnp.concatenate([jnp.broadcast_to(r, (c, LANES)) for r in refs], axis=0)
    e_q = jnp.exp(cum - ref_b)
    qt = q * e_q
    kt = k * (1.0 / e_q)
    qtb, ktb, vb = qt.astype(BF16), kt.astype(BF16), v.astype(BF16)
    raw, upds, decays = [], [], []
    for n, rs in enumerate(chunks):
        raw.append(_dot_nt(qtb[rs], ktb[rs]))
        kbar = (kt[rs] * jnp.exp(totals[n] - refs[n])).astype(BF16)
        upds.append(_dot_tn(vb[rs], kbar))
        decays.append(jnp.exp(totals[n]))
    yield
    outs = [_dot(jnp.where(mask, s, 0.0).astype(BF16), vb[rs]) for s, rs in zip(raw, chunks)]
    yield
    for n in (range(nchunk - 1, -1, -1) if reverse else range(nchunk)):
        qi = (qt[chunks[n]] * jnp.exp(refs[n])).astype(BF16)
        outs[n] = outs[n] + _dot_nt(qi, st_t.astype(BF16))
        st_t = st_t * decays[n] + upds[n]
    return jnp.concatenate(outs, axis=0), st_t


def _run_in_lockstep(generators):
    results = [None] * len(generators)
    active = list(range(len(generators)))
    while active:
        for i in list(active):
            try:
                next(generators[i])
            except StopIteration as done:
                results[i] = done.value
                active.remove(i)
    return results


HGRN_HEADS_PER_STEP = 8


def _hgrn_kernel(*refs, reverse, layer):
    if reverse:
        (ql, fl, il, gl, ofl, qc, fc, ic, gc, ofc, lg_ref, gn_ref, ol_ref, oc_ref, st_ref) = refs
    else:
        (ql, fl, il, qc, fc, ic, lg_ref, ol_ref, oc_ref, st_ref) = refs
    is_ctx = pl.program_id(2) == 0

    @pl.when(is_ctx)
    def _():
        st_ref[...] = jnp.zeros_like(st_ref)

    def pick(c_ref, l_ref, hh):
        return jnp.where(is_ctx, c_ref[0, hh], l_ref[0, hh]).astype(F32)

    mask = _gla_mask(reverse)
    heads = []
    for hh in range(HGRN_HEADS_PER_STEP):
        cs = slice(hh * HGRN_HEAD_DIM, (hh + 1) * HGRN_HEAD_DIM)
        lg = lg_ref[:, cs]
        e = jnp.exp(lg - jnp.max(lg, axis=0, keepdims=True))
        sm = e / jnp.sum(e, axis=0, keepdims=True)
        lb = jnp.zeros((1, HGRN_HEAD_DIM), F32)
        for l in range(1, layer + 1):
            lb = lb + sm[l:l + 1]
        q = _silu(pick(qc, ql, hh))
        f = lb + (1.0 - lb) * _sigmoid(pick(fc, fl, hh))
        heads.append(_hgrn_head(q, f, pick(ic, il, hh), st_ref[hh], mask, reverse))
    outs = []
    for hh, (o, st_new) in enumerate(_run_in_lockstep(heads)):
        st_ref[hh] = st_new
        if reverse:
            o = (_rms(pick(ofc, ofl, hh) + o) * gn_ref[...]) * _silu(pick(gc, gl, hh))
        outs.append(o.astype(ol_ref.dtype))

    def store(out_ref):
        for hh, ob in enumerate(outs):
            if reverse:
                out_ref[0, :, hh * HGRN_HEAD_DIM:(hh + 1) * HGRN_HEAD_DIM] = ob
            else:
                out_ref[0, hh] = ob

    @pl.when(is_ctx)
    def _():
        store(oc_ref)

    @pl.when(jnp.logical_not(is_ctx))
    def _():
        store(ol_ref)


def _hgrn_scan(p_lat, p_ctx, logits_d, layer, gn=None, of_lat=None, of_ctx=None):
    reverse = of_lat is not None
    b, _, s_len, hd = p_lat.shape
    h, hps = HGRN_HEADS, HGRN_HEADS_PER_STEP
    tt = p_ctx.shape[2]
    assert s_len % tt == 0 and tt % GLA_CHUNK == 0 and h % hps == 0
    nt = s_len // tt
    ng = h // hps
    fsel = 2 if reverse else 1

    def lat_tile(si):
        return (nt - jnp.maximum(si, 1)) if reverse else jnp.maximum(si - 1, 0)

    lat = lambda grp: pl.BlockSpec((1, hps, tt, hd), lambda bi, hi, si: (bi, grp * ng + hi, lat_tile(si), 0))
    ctx = lambda grp: pl.BlockSpec((1, hps, tt, hd), lambda bi, hi, si: (bi, grp * ng + hi, 0, 0))
    lg_spec = pl.BlockSpec((logits_d.shape[0], hps * hd), lambda bi, hi, si: (0, hi))
    if reverse:
        args = [p_lat, p_lat, p_lat, p_lat, of_lat, p_ctx, p_ctx, p_ctx, p_ctx, of_ctx, logits_d, gn]
        in_specs = [lat(0), lat(fsel), lat(3), lat(4), lat(0), ctx(0), ctx(fsel), ctx(3), ctx(4), ctx(0),
                    lg_spec, pl.BlockSpec((1, hd), lambda bi, hi, si: (0, 0))]
        out_specs = [pl.BlockSpec((1, tt, hps * hd), lambda bi, hi, si: (bi, lat_tile(si), hi)),
                     pl.BlockSpec((1, tt, hps * hd), lambda bi, hi, si: (bi, 0, hi))]
        out_shape = [jax.ShapeDtypeStruct((b, s_len, h * hd), BF16), jax.ShapeDtypeStruct((b, tt, h * hd), BF16)]
    else:
        args = [p_lat, p_lat, p_lat, p_ctx, p_ctx, p_ctx, logits_d]
        in_specs = [lat(0), lat(fsel), lat(3), ctx(0), ctx(fsel), ctx(3), lg_spec]
        out_specs = [lat(0), ctx(0)]
        out_shape = [jax.ShapeDtypeStruct((b, h, s_len, hd), BF16), jax.ShapeDtypeStruct((b, h, tt, hd), BF16)]
    return pl.pallas_call(
        functools.partial(_hgrn_kernel, reverse=reverse, layer=layer),
        grid=(b, ng, nt + 1),
        in_specs=in_specs,
        out_specs=out_specs,
        out_shape=out_shape,
        scratch_shapes=[pltpu.VMEM((hps, hd, hd), F32)],
        compiler_params=_cparams(("parallel", "parallel", "arbitrary"), 32),
        name="hgrn_bwd" if reverse else "hgrn_fwd",
    )(*args)


def _rope_tables(n_lat, rot_dim, reps):
    n_freq = rot_dim // 4
    t = jnp.arange(n_lat)
    inv = ROPE_THETA ** (-jnp.arange(n_freq, dtype=F32) / n_freq)
    ang_r = (t // GRID_W).astype(F32)[:, None] * inv[None, :]
    ang_c = (t % GRID_W).astype(F32)[:, None] * inv[None, :]
    cos = jnp.concatenate([jnp.cos(ang_r)] * 2 + [jnp.cos(ang_c)] * 2, axis=-1)
    sin = jnp.concatenate([-jnp.sin(ang_r), jnp.sin(ang_r), -jnp.sin(ang_c), jnp.sin(ang_c)], axis=-1)
    return jnp.concatenate([jnp.tile(cos, (1, reps)), jnp.tile(sin, (1, reps))], axis=-1)


def _rope(x, cos, sin, n_freq):
    width = x.shape[1]
    lane = lax.broadcasted_iota(jnp.int32, (1, width), 1)
    first = (lane % (2 * n_freq)) < n_freq
    partner = jnp.where(first, pltpu.roll(x, width - n_freq, 1), pltpu.roll(x, n_freq, 1))
    return x * cos + partner * sin


def _mla_qkv_kernel(*refs, use_rope):
    if use_rope:
        (c_ref, qg_ref, kvg_ref, wq_ref, wk_ref, wvt_ref, gqn_ref, gqr_ref, gkn_ref, gkr_ref, tab_ref,
         q_ref, k_ref, vt_ref) = refs
    else:
        (c_ref, qg_ref, kvg_ref, wq_ref, wk_ref, wvt_ref, gqn_ref, gqr_ref, gkn_ref, gkr_ref,
         q_ref, k_ref, vt_ref) = refs
    c = c_ref[0].astype(F32)
    cqn = (_rms(c[:, :MLA_Q_RANK]) * qg_ref[...]).astype(BF16)
    ckvn = (_rms(c[:, MLA_Q_RANK:MLA_Q_RANK + MLA_KV_RANK]) * kvg_ref[...]).astype(BF16)
    kr = c[:, MLA_Q_RANK + MLA_KV_RANK:]
    qa = _dot(cqn, wq_ref[...])
    ka = _dot(ckvn, wk_ref[...])
    vt = _dot_nt(wvt_ref[...], ckvn)
    lane = lax.broadcasted_iota(jnp.int32, (1, LANES), 1)
    low = lane < MLA_ROPE
    if use_rope:
        cos, sin = tab_ref[:, :LANES], tab_ref[:, LANES:]
    inv_rope = 1.0 / MLA_ROPE

    def halves_rms(x):
        sq = x * x
        ss_lo = jnp.sum(jnp.where(low, sq, 0.0), axis=-1, keepdims=True)
        ss_hi = jnp.sum(jnp.where(low, 0.0, sq), axis=-1, keepdims=True)
        return x * jnp.where(low, lax.rsqrt(ss_lo * inv_rope + NORM_EPS), lax.rsqrt(ss_hi * inv_rope + NORM_EPS))

    krn = halves_rms(kr) * gkr_ref[...]
    if use_rope:
        krn = _rope(krn, cos, sin, MLA_ROPE // 4)
    krn = jnp.where(low, krn, 0.0)
    nope_w = MLA_HEADS * MLA_NOPE
    qscale = MLA_SCALE * LOG2E
    for p in range(MLA_HEADS // 2):
        qr = halves_rms(qa[:, nope_w + p * LANES:nope_w + (p + 1) * LANES]) * gqr_ref[...]
        if use_rope:
            qr = _rope(qr, cos, sin, MLA_ROPE // 4)
        for e in range(2):
            h = 2 * p + e
            hs = slice(h * MLA_NOPE, (h + 1) * MLA_NOPE)
            qn = _rms(qa[:, hs]) * gqn_ref[...]
            rot = qr if e == 0 else pltpu.roll(qr, MLA_ROPE, 1)
            rot = jnp.where(low, rot, 0.0)
            q_ref[0, h] = (jnp.concatenate([qn, rot], axis=1) * qscale).astype(BF16)
            kn = _rms(ka[:, hs]) * gkn_ref[...]
            k_ref[0, h] = jnp.concatenate([kn, krn], axis=1).astype(BF16)
            vt_ref[0, h, 0, :MLA_V, :] = vt[hs, :].astype(BF16)
            vt_ref[0, h, 0, MLA_V:, :] = jnp.ones((MLA_V_ROWS - MLA_V, vt.shape[1]), BF16)


def _mla_qkv(c, qg, kvg, wq, wk, wvt, gqn, gqr, gkn, gkr, tab, tm):
    b, r, cw = c.shape
    assert r % tm == 0
    h = MLA_HEADS
    use_rope = tab is not None
    full = lambda a: pl.BlockSpec(a.shape, lambda bi, i: (0,) * a.ndim)
    args = [c, qg, kvg, wq, wk, wvt, gqn, gqr, gkn, gkr]
    in_specs = [pl.BlockSpec((1, tm, cw), lambda bi, i: (bi, i, 0))] + [full(a) for a in args[1:]]
    if use_rope:
        args.append(tab)
        in_specs.append(pl.BlockSpec((tm, tab.shape[1]), lambda bi, i: (i, 0)))
    return pl.pallas_call(
        functools.partial(_mla_qkv_kernel, use_rope=use_rope),
        grid=(b, r // tm),
        in_specs=in_specs,
        out_specs=[pl.BlockSpec((1, h, tm, MLA_QK_PAD), lambda bi, i: (bi, 0, i, 0)),
                   pl.BlockSpec((1, h, tm, MLA_QK_PAD), lambda bi, i: (bi, 0, i, 0)),
                   pl.BlockSpec((1, h, 1, MLA_V_ROWS, tm), lambda bi, i: (bi, 0, i, 0, 0))],
        out_shape=[jax.ShapeDtypeStruct((b, h, r, MLA_QK_PAD), BF16),
                   jax.ShapeDtypeStruct((b, h, r, MLA_QK_PAD), BF16),
                   jax.ShapeDtypeStruct((b, h, r // tm, MLA_V_ROWS, tm), BF16)],
        compiler_params=_cparams(("parallel", "parallel"), 48),
        name="mla_qkv_rope" if use_rope else "mla_qkv",
    )(*args)


MLA_KV_TILES_PER_TRIP = 4
MLA_Q_CHUNK = 512
MLA_Q_CHUNKS_PER_STEP = 2


def _attn_update(s, vt, m, acc):
    m_new = jnp.maximum(m, jnp.max(s, axis=0, keepdims=True))
    alpha = jnp.exp2(m - m_new)
    p = jnp.exp2(s - m_new).astype(BF16)
    return m_new, alpha * acc + _dot(vt, p)


MLA_STALE_MARGIN = 100.0
MLA_STALE_BOUND = (126.0 + MLA_STALE_MARGIN) / 2.0


def _attn_update_stale(s, vt, m, acc):
    p = jnp.exp2(s - m).astype(BF16)
    m_new = jnp.maximum(m, jnp.max(s, axis=0, keepdims=True))
    return m_new, (acc + _dot(vt, p)) * jnp.exp2(m - m_new)


def _mla_attn_stale(qs, kc_ref, vc_ref, kl_ref, vl_ref, floor):
    nk, tk = vl_ref.shape[2], vl_ref.shape[4]
    unroll = min(MLA_KV_TILES_PER_TRIP, nk)
    ms, accs = [], []
    for q in qs:
        m = jnp.full((1, q.shape[0]), floor, F32)
        acc = jnp.zeros((vc_ref.shape[3], q.shape[0]), F32)
        m, acc = _attn_update(_dot_nt(kc_ref[0, 0], q), vc_ref[0, 0, 0], m, acc)
        ms.append(m)
        accs.append(acc)

    def body(i, carry):
        ms, accs = list(carry[0]), list(carry[1])
        for u in range(unroll):
            t = unroll * i + u
            k = kl_ref[0, 0, pl.ds(pl.multiple_of(t * tk, tk), tk), :]
            vt = vl_ref[0, 0, t]
            ss = [_dot_nt(k, q) for q in qs]
            for c in range(len(qs)):
                ms[c], accs[c] = _attn_update_stale(ss[c], vt, ms[c], accs[c])
        return tuple(ms), tuple(accs)

    return lax.fori_loop(0, nk // unroll, body, (tuple(ms), tuple(accs)))[1]


def _mla_attn_online(q, kc_ref, vc_ref, kl_ref, vl_ref):
    nk, tk = vl_ref.shape[2], vl_ref.shape[4]
    unroll = min(MLA_KV_TILES_PER_TRIP, nk)
    m = jnp.full((1, q.shape[0]), -jnp.inf, F32)
    acc = jnp.zeros((vc_ref.shape[3], q.shape[0]), F32)

    def scores(j):
        return _dot_nt(kl_ref[0, 0, pl.ds(pl.multiple_of(j * tk, tk), tk), :], q)

    s_cur = scores(0)
    m, acc = _attn_update(_dot_nt(kc_ref[0, 0], q), vc_ref[0, 0, 0], m, acc)

    def body(i, carry):
        s_cur, m, acc = carry
        for u in range(unroll):
            t = unroll * i + u
            s_next = scores(t + 1)
            m, acc = _attn_update(s_cur, vl_ref[0, 0, t], m, acc)
            s_cur = s_next
        return s_cur, m, acc

    s_cur, m, acc = lax.fori_loop(0, nk // unroll - 1, body, (s_cur, m, acc))
    for t in range(nk - unroll, nk):
        s_next = scores(t + 1) if t + 1 < nk else None
        m, acc = _attn_update(s_cur, vl_ref[0, 0, t], m, acc)
        s_cur = s_next
    return acc


def _mla_attn_kernel(*refs, with_lat, chunk):
    if with_lat:
        bound_ref, q_ref, kc_ref, vc_ref, kl_ref, vl_ref, o_ref = refs
    else:
        q_ref, kc_ref, vc_ref, o_ref = refs
    tq = q_ref.shape[2]
    rows = [slice(c * chunk, (c + 1) * chunk) for c in range(tq // chunk)]

    def finish(acc, rs):
        o_ref[0, rs, :] = (acc[:MLA_V] / acc[MLA_V:MLA_V + 1]).T.astype(o_ref.dtype)

    if with_lat:
        bound = bound_ref[0]

        @pl.when(bound < MLA_STALE_BOUND)
        def _():
            accs = _mla_attn_stale([q_ref[0, 0, rs, :] for rs in rows], kc_ref, vc_ref, kl_ref, vl_ref,
                                   bound - MLA_STALE_MARGIN)
            for acc, rs in zip(accs, rows):
                finish(acc, rs)

        @pl.when(bound >= MLA_STALE_BOUND)
        def _():
            for rs in rows:
                finish(_mla_attn_online(q_ref[0, 0, rs, :], kc_ref, vc_ref, kl_ref, vl_ref), rs)
    else:
        for rs in rows:
            q = q_ref[0, 0, rs, :]
            m = jnp.full((1, chunk), -jnp.inf, F32)
            acc = jnp.zeros((vc_ref.shape[3], chunk), F32)
            finish(_attn_update(_dot_nt(kc_ref[0, 0], q), vc_ref[0, 0, 0], m, acc)[1], rs)


def _mla_attention(q, k_ctx, vt_ctx, k_lat=None, vt_lat=None, bound=None):
    b, h, n, dq = q.shape
    with_lat = k_lat is not None
    chunk = min(n, MLA_Q_CHUNK)
    tq = min(n, MLA_Q_CHUNKS_PER_STEP * chunk)
    assert n % tq == 0
    rc = k_ctx.shape[2]
    args = [q, k_ctx, vt_ctx]
    in_specs = [pl.BlockSpec((1, 1, tq, dq), lambda bi, hi, i: (bi, hi, i, 0)),
                pl.BlockSpec((1, 1, rc, dq), lambda bi, hi, i: (bi, hi, 0, 0)),
                pl.BlockSpec((1, 1, 1, MLA_V_ROWS, rc), lambda bi, hi, i: (bi, hi, 0, 0, 0))]
    if with_lat:
        nk, tk = vt_lat.shape[2], vt_lat.shape[4]
        assert nk % min(MLA_KV_TILES_PER_TRIP, nk) == 0
        args = [bound] + args + [k_lat, vt_lat]
        in_specs = [pl.BlockSpec(memory_space=pltpu.SMEM)] + in_specs
        in_specs += [pl.BlockSpec((1, 1, nk * tk, dq), lambda bi, hi, i: (bi, hi, 0, 0)),
                     pl.BlockSpec((1, 1, nk, MLA_V_ROWS, tk), lambda bi, hi, i: (bi, hi, 0, 0, 0))]
    return pl.pallas_call(
        functools.partial(_mla_attn_kernel, with_lat=with_lat, chunk=chunk),
        grid=(b, h, n // tq),
        in_specs=in_specs,
        out_specs=pl.BlockSpec((1, tq, MLA_V), lambda bi, hi, i: (bi, i, hi)),
        out_shape=jax.ShapeDtypeStruct((b, n, h * MLA_V), BF16),
        compiler_params=_cparams(("parallel", "parallel", "arbitrary"), 48),
        name="mla_attn" if with_lat else "mla_attn_ctx",
    )(*args)


def _swa_inproj_kernel(*refs, use_rope, n_norm_tiles):
    if use_rope:
        x_ref, g_ref, sh_ref, sc_ref, w_ref, hg_ref, tab_ref, o_ref, h_ref = refs
    else:
        x_ref, g_ref, sh_ref, sc_ref, w_ref, hg_ref, o_ref, h_ref = refs
    j = pl.program_id(2)

    @pl.when(j == 0)
    def _():
        h_ref[...] = _adaln(x_ref[0], g_ref[...], sh_ref[0], sc_ref[0]).astype(BF16)

    res = _dot(h_ref[...], w_ref[...])

    @pl.when(j < n_norm_tiles)
    def _():
        for e in range(res.shape[1] // SWA_HEAD_DIM):
            cs = slice(e * SWA_HEAD_DIM, (e + 1) * SWA_HEAD_DIM)
            xh = _rms(res[:, cs]) * hg_ref[:, cs]
            if use_rope:
                xh = _rope(xh, tab_ref[:, :SWA_HEAD_DIM], tab_ref[:, SWA_HEAD_DIM:], SWA_HEAD_DIM // 4)
            o_ref[0, :, cs] = xh.astype(o_ref.dtype)

    @pl.when(j >= n_norm_tiles)
    def _():
        o_ref[0] = res.astype(o_ref.dtype)


def _swa_inproj(x, g, shift, scale, w, head_gain, tab):
    b, r, d = x.shape
    n = w.shape[1]
    tm = min(r, 512)
    tn = 512
    use_rope = tab is not None
    n_norm_tiles = (SWA_Q_HEADS + SWA_KV_HEADS) * SWA_HEAD_DIM // tn
    vec = pl.BlockSpec((1, 1, d), lambda bi, i, j: (bi, 0, 0))
    args = [x, g, shift, scale, w, head_gain]
    in_specs = [pl.BlockSpec((1, tm, d), lambda bi, i, j: (bi, i, 0)),
                pl.BlockSpec((1, d), lambda bi, i, j: (0, 0)),
                vec, vec,
                pl.BlockSpec((d, tn), lambda bi, i, j: (0, j)),
                pl.BlockSpec((1, tn), lambda bi, i, j: (0, j))]
    if use_rope:
        args.append(tab)
        in_specs.append(pl.BlockSpec((tm, tab.shape[1]), lambda bi, i, j: (i, 0)))
    return pl.pallas_call(
        functools.partial(_swa_inproj_kernel, use_rope=use_rope, n_norm_tiles=n_norm_tiles),
        grid=(b, r // tm, n // tn),
        in_specs=in_specs,
        out_specs=pl.BlockSpec((1, tm, tn), lambda bi, i, j: (bi, i, j)),
        out_shape=jax.ShapeDtypeStruct((b, r, n), BF16),
        scratch_shapes=[pltpu.VMEM((tm, d), BF16)],
        compiler_params=_cparams(("parallel", "parallel", "arbitrary"), 48),
        name="swa_inproj_rope" if use_rope else "swa_inproj",
    )(*args)


def _swa_block(q, keys, vals, sinks, nctx, prev_ok, next_ok):
    blk, grp = SWA_BLOCK, SWA_GROUP
    qs = jnp.concatenate([q[:, e * SWA_HEAD_DIM:(e + 1) * SWA_HEAD_DIM] for e in range(grp)], axis=0)
    s = _dot_nt(qs, jnp.concatenate(keys, axis=0))
    yield
    rows = s.shape[0]
    r_i = lax.broadcasted_iota(jnp.int32, (rows, blk), 0) % blk
    c_i = lax.broadcasted_iota(jnp.int32, (rows, blk), 1)
    off_prev = jnp.where(prev_ok, 0, 2 * blk)
    off_next = jnp.where(next_ok, 0, 2 * blk)
    s_prev = jnp.where(c_i >= r_i + off_prev, s[:, nctx:nctx + blk], MASK_VALUE)
    s_next = jnp.where(c_i + off_next <= r_i, s[:, nctx + 2 * blk:], MASK_VALUE)
    s = jnp.concatenate([s[:, :nctx], s_prev, s[:, nctx + blk:nctx + 2 * blk], s_next], axis=1)
    row1 = lax.broadcasted_iota(jnp.int32, (rows, 1), 0)
    sink = jnp.zeros((rows, 1), F32)
    for e in range(grp):
        sink = jnp.where(row1 // blk == e, sinks[e], sink)
    m = jnp.maximum(jnp.max(s, axis=1, keepdims=True), sink)
    p = jnp.exp2(s - m)
    l = jnp.sum(p, axis=1, keepdims=True) + jnp.exp2(sink - m)
    o = _dot(p.astype(BF16), jnp.concatenate(vals, axis=0)) / l
    return jnp.concatenate([o[e * blk:(e + 1) * blk] for e in range(grp)], axis=1)


def _swa_attn_kernel(sink_ref, q_ref, kp_ref, kc_ref, kn_ref, vp_ref, vc_ref, vn_ref, kx_ref, vx_ref, o_ref):
    hk = pl.program_id(1)
    i = pl.program_id(2)
    last = pl.num_programs(2) - 1
    blk = SWA_BLOCK
    nctx = kx_ref.shape[1]
    sinks = [sink_ref[hk * SWA_GROUP + e] * LOG2E for e in range(SWA_GROUP)]
    k0, k1 = kc_ref[0, :blk, :], kc_ref[0, blk:, :]
    v0, v1 = vc_ref[0, :blk, :], vc_ref[0, blk:, :]
    o_a, o_b = _run_in_lockstep([
        _swa_block(q_ref[0, :blk, :], [kx_ref[0], kp_ref[0], k0, k1], [vx_ref[0], vp_ref[0], v0, v1],
                   sinks, nctx, i > 0, True),
        _swa_block(q_ref[0, blk:, :], [kx_ref[0], k0, k1, kn_ref[0]], [vx_ref[0], v0, v1, vn_ref[0]],
                   sinks, nctx, True, i < last)])
    o_ref[0, :blk, :] = o_a.astype(o_ref.dtype)
    o_ref[0, blk:, :] = o_b.astype(o_ref.dtype)


def _swa_attention(qkv_lat, qkv_ctx, sink):
    b, n, _ = qkv_lat.shape
    nctx = qkv_ctx.shape[1]
    blk, dh, grp = SWA_BLOCK, SWA_HEAD_DIM, SWA_GROUP
    nb = n // blk
    assert nb % 2 == 0
    kcol = SWA_Q_HEADS
    vcol = SWA_Q_HEADS + SWA_KV_HEADS
    prev = lambda col0: pl.BlockSpec((1, blk, dh), lambda bi, hk, i: (bi, jnp.maximum(2 * i - 1, 0), col0 + hk))
    nxt = lambda col0: pl.BlockSpec((1, blk, dh), lambda bi, hk, i: (bi, jnp.minimum(2 * i + 2, nb - 1), col0 + hk))
    pair = lambda col0: pl.BlockSpec((1, 2 * blk, dh), lambda bi, hk, i: (bi, i, col0 + hk))
    ctx = lambda col0: pl.BlockSpec((1, nctx, dh), lambda bi, hk, i: (bi, 0, col0 + hk))
    return pl.pallas_call(
        _swa_attn_kernel,
        grid=(b, SWA_KV_HEADS, nb // 2),
        in_specs=[pl.BlockSpec(memory_space=pltpu.SMEM),
                  pl.BlockSpec((1, 2 * blk, grp * dh), lambda bi, hk, i: (bi, i, hk)),
                  prev(kcol), pair(kcol), nxt(kcol),
                  prev(vcol), pair(vcol), nxt(vcol),
                  ctx(kcol), ctx(vcol)],
        out_specs=pl.BlockSpec((1, 2 * blk, grp * dh), lambda bi, hk, i: (bi, i, hk)),
        out_shape=jax.ShapeDtypeStruct((b, n, SWA_Q_HEADS * dh), BF16),
        compiler_params=_cparams(("parallel", "parallel", "arbitrary"), 32),
        name="swa_attn",
    )(sink, qkv_lat, qkv_lat, qkv_lat, qkv_lat, qkv_lat, qkv_lat, qkv_lat, qkv_ctx, qkv_ctx)


def kernel(x, c, ctx, c_ctx, mod_w, mod_b, norm_g, ffn_w_in, ffn_w_out, rglru_w_in, rglru_conv_w, rglru_conv_b, rglru_gate_w, rglru_gate_b, rglru_lambda, rglru_w_out, hgrn_w_in, hgrn_lb_logits, hgrn_gnorm_g, hgrn_w_out, mla_w_in, mla_q_norm_g, mla_kv_norm_g, mla_w_uq, mla_w_ukv, mla_qk_g, mla_w_out, swa_w_in, swa_qk_g, swa_sink, swa_w_out):
    b, n_lat, d = x.shape
    depth = mod_w.shape[0]
    assert depth == DEPTH and b + 1 <= 8
    bf = lambda a: a.astype(BF16)

    cond = jnp.concatenate([c, c_ctx[None, :], jnp.zeros((8 - b - 1, d), F32)], axis=0)
    mod = _modulation(cond, mod_w, mod_b).reshape(depth, 8, 6, d)

    x_lat, x_ctx = x, ctx
    for layer in range(depth):
        kind = layer % 4
        need_ctx = layer < depth - 1
        m_lat = [mod[layer, :b, k][:, None, :] for k in range(6)]
        m_ctx = [jnp.broadcast_to(mod[layer, b:b + 1, k][:, None, :], (b, 1, d)) for k in range(6)]
        g1 = norm_g[layer, 0][None, :]
        g2 = norm_g[layer, 1][None, :]
        y_ctx = None
        if kind == 0:
            w_in = bf(rglru_w_in[0])
            gr_lat = _inproj(x_lat, g1, m_lat[0], m_lat[1], w_in, tn=2048,name="rglru_inproj")
            gr_ctx = _inproj(x_ctx, g1, m_ctx[0], m_ctx[1], w_in, tn=2048,name="rglru_inproj_ctx")
            gw = rglru_gate_w[0]
            wg = [bf(jnp.concatenate([gw[dr, 0], gw[dr, 1]], axis=-1)) for dr in range(2)]
            cw, cb = rglru_conv_w[0], rglru_conv_b[0][None, :]
            scan = lambda dr, **kw: _rglru_scan(gr_lat, gr_ctx, wg[dr], rglru_gate_b[0, dr],
                                                rglru_lambda[0, dr][None, :], cw, cb, **kw)
            hf_lat, hf_ctx = scan(0)
            y_lat, y_ctx = scan(1, hf_lat=hf_lat, hf_ctx=hf_ctx)
            w_out = bf(rglru_w_out[0])
        elif kind == 1:
            w_in = bf(hgrn_w_in[0])
            p_lat = _inproj(x_lat, g1, m_lat[0], m_lat[1], w_in, tn=2048,head_major=True, name="hgrn_inproj")
            p_ctx = _inproj(x_ctx, g1, m_ctx[0], m_ctx[1], w_in, tn=2048,head_major=True, name="hgrn_inproj_ctx")
            of_lat, of_ctx = _hgrn_scan(p_lat, p_ctx, hgrn_lb_logits[:, 0, :], layer)
            y_lat, y_ctx = _hgrn_scan(p_lat, p_ctx, hgrn_lb_logits[:, 1, :], layer, gn=hgrn_gnorm_g[0][None, :],
                                      of_lat=of_lat, of_ctx=of_ctx)
            w_out = bf(hgrn_w_out[0])
        elif kind == 2:
            cw_real = mla_w_in.shape[2]
            cw_pad = -(-cw_real // LANES) * LANES
            w_in = bf(jnp.pad(mla_w_in[0], ((0, 0), (0, cw_pad - cw_real))))
            c_lat = _inproj(x_lat, g1, m_lat[0], m_lat[1], w_in, tn=cw_pad, name="mla_inproj")
            c_ctx_ = _inproj(x_ctx, g1, m_ctx[0], m_ctx[1], w_in, tn=cw_pad, name="mla_inproj_ctx")
            wq3 = mla_w_uq[0].reshape(MLA_Q_RANK, MLA_HEADS, MLA_NOPE + MLA_ROPE)
            wq = bf(jnp.concatenate([wq3[:, :, :MLA_NOPE].reshape(MLA_Q_RANK, -1),
                                     wq3[:, :, MLA_NOPE:].reshape(MLA_Q_RANK, -1)], axis=1))
            wkv3 = mla_w_ukv[0].reshape(MLA_KV_RANK, MLA_HEADS, MLA_NOPE + MLA_V)
            wk = bf(wkv3[:, :, :MLA_NOPE].reshape(MLA_KV_RANK, -1))
            wvt = bf(wkv3[:, :, MLA_NOPE:].reshape(MLA_KV_RANK, -1).T)
            qk_g = mla_qk_g[0]
            gqn, gkn = qk_g[0:1, :MLA_NOPE], qk_g[1:2, :MLA_NOPE]
            gqr = jnp.tile(qk_g[0:1, MLA_NOPE:], (1, LANES // MLA_ROPE))
            gkr = jnp.pad(qk_g[1:2, MLA_NOPE:], ((0, 0), (0, LANES - MLA_ROPE)))
            tab = _rope_tables(n_lat, MLA_ROPE, LANES // MLA_ROPE)
            small = (mla_q_norm_g[0][None, :], mla_kv_norm_g[0][None, :], wq, wk, wvt, gqn, gqr, gkn, gkr)
            q_l, k_l, vt_l = _mla_qkv(c_lat, *small, tab, min(n_lat, 512))
            q_c, k_c, vt_c = _mla_qkv(c_ctx_, *small, None, c_ctx_.shape[1])
            gmax2 = lambda g: jnp.max(g * g)
            qk_norm2 = ((MLA_NOPE * gmax2(gqn) + MLA_ROPE * gmax2(gqr))
                        * (MLA_NOPE * gmax2(gkn) + MLA_ROPE * gmax2(gkr)))
            bound = (1.05 * MLA_SCALE * LOG2E) * jnp.sqrt(qk_norm2).reshape(1)
            y_lat = _mla_attention(q_l, k_c, vt_c, k_l, vt_l, bound)
            if need_ctx:
                y_ctx = _mla_attention(q_c, k_c, vt_c)
            w_out = bf(mla_w_out[0])
        else:
            w_in = bf(swa_w_in[0])
            gq = jnp.tile(swa_qk_g[0, 0] * (SWA_SCALE * LOG2E), SWA_Q_HEADS)
            gk = jnp.tile(swa_qk_g[0, 1], SWA_KV_HEADS)
            head_gain = jnp.concatenate([gq, gk, jnp.ones((SWA_KV_HEADS * SWA_HEAD_DIM,), F32)])[None, :]
            tab = _rope_tables(n_lat, SWA_HEAD_DIM, 1)
            qkv_lat = _swa_inproj(x_lat, g1, m_lat[0], m_lat[1], w_in, head_gain, tab)
            qkv_ctx = _swa_inproj(x_ctx, g1, m_ctx[0], m_ctx[1], w_in, head_gain, None)
            y_lat = _swa_attention(qkv_lat, qkv_ctx, swa_sink[0])
            assert not need_ctx
            w_out = bf(swa_w_out[0])

        w_ffn_in, w_ffn_out = bf(ffn_w_in[layer]), bf(ffn_w_out[layer])
        x_lat = _outproj(y_lat, w_out, x_lat, m_lat[2])
        x_lat = _ffn(x_lat, g2, m_lat[3], m_lat[4], m_lat[5], w_ffn_in, w_ffn_out)
        if need_ctx:
            x_ctx = _outproj(y_ctx, w_out, x_ctx, m_ctx[2], name="outproj_ctx")
            x_ctx = _ffn(x_ctx, g2, m_ctx[3], m_ctx[4], m_ctx[5], w_ffn_in, w_ffn_out, name="ffn_ctx")
    return x_lat
```

```python
import functools
import math

import jax
import jax.numpy as jnp
from jax import lax
from jax.experimental import pallas as pl
from jax.experimental.pallas import tpu as pltpu

F32 = jnp.float32
BF16 = jnp.bfloat16

DEPTH = 4
GRID_W = 64
NORM_EPS = 1e-6
ROPE_THETA = 10000.0
LOG2E = math.log2(math.e)

LRU_BLOCKS = 16
LRU_BLOCK = 128
LRU_C = 8.0
CONV_WIDTH = 4
CONV_PAD_LEFT = 2

HGRN_HEADS = 16
HGRN_HEAD_DIM = 128
GLA_CHUNK = 64

MLA_HEADS = 16
MLA_Q_RANK = 512
MLA_KV_RANK = 512
MLA_NOPE = 128
MLA_ROPE = 64
MLA_V = 128
MLA_SCALE = (MLA_NOPE + MLA_ROPE) ** -0.5
MLA_QK_PAD = 256
MLA_V_ROWS = MLA_V + 16

SWA_Q_HEADS = 16
SWA_KV_HEADS = 4
SWA_GROUP = SWA_Q_HEADS // SWA_KV_HEADS
SWA_HEAD_DIM = 128
SWA_WINDOW = 128
SWA_BLOCK = 128
SWA_SCALE = SWA_HEAD_DIM ** -0.5

LANES = 128
MIB = 1024 * 1024
MASK_VALUE = -1e30


def _cparams(semantics, vmem_mib):
    return pltpu.CompilerParams(dimension_semantics=semantics, vmem_limit_bytes=vmem_mib * MIB)


def _sigmoid(x):
    return jax.nn.sigmoid(x)


def _silu(x):
    return x * _sigmoid(x)


def _rms(x, eps=NORM_EPS):
    return x * lax.rsqrt(jnp.mean(x * x, axis=-1, keepdims=True) + eps)


def _adaln(x, g, shift, scale):
    return (_rms(x) * g) * (1.0 + scale) + shift


def _dot(a, b):
    return jnp.dot(a, b, preferred_element_type=F32)


def _dot_nt(a, b):
    return lax.dot_general(a, b, (((1,), (1,)), ((), ())), preferred_element_type=F32)


def _dot_tn(a, b):
    return lax.dot_general(a, b, (((0,), (0,)), ((), ())), preferred_element_type=F32)


def _mod_kernel(c_ref, w_ref, b_ref, o_ref):
    s = _silu(c_ref[...]).astype(BF16)
    o_ref[0] = _dot(s, w_ref[0].astype(BF16)) + b_ref[0]


def _modulation(cond, mod_w, mod_b):
    depth, d, n = mod_w.shape
    tn = 1024
    return pl.pallas_call(
        _mod_kernel,
        grid=(depth, n // tn),
        in_specs=[pl.BlockSpec((8, d), lambda l, j: (0, 0)),
                  pl.BlockSpec((1, d, tn), lambda l, j: (l, 0, j)),
                  pl.BlockSpec((1, 1, tn), lambda l, j: (l, 0, j))],
        out_specs=pl.BlockSpec((1, 8, tn), lambda l, j: (l, 0, j)),
        out_shape=jax.ShapeDtypeStruct((depth, 8, n), F32),
        compiler_params=_cparams(("parallel", "parallel"), 40),
        name="modulation",
    )(cond, mod_w, mod_b.reshape(depth, 1, n))


def _inproj_kernel(x_ref, g_ref, sh_ref, sc_ref, w_ref, o_ref, h_ref, *, head_major):
    def project(h):
        res = _dot(h, w_ref[...])
        if head_major:
            for c in range(res.shape[1] // LANES):
                o_ref[0, c] = res[:, c * LANES:(c + 1) * LANES].astype(o_ref.dtype)
        else:
            o_ref[0] = res.astype(o_ref.dtype)

    @pl.when(pl.program_id(2) == 0)
    def _():
        h = _adaln(x_ref[0], g_ref[...], sh_ref[0], sc_ref[0]).astype(BF16)
        h_ref[...] = h
        project(h)

    @pl.when(pl.program_id(2) > 0)
    def _():
        project(h_ref[...])


def _inproj(x, g, shift, scale, w, *, tn, head_major=False, name="inproj"):
    b, r, d = x.shape
    n = w.shape[1]
    tm = min(r, 512)
    assert r % tm == 0 and n % tn == 0
    if head_major:
        out_shape = jax.ShapeDtypeStruct((b, n // LANES, r, LANES), BF16)
        out_spec = pl.BlockSpec((1, tn // LANES, tm, LANES), lambda bi, i, j: (bi, j, i, 0))
    else:
        out_shape = jax.ShapeDtypeStruct((b, r, n), BF16)
        out_spec = pl.BlockSpec((1, tm, tn), lambda bi, i, j: (bi, i, j))
    vec = pl.BlockSpec((1, 1, d), lambda bi, i, j: (bi, 0, 0))
    return pl.pallas_call(
        functools.partial(_inproj_kernel, head_major=head_major),
        grid=(b, r // tm, n // tn),
        in_specs=[pl.BlockSpec((1, tm, d), lambda bi, i, j: (bi, i, 0)),
                  pl.BlockSpec((1, d), lambda bi, i, j: (0, 0)),
                  vec, vec,
                  pl.BlockSpec((d, tn), lambda bi, i, j: (0, j))],
        out_specs=out_spec,
        out_shape=out_shape,
        scratch_shapes=[pltpu.VMEM((tm, d), BF16)],
        compiler_params=_cparams(("parallel", "parallel", "arbitrary"), 48),
        name=name,
    )(x, g, shift, scale, w)


def _outproj_kernel(a_ref, w_ref, x_ref, gate_ref, o_ref):
    o_ref[0] = x_ref[0] + gate_ref[0] * _dot(a_ref[0], w_ref[...])


def _outproj(a, w, x, gate, name="outproj"):
    b, r, k = a.shape
    d = w.shape[1]
    tm = min(r, 512)
    return pl.pallas_call(
        _outproj_kernel,
        grid=(b, r // tm),
        in_specs=[pl.BlockSpec((1, tm, k), lambda bi, i: (bi, i, 0)),
                  pl.BlockSpec((k, d), lambda bi, i: (0, 0)),
                  pl.BlockSpec((1, tm, d), lambda bi, i: (bi, i, 0)),
                  pl.BlockSpec((1, 1, d), lambda bi, i: (bi, 0, 0))],
        out_specs=pl.BlockSpec((1, tm, d), lambda bi, i: (bi, i, 0)),
        out_shape=jax.ShapeDtypeStruct((b, r, d), F32),
        compiler_params=_cparams(("parallel", "parallel"), 48),
        name=name,
    )(a, w, x, gate)


FFN_ROW_TILE = 1024
FFN_VMEM_MIB = 56


def _ffn_kernel(x_ref, g_ref, sh_ref, sc_ref, gate_ref, wg_ref, wu_ref, wo_ref, o_ref, f_ref):
    j = pl.program_id(2)
    last = pl.num_programs(2) - 1

    def partial_out(f):
        half = wg_ref.shape[1] // 2
        halves = (slice(0, half), slice(half, 2 * half))
        pre = [(_dot(f, wg_ref[:, cs]), _dot(f, wu_ref[:, cs])) for cs in halves]
        return [_dot((_silu(gt) * up).astype(BF16), wo_ref[cs, :]) for (gt, up), cs in zip(pre, halves)]

    @pl.when(j == 0)
    def _():
        f = _adaln(x_ref[0], g_ref[...], sh_ref[0], sc_ref[0]).astype(BF16)
        f_ref[...] = f
        pa, pb = partial_out(f)
        o_ref[0] = pa
        o_ref[0] += pb

    @pl.when((j > 0) & (j < last))
    def _():
        for part in partial_out(f_ref[...]):
            o_ref[0] += part

    @pl.when(j == last)
    def _():
        pa, pb = partial_out(f_ref[...])
        o_ref[0] += pa
        o_ref[0] = x_ref[0] + gate_ref[0] * (o_ref[0] + pb)


def _ffn(x, g, shift, scale, gate, w_in, w_out, name="ffn"):
    b, r, d = x.shape
    hidden = w_out.shape[0]
    tm = min(r, FFN_ROW_TILE)
    tf = 512
    nf = hidden // tf
    assert hidden % tf == 0 and nf >= 2
    vec = pl.BlockSpec((1, 1, d), lambda bi, i, j: (bi, 0, 0))
    return pl.pallas_call(
        _ffn_kernel,
        grid=(b, r // tm, nf),
        in_specs=[pl.BlockSpec((1, tm, d), lambda bi, i, j: (bi, i, 0)),
                  pl.BlockSpec((1, d), lambda bi, i, j: (0, 0)),
                  vec, vec, vec,
                  pl.BlockSpec((d, tf), lambda bi, i, j: (0, j)),
                  pl.BlockSpec((d, tf), lambda bi, i, j: (0, j + nf)),
                  pl.BlockSpec((tf, d), lambda bi, i, j: (j, 0))],
        out_specs=pl.BlockSpec((1, tm, d), lambda bi, i, j: (bi, i, 0)),
        out_shape=jax.ShapeDtypeStruct((b, r, d), F32),
        scratch_shapes=[pltpu.VMEM((tm, d), BF16)],
        compiler_params=_cparams(("parallel", "parallel", "arbitrary"), FFN_VMEM_MIB),
        name=name,
    )(x, g, shift, scale, gate, w_in, w_in, w_out)


def _softplus(x):
    return jnp.maximum(x, 0.0) + jnp.log1p(jnp.exp(-jnp.abs(x)))


def _gelu_tanh(x):
    return 0.5 * x * (1.0 + jnp.tanh(math.sqrt(2.0 / math.pi) * (x + 0.044715 * (x * x * x))))


def _rglru_tile(rec, prev8, next8, wg_ref, gb_ref, lam_ref, cw_ref, cb_ref,
                ubuf, a_s, b_s, h_s, hcar, *, reverse):
    tt = rec.shape[0]
    ubuf[0:8, :] = prev8
    ubuf[8:8 + tt, :] = rec
    ubuf[8 + tt:16 + tt, :] = next8
    base = 8 - CONV_PAD_LEFT
    u = cb_ref[...] + cw_ref[0:1, :] * ubuf[base:base + tt, :]
    for j in range(1, CONV_WIDTH):
        u = u + cw_ref[j:j + 1, :] * ubuf[base + j:base + j + tt, :]
    sp = _softplus(-lam_ref[...])
    for k in range(LRU_BLOCKS):
        cs = slice(k * LRU_BLOCK, (k + 1) * LRU_BLOCK)
        uk = u[:, cs]
        gts = _dot(uk.astype(BF16), wg_ref[k])
        r = _sigmoid(gts[:, :LRU_BLOCK] + gb_ref[0:1, cs])
        i = _sigmoid(gts[:, LRU_BLOCK:] + gb_ref[1:2, cs])
        log_a = (-LRU_C) * r * sp[:, cs]
        a = jnp.exp(log_a)
        a_s[:, cs] = a
        b_s[:, cs] = jnp.sqrt(-jnp.tanh(log_a) * (a * a + 1.0)) * (i * uk)

    def body(t, h):
        row = (tt - 1 - t) if reverse else t
        h = a_s[pl.ds(row, 1), :] * h + b_s[pl.ds(row, 1), :]
        h_s[pl.ds(row, 1), :] = h
        return h

    hcar[...] = lax.fori_loop(0, tt, body, hcar[...], unroll=8)


def _rglru_kernel(*refs, reverse, tt, nt):
    if reverse:
        (gl_ref, rl_ref, pv_ref, nx_ref, gc_ref, rc_ref, hfl_ref, hfc_ref,
         wg_ref, gb_ref, lam_ref, cw_ref, cb_ref, ol_ref, oc_ref, ubuf, a_s, b_s, h_s, hcar) = refs
    else:
        (rl_ref, pv_ref, nx_ref, rc_ref,
         wg_ref, gb_ref, lam_ref, cw_ref, cb_ref, ol_ref, oc_ref, ubuf, a_s, b_s, h_s, hcar) = refs
    s = pl.program_id(1)
    tile = functools.partial(_rglru_tile, wg_ref=wg_ref, gb_ref=gb_ref, lam_ref=lam_ref, cw_ref=cw_ref,
                             cb_ref=cb_ref, ubuf=ubuf, a_s=a_s, b_s=b_s, h_s=h_s, hcar=hcar, reverse=reverse)
    zeros8 = jnp.zeros((8, rl_ref.shape[2]), F32)

    @pl.when(s == 0)
    def _():
        hcar[...] = jnp.zeros_like(hcar)
        tile(rc_ref[0].astype(F32), zeros8, zeros8)
        if reverse:
            oc_ref[0] = (_gelu_tanh(gc_ref[0].astype(F32)) * (hfc_ref[0].astype(F32) + h_s[...])).astype(oc_ref.dtype)
        else:
            oc_ref[0] = h_s[...].astype(oc_ref.dtype)

    @pl.when(s > 0)
    def _():
        tl = (nt - s) if reverse else (s - 1)
        has_prev = (tl > 0).astype(F32)
        has_next = (tl < nt - 1).astype(F32)
        prev8 = pv_ref[0].astype(F32)[8:16, :] * has_prev
        next8 = nx_ref[0].astype(F32)[0:8, :] * has_next
        tile(rl_ref[0].astype(F32), prev8, next8)
        if reverse:
            ol_ref[0] = (_gelu_tanh(gl_ref[0].astype(F32)) * (hfl_ref[0].astype(F32) + h_s[...])).astype(ol_ref.dtype)
        else:
            ol_ref[0] = h_s[...].astype(ol_ref.dtype)


def _rglru_scan(gr_lat, gr_ctx, wg, gb, lam, cw, cb, hf_lat=None, hf_ctx=None):
    reverse = hf_lat is not None
    b, s_len, w2 = gr_lat.shape
    w = w2 // 2
    tt = gr_ctx.shape[1]
    assert s_len % tt == 0 and tt % 16 == 0
    nt = s_len // tt
    hb = tt // 16

    def lat_tile(si):
        return (nt - jnp.maximum(si, 1)) if reverse else jnp.maximum(si - 1, 0)

    lat_rows = lambda col: pl.BlockSpec((1, tt, w), lambda bi, si: (bi, lat_tile(si), col))
    ctx_rows = lambda col: pl.BlockSpec((1, tt, w), lambda bi, si: (bi, 0, col))
    prev_spec = pl.BlockSpec((1, 16, w), lambda bi, si: (bi, jnp.maximum(lat_tile(si) * hb - 1, 0), 1))
    next_spec = pl.BlockSpec((1, 16, w), lambda bi, si: (bi, jnp.minimum((lat_tile(si) + 1) * hb, nt * hb - 1), 1))
    full = lambda shape: pl.BlockSpec(shape, lambda bi, si: (0,) * len(shape))
    params = [wg, gb, lam, cw, cb]
    param_specs = [full(wg.shape), full(gb.shape), full(lam.shape), full(cw.shape), full(cb.shape)]
    if reverse:
        args = [gr_lat, gr_lat, gr_lat, gr_lat, gr_ctx, gr_ctx, hf_lat, hf_ctx] + params
        in_specs = [lat_rows(0), lat_rows(1), prev_spec, next_spec, ctx_rows(0), ctx_rows(1),
                    lat_rows(0), ctx_rows(0)] + param_specs
    else:
        args = [gr_lat, gr_lat, gr_lat, gr_ctx] + params
        in_specs = [lat_rows(1), prev_spec, next_spec, ctx_rows(1)] + param_specs
    return pl.pallas_call(
        functools.partial(_rglru_kernel, reverse=reverse, tt=tt, nt=nt),
        grid=(b, nt + 1),
        in_specs=in_specs,
        out_specs=[lat_rows(0), ctx_rows(0)],
        out_shape=[jax.ShapeDtypeStruct((b, s_len, w), BF16), jax.ShapeDtypeStruct((b, tt, w), BF16)],
        scratch_shapes=[pltpu.VMEM((tt + 16, w), F32), pltpu.VMEM((tt, w), F32), pltpu.VMEM((tt, w), F32),
                        pltpu.VMEM((tt, w), F32), pltpu.VMEM((1, w), F32)],
        compiler_params=_cparams(("parallel", "arbitrary"), 48),
        name="rglru_bwd" if reverse else "rglru_fwd",
    )(*args)


def _gla_mask(reverse):
    r_i = lax.broadcasted_iota(jnp.int32, (GLA_CHUNK, GLA_CHUNK), 0)
    c_i = lax.broadcasted_iota(jnp.int32, (GLA_CHUNK, GLA_CHUNK), 1)
    return (c_i >= r_i) if reverse else (c_i <= r_i)


def _hgrn_head(q, f, v, st_t, mask, reverse):
    rows = q.shape[0]
    c = GLA_CHUNK
    half = c // 2
    nchunk = rows // c
    chunks = [slice(n * c, (n + 1) * c) for n in range(nchunk)]
    tri = jnp.where(mask, 1.0, 0.0).astype(BF16)
    g = jnp.log(f)
    k = 1.0 - f
    hi = g.astype(BF16)
    r1 = g - hi.astype(F32)
    mid = r1.astype(BF16)
    lo = (r1 - mid.astype(F32)).astype(BF16)
    g3 = jnp.concatenate([hi, mid, lo], axis=1)
    c3s = [_dot(tri, g3[rs]) for rs in chunks]
    yield
    cums = [(c3[:, :LANES] + c3[:, LANES:2 * LANES]) + c3[:, 2 * LANES:] for c3 in c3s]
    if reverse:
        totals = [cm[0:1] for cm in cums]
        refs = [cm[half:half + 1] for cm in cums]
    else:
        totals = [cm[c - 1:c] for cm in cums]
        refs = [cm[half - 1:half] for cm in cums]
    cum = jnp.concatenate(cums, axis=0)
    ref_b = jnp.concatenate([jnp.broadcast_to(r, (c, LANES)) for r in refs], axis=0)
    e_q = jnp.exp(cum - ref_b)
    qt = q * e_q
    kt = k * (1.0 / e_q)
    qtb, ktb, vb = qt.astype(BF16), kt.astype(BF16), v.astype(BF16)
    raw, upds, decays = [], [], []
    for n, rs in enumerate(chunks):
        raw.append(_dot_nt(qtb[rs], ktb[rs]))
        kbar = (kt[rs] * jnp.exp(totals[n] - refs[n])).astype(BF16)
        upds.append(_dot_tn(vb[rs], kbar))
        decays.append(jnp.exp(totals[n]))
    yield
    outs = [_dot(jnp.where(mask, s, 0.0).astype(BF16), vb[rs]) for s, rs in zip(raw, chunks)]
    yield
    for n in (range(nchunk - 1, -1, -1) if reverse else range(nchunk)):
        qi = (qt[chunks[n]] * jnp.exp(refs[n])).astype(BF16)
        outs[n] = outs[n] + _dot_nt(qi, st_t.astype(BF16))
        st_t = st_t * decays[n] + upds[n]
    return jnp.concatenate(outs, axis=0), st_t


def _run_in_lockstep(generators):
    results = [None] * len(generators)
    active = list(range(len(generators)))
    while active:
        for i in list(active):
            try:
                next(generators[i])
            except StopIteration as done:
                results[i] = done.value
                active.remove(i)
    return results


HGRN_HEADS_PER_STEP = 8


def _hgrn_kernel(*refs, reverse, layer):
    if reverse:
        (ql, fl, il, gl, ofl, qc, fc, ic, gc, ofc, lg_ref, gn_ref, ol_ref, oc_ref, st_ref) = refs
    else:
        (ql, fl, il, qc, fc, ic, lg_ref, ol_ref, oc_ref, st_ref) = refs
    is_ctx = pl.program_id(2) == 0

    @pl.when(is_ctx)
    def _():
        st_ref[...] = jnp.zeros_like(st_ref)

    def pick(c_ref, l_ref, hh):
        return jnp.where(is_ctx, c_ref[0, hh], l_ref[0, hh]).astype(F32)

    mask = _gla_mask(reverse)
    heads = []
    for hh in range(HGRN_HEADS_PER_STEP):
        cs = slice(hh * HGRN_HEAD_DIM, (hh + 1) * HGRN_HEAD_DIM)
        lg = lg_ref[:, cs]
        e = jnp.exp(lg - jnp.max(lg, axis=0, keepdims=True))
        sm = e / jnp.sum(e, axis=0, keepdims=True)
        lb = jnp.zeros((1, HGRN_HEAD_DIM), F32)
        for l in range(1, layer + 1):
            lb = lb + sm[l:l + 1]
        q = _silu(pick(qc, ql, hh))
        f = lb + (1.0 - lb) * _sigmoid(pick(fc, fl, hh))
        heads.append(_hgrn_head(q, f, pick(ic, il, hh), st_ref[hh], mask, reverse))
    outs = []
    for hh, (o, st_new) in enumerate(_run_in_lockstep(heads)):
        st_ref[hh] = st_new
        if reverse:
            o = (_rms(pick(ofc, ofl, hh) + o) * gn_ref[...]) * _silu(pick(gc, gl, hh))
        outs.append(o.astype(ol_ref.dtype))

    def store(out_ref):
        for hh, ob in enumerate(outs):
            if reverse:
                out_ref[0, :, hh * HGRN_HEAD_DIM:(hh + 1) * HGRN_HEAD_DIM] = ob
            else:
                out_ref[0, hh] = ob

    @pl.when(is_ctx)
    def _():
        store(oc_ref)

    @pl.when(jnp.logical_not(is_ctx))
    def _():
        store(ol_ref)


def _hgrn_scan(p_lat, p_ctx, logits_d, layer, gn=None, of_lat=None, of_ctx=None):
    reverse = of_lat is not None
    b, _, s_len, hd = p_lat.shape
    h, hps = HGRN_HEADS, HGRN_HEADS_PER_STEP
    tt = p_ctx.shape[2]
    assert s_len % tt == 0 and tt % GLA_CHUNK == 0 and h % hps == 0
    nt = s_len // tt
    ng = h // hps
    fsel = 2 if reverse else 1

    def lat_tile(si):
        return (nt - jnp.maximum(si, 1)) if reverse else jnp.maximum(si - 1, 0)

    lat = lambda grp: pl.BlockSpec((1, hps, tt, hd), lambda bi, hi, si: (bi, grp * ng + hi, lat_tile(si), 0))
    ctx = lambda grp: pl.BlockSpec((1, hps, tt, hd), lambda bi, hi, si: (bi, grp * ng + hi, 0, 0))
    lg_spec = pl.BlockSpec((logits_d.shape[0], hps * hd), lambda bi, hi, si: (0, hi))
    if reverse:
        args = [p_lat, p_lat, p_lat, p_lat, of_lat, p_ctx, p_ctx, p_ctx, p_ctx, of_ctx, logits_d, gn]
        in_specs = [lat(0), lat(fsel), lat(3), lat(4), lat(0), ctx(0), ctx(fsel), ctx(3), ctx(4), ctx(0),
                    lg_spec, pl.BlockSpec((1, hd), lambda bi, hi, si: (0, 0))]
        out_specs = [pl.BlockSpec((1, tt, hps * hd), lambda bi, hi, si: (bi, lat_tile(si), hi)),
                     pl.BlockSpec((1, tt, hps * hd), lambda bi, hi, si: (bi, 0, hi))]
        out_shape = [jax.ShapeDtypeStruct((b, s_len, h * hd), BF16), jax.ShapeDtypeStruct((b, tt, h * hd), BF16)]
    else:
        args = [p_lat, p_lat, p_lat, p_ctx, p_ctx, p_ctx, logits_d]
        in_specs = [lat(0), lat(fsel), lat(3), ctx(0), ctx(fsel), ctx(3), lg_spec]
        out_specs = [lat(0), ctx(0)]
        out_shape = [jax.ShapeDtypeStruct((b, h, s_len, hd), BF16), jax.ShapeDtypeStruct((b, h, tt, hd), BF16)]
    return pl.pallas_call(
        functools.partial(_hgrn_kernel, reverse=reverse, layer=layer),
        grid=(b, ng, nt + 1),
        in_specs=in_specs,
        out_specs=out_specs,
        out_shape=out_shape,
        scratch_shapes=[pltpu.VMEM((hps, hd, hd), F32)],
        compiler_params=_cparams(("parallel", "parallel", "arbitrary"), 32),
        name="hgrn_bwd" if reverse else "hgrn_fwd",
    )(*args)


def _rope_tables(n_lat, rot_dim, reps):
    n_freq = rot_dim // 4
    t = jnp.arange(n_lat)
    inv = ROPE_THETA ** (-jnp.arange(n_freq, dtype=F32) / n_freq)
    ang_r = (t // GRID_W).astype(F32)[:, None] * inv[None, :]
    ang_c = (t % GRID_W).astype(F32)[:, None] * inv[None, :]
    cos = jnp.concatenate([jnp.cos(ang_r)] * 2 + [jnp.cos(ang_c)] * 2, axis=-1)
    sin = jnp.concatenate([-jnp.sin(ang_r), jnp.sin(ang_r), -jnp.sin(ang_c), jnp.sin(ang_c)], axis=-1)
    return jnp.concatenate([jnp.tile(cos, (1, reps)), jnp.tile(sin, (1, reps))], axis=-1)


def _rope(x, cos, sin, n_freq):
    width = x.shape[1]
    lane = lax.broadcasted_iota(jnp.int32, (1, width), 1)
    first = (lane % (2 * n_freq)) < n_freq
    partner = jnp.where(first, pltpu.roll(x, width - n_freq, 1), pltpu.roll(x, n_freq, 1))
    return x * cos + partner * sin


def _mla_qkv_kernel(*refs, use_rope):
    if use_rope:
        (c_ref, qg_ref, kvg_ref, wq_ref, wk_ref, wvt_ref, gqn_ref, gqr_ref, gkn_ref, gkr_ref, tab_ref,
         q_ref, k_ref, vt_ref) = refs
    else:
        (c_ref, qg_ref, kvg_ref, wq_ref, wk_ref, wvt_ref, gqn_ref, gqr_ref, gkn_ref, gkr_ref,
         q_ref, k_ref, vt_ref) = refs
    c = c_ref[0].astype(F32)
    cqn = (_rms(c[:, :MLA_Q_RANK]) * qg_ref[...]).astype(BF16)
    ckvn = (_rms(c[:, MLA_Q_RANK:MLA_Q_RANK + MLA_KV_RANK]) * kvg_ref[...]).astype(BF16)
    kr = c[:, MLA_Q_RANK + MLA_KV_RANK:]
    qa = _dot(cqn, wq_ref[...])
    ka = _dot(ckvn, wk_ref[...])
    vt = _dot_nt(wvt_ref[...], ckvn)
    lane = lax.broadcasted_iota(jnp.int32, (1, LANES), 1)
    low = lane < MLA_ROPE
    if use_rope:
        cos, sin = tab_ref[:, :LANES], tab_ref[:, LANES:]
    inv_rope = 1.0 / MLA_ROPE

    def halves_rms(x):
        sq = x * x
        ss_lo = jnp.sum(jnp.where(low, sq, 0.0), axis=-1, keepdims=True)
        ss_hi = jnp.sum(jnp.where(low, 0.0, sq), axis=-1, keepdims=True)
        return x * jnp.where(low, lax.rsqrt(ss_lo * inv_rope + NORM_EPS), lax.rsqrt(ss_hi * inv_rope + NORM_EPS))

    krn = halves_rms(kr) * gkr_ref[...]
    if use_rope:
        krn = _rope(krn, cos, sin, MLA_ROPE // 4)
    krn = jnp.where(low, krn, 0.0)
    nope_w = MLA_HEADS * MLA_NOPE
    qscale = MLA_SCALE * LOG2E
    for p in range(MLA_HEADS // 2):
        qr = halves_rms(qa[:, nope_w + p * LANES:nope_w + (p + 1) * LANES]) * gqr_ref[...]
        if use_rope:
            qr = _rope(qr, cos, sin, MLA_ROPE // 4)
        for e in range(2):
            h = 2 * p + e
            hs = slice(h * MLA_NOPE, (h + 1) * MLA_NOPE)
            qn = _rms(qa[:, hs]) * gqn_ref[...]
            rot = qr if e == 0 else pltpu.roll(qr, MLA_ROPE, 1)
            rot = jnp.where(low, rot, 0.0)
            q_ref[0, h] = (jnp.concatenate([qn, rot], axis=1) * qscale).astype(BF16)
            kn = _rms(ka[:, hs]) * gkn_ref[...]
            k_ref[0, h] = jnp.concatenate([kn, krn], axis=1).astype(BF16)
            vt_ref[0, h, 0, :MLA_V, :] = vt[hs, :].astype(BF16)
            vt_ref[0, h, 0, MLA_V:, :] = jnp.ones((MLA_V_ROWS - MLA_V, vt.shape[1]), BF16)


def _mla_qkv(c, qg, kvg, wq, wk, wvt, gqn, gqr, gkn, gkr, tab, tm):
    b, r, cw = c.shape
    assert r % tm == 0
    h = MLA_HEADS
    use_rope = tab is not None
    full = lambda a: pl.BlockSpec(a.shape, lambda bi, i: (0,) * a.ndim)
    args = [c, qg, kvg, wq, wk, wvt, gqn, gqr, gkn, gkr]
    in_specs = [pl.BlockSpec((1, tm, cw), lambda bi, i: (bi, i, 0))] + [full(a) for a in args[1:]]
    if use_rope:
        args.append(tab)
        in_specs.append(pl.BlockSpec((tm, tab.shape[1]), lambda bi, i: (i, 0)))
    return pl.pallas_call(
        functools.partial(_mla_qkv_kernel, use_rope=use_rope),
        grid=(b, r // tm),
        in_specs=in_specs,
        out_specs=[pl.BlockSpec((1, h, tm, MLA_QK_PAD), lambda bi, i: (bi, 0, i, 0)),
                   pl.BlockSpec((1, h, tm, MLA_QK_PAD), lambda bi, i: (bi, 0, i, 0)),
                   pl.BlockSpec((1, h, 1, MLA_V_ROWS, tm), lambda bi, i: (bi, 0, i, 0, 0))],
        out_shape=[jax.ShapeDtypeStruct((b, h, r, MLA_QK_PAD), BF16),
                   jax.ShapeDtypeStruct((b, h, r, MLA_QK_PAD), BF16),
                   jax.ShapeDtypeStruct((b, h, r // tm, MLA_V_ROWS, tm), BF16)],
        compiler_params=_cparams(("parallel", "parallel"), 48),
        name="mla_qkv_rope" if use_rope else "mla_qkv",
    )(*args)


MLA_KV_TILES_PER_TRIP = 4
MLA_Q_CHUNK = 512
MLA_Q_CHUNKS_PER_STEP = 2


def _attn_update(s, vt, m, acc):
    m_new = jnp.maximum(m, jnp.max(s, axis=0, keepdims=True))
    alpha = jnp.exp2(m - m_new)
    p = jnp.exp2(s - m_new).astype(BF16)
    return m_new, alpha * acc + _dot(vt, p)


MLA_STALE_MARGIN = 100.0
MLA_STALE_BOUND = (126.0 + MLA_STALE_MARGIN) / 2.0


def _attn_update_stale(s, vt, m, acc):
    p = jnp.exp2(s - m).astype(BF16)
    m_new = jnp.maximum(m, jnp.max(s, axis=0, keepdims=True))
    return m_new, (acc + _dot(vt, p)) * jnp.exp2(m - m_new)


def _mla_attn_stale(qs, kc_ref, vc_ref, kl_ref, vl_ref, floor):
    nk, tk = vl_ref.shape[2], vl_ref.shape[4]
    unroll = min(MLA_KV_TILES_PER_TRIP, nk)
    ms, accs = [], []
    for q in qs:
        m = jnp.full((1, q.shape[0]), floor, F32)
        acc = jnp.zeros((vc_ref.shape[3], q.shape[0]), F32)
        m, acc = _attn_update(_dot_nt(kc_ref[0, 0], q), vc_ref[0, 0, 0], m, acc)
        ms.append(m)
        accs.append(acc)

    def body(i, carry):
        ms, accs = list(carry[0]), list(carry[1])
        for u in range(unroll):
            t = unroll * i + u
            k = kl_ref[0, 0, pl.ds(pl.multiple_of(t * tk, tk), tk), :]
            vt = vl_ref[0, 0, t]
            ss = [_dot_nt(k, q) for q in qs]
            for c in range(len(qs)):
                ms[c], accs[c] = _attn_update_stale(ss[c], vt, ms[c], accs[c])
        return tuple(ms), tuple(accs)

    return lax.fori_loop(0, nk // unroll, body, (tuple(ms), tuple(accs)))[1]


def _mla_attn_online(q, kc_ref, vc_ref, kl_ref, vl_ref):
    nk, tk = vl_ref.shape[2], vl_ref.shape[4]
    unroll = min(MLA_KV_TILES_PER_TRIP, nk)
    m = jnp.full((1, q.shape[0]), -jnp.inf, F32)
    acc = jnp.zeros((vc_ref.shape[3], q.shape[0]), F32)

    def scores(j):
        return _dot_nt(kl_ref[0, 0, pl.ds(pl.multiple_of(j * tk, tk), tk), :], q)

    s_cur = scores(0)
    m, acc = _attn_update(_dot_nt(kc_ref[0, 0], q), vc_ref[0, 0, 0], m, acc)

    def body(i, carry):
        s_cur, m, acc = carry
        for u in range(unroll):
            t = unroll * i + u
            s_next = scores(t + 1)
            m, acc = _attn_update(s_cur, vl_ref[0, 0, t], m, acc)
            s_cur = s_next
        return s_cur, m, acc

    s_cur, m, acc = lax.fori_loop(0, nk // unroll - 1, body, (s_cur, m, acc))
    for t in range(nk - unroll, nk):
        s_next = scores(t + 1) if t + 1 < nk else None
        m, acc = _attn_update(s_cur, vl_ref[0, 0, t], m, acc)
        s_cur = s_next
    return acc


def _mla_attn_kernel(*refs, with_lat, chunk):
    if with_lat:
        bound_ref, q_ref, kc_ref, vc_ref, kl_ref, vl_ref, o_ref = refs
    else:
        q_ref, kc_ref, vc_ref, o_ref = refs
    tq = q_ref.shape[2]
    rows = [slice(c * chunk, (c + 1) * chunk) for c in range(tq // chunk)]

    def finish(acc, rs):
        o_ref[0, rs, :] = (acc[:MLA_V] / acc[MLA_V:MLA_V + 1]).T.astype(o_ref.dtype)

    if with_lat:
        bound = bound_ref[0]

        @pl.when(bound < MLA_STALE_BOUND)
        def _():
            accs = _mla_attn_stale([q_ref[0, 0, rs, :] for rs in rows], kc_ref, vc_ref, kl_ref, vl_ref,
                                   bound - MLA_STALE_MARGIN)
            for acc, rs in zip(accs, rows):
                finish(acc, rs)

        @pl.when(bound >= MLA_STALE_BOUND)
        def _():
            for rs in rows:
                finish(_mla_attn_online(q_ref[0, 0, rs, :], kc_ref, vc_ref, kl_ref, vl_ref), rs)
    else:
        for rs in rows:
            q = q_ref[0, 0, rs, :]
            m = jnp.full((1, chunk), -jnp.inf, F32)
            acc = jnp.zeros((vc_ref.shape[3], chunk), F32)
            finish(_attn_update(_dot_nt(kc_ref[0, 0], q), vc_ref[0, 0, 0], m, acc)[1], rs)


def _mla_attention(q, k_ctx, vt_ctx, k_lat=None, vt_lat=None, bound=None):
    b, h, n, dq = q.shape
    with_lat = k_lat is not None
    chunk = min(n, MLA_Q_CHUNK)
    tq = min(n, MLA_Q_CHUNKS_PER_STEP * chunk)
    assert n % tq == 0
    rc = k_ctx.shape[2]
    args = [q, k_ctx, vt_ctx]
    in_specs = [pl.BlockSpec((1, 1, tq, dq), lambda bi, hi, i: (bi, hi, i, 0)),
                pl.BlockSpec((1, 1, rc, dq), lambda bi, hi, i: (bi, hi, 0, 0)),
                pl.BlockSpec((1, 1, 1, MLA_V_ROWS, rc), lambda bi, hi, i: (bi, hi, 0, 0, 0))]
    if with_lat:
        nk, tk = vt_lat.shape[2], vt_lat.shape[4]
        assert nk % min(MLA_KV_TILES_PER_TRIP, nk) == 0
        args = [bound] + args + [k_lat, vt_lat]
        in_specs = [pl.BlockSpec(memory_space=pltpu.SMEM)] + in_specs
        in_specs += [pl.BlockSpec((1, 1, nk * tk, dq), lambda bi, hi, i: (bi, hi, 0, 0)),
                     pl.BlockSpec((1, 1, nk, MLA_V_ROWS, tk), lambda bi, hi, i: (bi, hi, 0, 0, 0))]
    return pl.pallas_call(
        functools.partial(_mla_attn_kernel, with_lat=with_lat, chunk=chunk),
        grid=(b, h, n // tq),
        in_specs=in_specs,
        out_specs=pl.BlockSpec((1, tq, MLA_V), lambda bi, hi, i: (bi, i, hi)),
        out_shape=jax.ShapeDtypeStruct((b, n, h * MLA_V), BF16),
        compiler_params=_cparams(("parallel", "parallel", "arbitrary"), 48),
        name="mla_attn" if with_lat else "mla_attn_ctx",
    )(*args)


def _swa_inproj_kernel(*refs, use_rope, n_norm_tiles):
    if use_rope:
        x_ref, g_ref, sh_ref, sc_ref, w_ref, hg_ref, tab_ref, o_ref, h_ref = refs
    else:
        x_ref, g_ref, sh_ref, sc_ref, w_ref, hg_ref, o_ref, h_ref = refs
    j = pl.program_id(2)

    @pl.when(j == 0)
    def _():
        h_ref[...] = _adaln(x_ref[0], g_ref[...], sh_ref[0], sc_ref[0]).astype(BF16)

    res = _dot(h_ref[...], w_ref[...])

    @pl.when(j < n_norm_tiles)
    def _():
        for e in range(res.shape[1] // SWA_HEAD_DIM):
            cs = slice(e * SWA_HEAD_DIM, (e + 1) * SWA_HEAD_DIM)
            xh = _rms(res[:, cs]) * hg_ref[:, cs]
            if use_rope:
                xh = _rope(xh, tab_ref[:, :SWA_HEAD_DIM], tab_ref[:, SWA_HEAD_DIM:], SWA_HEAD_DIM // 4)
            o_ref[0, :, cs] = xh.astype(o_ref.dtype)

    @pl.when(j >= n_norm_tiles)
    def _():
        o_ref[0] = res.astype(o_ref.dtype)


def _swa_inproj(x, g, shift, scale, w, head_gain, tab):
    b, r, d = x.shape
    n = w.shape[1]
    tm = min(r, 512)
    tn = 512
    use_rope = tab is not None
    n_norm_tiles = (SWA_Q_HEADS + SWA_KV_HEADS) * SWA_HEAD_DIM // tn
    vec = pl.BlockSpec((1, 1, d), lambda bi, i, j: (bi, 0, 0))
    args = [x, g, shift, scale, w, head_gain]
    in_specs = [pl.BlockSpec((1, tm, d), lambda bi, i, j: (bi, i, 0)),
                pl.BlockSpec((1, d), lambda bi, i, j: (0, 0)),
                vec, vec,
                pl.BlockSpec((d, tn), lambda bi, i, j: (0, j)),
                pl.BlockSpec((1, tn), lambda bi, i, j: (0, j))]
    if use_rope:
        args.append(tab)
        in_specs.append(pl.BlockSpec((tm, tab.shape[1]), lambda bi, i, j: (i, 0)))
    return pl.pallas_call(
        functools.partial(_swa_inproj_kernel, use_rope=use_rope, n_norm_tiles=n_norm_tiles),
        grid=(b, r // tm, n // tn),
        in_specs=in_specs,
        out_specs=pl.BlockSpec((1, tm, tn), lambda bi, i, j: (bi, i, j)),
        out_shape=jax.ShapeDtypeStruct((b, r, n), BF16),
        scratch_shapes=[pltpu.VMEM((tm, d), BF16)],
        compiler_params=_cparams(("parallel", "parallel", "arbitrary"), 48),
        name="swa_inproj_rope" if use_rope else "swa_inproj",
    )(*args)


def _swa_block(q, keys, vals, sinks, nctx, prev_ok, next_ok):
    blk, grp = SWA_BLOCK, SWA_GROUP
    qs = jnp.concatenate([q[:, e * SWA_HEAD_DIM:(e + 1) * SWA_HEAD_DIM] for e in range(grp)], axis=0)
    s = _dot_nt(qs, jnp.concatenate(keys, axis=0))
    yield
    rows = s.shape[0]
    r_i = lax.broadcasted_iota(jnp.int32, (rows, blk), 0) % blk
    c_i = lax.broadcasted_iota(jnp.int32, (rows, blk), 1)
    off_prev = jnp.where(prev_ok, 0, 2 * blk)
    off_next = jnp.where(next_ok, 0, 2 * blk)
    s_prev = jnp.where(c_i >= r_i + off_prev, s[:, nctx:nctx + blk], MASK_VALUE)
    s_next = jnp.where(c_i + off_next <= r_i, s[:, nctx + 2 * blk:], MASK_VALUE)
    s = jnp.concatenate([s[:, :nctx], s_prev, s[:, nctx + blk:nctx + 2 * blk], s_next], axis=1)
    row1 = lax.broadcasted_iota(jnp.int32, (rows, 1), 0)
    sink = jnp.zeros((rows, 1), F32)
    for e in range(grp):
        sink = jnp.where(row1 // blk == e, sinks[e], sink)
    m = jnp.maximum(jnp.max(s, axis=1, keepdims=True), sink)
    p = jnp.exp2(s - m)
    l = jnp.sum(p, axis=1, keepdims=True) + jnp.exp2(sink - m)
    o = _dot(p.astype(BF16), jnp.concatenate(vals, axis=0)) / l
    return jnp.concatenate([o[e * blk:(e + 1) * blk] for e in range(grp)], axis=1)


def _swa_attn_kernel(sink_ref, q_ref, kp_ref, kc_ref, kn_ref, vp_ref, vc_ref, vn_ref, kx_ref, vx_ref, o_ref):
    hk = pl.program_id(1)
    i = pl.program_id(2)
    last = pl.num_programs(2) - 1
    blk = SWA_BLOCK
    nctx = kx_ref.shape[1]
    sinks = [sink_ref[hk * SWA_GROUP + e] * LOG2E for e in range(SWA_GROUP)]
    k0, k1 = kc_ref[0, :blk, :], kc_ref[0, blk:, :]
    v0, v1 = vc_ref[0, :blk, :], vc_ref[0, blk:, :]
    o_a, o_b = _run_in_lockstep([
        _swa_block(q_ref[0, :blk, :], [kx_ref[0], kp_ref[0], k0, k1], [vx_ref[0], vp_ref[0], v0, v1],
                   sinks, nctx, i > 0, True),
        _swa_block(q_ref[0, blk:, :], [kx_ref[0], k0, k1, kn_ref[0]], [vx_ref[0], v0, v1, vn_ref[0]],
                   sinks, nctx, True, i < last)])
    o_ref[0, :blk, :] = o_a.astype(o_ref.dtype)
    o_ref[0, blk:, :] = o_b.astype(o_ref.dtype)


def _swa_attention(qkv_lat, qkv_ctx, sink):
    b, n, _ = qkv_lat.shape
    nctx = qkv_ctx.shape[1]
    blk, dh, grp = SWA_BLOCK, SWA_HEAD_DIM, SWA_GROUP
    nb = n // blk
    assert nb % 2 == 0
    kcol = SWA_Q_HEADS
    vcol = SWA_Q_HEADS + SWA_KV_HEADS
    prev = lambda col0: pl.BlockSpec((1, blk, dh), lambda bi, hk, i: (bi, jnp.maximum(2 * i - 1, 0), col0 + hk))
    nxt = lambda col0: pl.BlockSpec((1, blk, dh), lambda bi, hk, i: (bi, jnp.minimum(2 * i + 2, nb - 1), col0 + hk))
    pair = lambda col0: pl.BlockSpec((1, 2 * blk, dh), lambda bi, hk, i: (bi, i, col0 + hk))
    ctx = lambda col0: pl.BlockSpec((1, nctx, dh), lambda bi, hk, i: (bi, 0, col0 + hk))
    return pl.pallas_call(
        _swa_attn_kernel,
        grid=(b, SWA_KV_HEADS, nb // 2),
        in_specs=[pl.BlockSpec(memory_space=pltpu.SMEM),
                  pl.BlockSpec((1, 2 * blk, grp * dh), lambda bi, hk, i: (bi, i, hk)),
                  prev(kcol), pair(kcol), nxt(kcol),
                  prev(vcol), pair(vcol), nxt(vcol),
                  ctx(kcol), ctx(vcol)],
        out_specs=pl.BlockSpec((1, 2 * blk, grp * dh), lambda bi, hk, i: (bi, i, hk)),
        out_shape=jax.ShapeDtypeStruct((b, n, SWA_Q_HEADS * dh), BF16),
        compiler_params=_cparams(("parallel", "parallel", "arbitrary"), 32),
        name="swa_attn",
    )(sink, qkv_lat, qkv_lat, qkv_lat, qkv_lat, qkv_lat, qkv_lat, qkv_lat, qkv_ctx, qkv_ctx)


def kernel(x, c, ctx, c_ctx, mod_w, mod_b, norm_g, ffn_w_in, ffn_w_out, rglru_w_in, rglru_conv_w, rglru_conv_b, rglru_gate_w, rglru_gate_b, rglru_lambda, rglru_w_out, hgrn_w_in, hgrn_lb_logits, hgrn_gnorm_g, hgrn_w_out, mla_w_in, mla_q_norm_g, mla_kv_norm_g, mla_w_uq, mla_w_ukv, mla_qk_g, mla_w_out, swa_w_in, swa_qk_g, swa_sink, swa_w_out):
    b, n_lat, d = x.shape
    depth = mod_w.shape[0]
    assert depth == DEPTH and b + 1 <= 8
    bf = lambda a: a.astype(BF16)

    cond = jnp.concatenate([c, c_ctx[None, :], jnp.zeros((8 - b - 1, d), F32)], axis=0)
    mod = _modulation(cond, mod_w, mod_b).reshape(depth, 8, 6, d)

    x_lat, x_ctx = x, ctx
    for layer in range(depth):
        kind = layer % 4
        need_ctx = layer < depth - 1
        m_lat = [mod[layer, :b, k][:, None, :] for k in range(6)]
        m_ctx = [jnp.broadcast_to(mod[layer, b:b + 1, k][:, None, :], (b, 1, d)) for k in range(6)]
        g1 = norm_g[layer, 0][None, :]
        g2 = norm_g[layer, 1][None, :]
        y_ctx = None
        if kind == 0:
            w_in = bf(rglru_w_in[0])
            gr_lat = _inproj(x_lat, g1, m_lat[0], m_lat[1], w_in, tn=2048,name="rglru_inproj")
            gr_ctx = _inproj(x_ctx, g1, m_ctx[0], m_ctx[1], w_in, tn=2048,name="rglru_inproj_ctx")
            gw = rglru_gate_w[0]
            wg = [bf(jnp.concatenate([gw[dr, 0], gw[dr, 1]], axis=-1)) for dr in range(2)]
            cw, cb = rglru_conv_w[0], rglru_conv_b[0][None, :]
            scan = lambda dr, **kw: _rglru_scan(gr_lat, gr_ctx, wg[dr], rglru_gate_b[0, dr],
                                                rglru_lambda[0, dr][None, :], cw, cb, **kw)
            hf_lat, hf_ctx = scan(0)
            y_lat, y_ctx = scan(1, hf_lat=hf_lat, hf_ctx=hf_ctx)
            w_out = bf(rglru_w_out[0])
        elif kind == 1:
            w_in = bf(hgrn_w_in[0])
            p_lat = _inproj(x_lat, g1, m_lat[0], m_lat[1], w_in, tn=2048,head_major=True, name="hgrn_inproj")
            p_ctx = _inproj(x_ctx, g1, m_ctx[0], m_ctx[1], w_in, tn=2048,head_major=True, name="hgrn_inproj_ctx")
            of_lat, of_ctx = _hgrn_scan(p_lat, p_ctx, hgrn_lb_logits[:, 0, :], layer)
            y_lat, y_ctx = _hgrn_scan(p_lat, p_ctx, hgrn_lb_logits[:, 1, :], layer, gn=hgrn_gnorm_g[0][None, :],
                                      of_lat=of_lat, of_ctx=of_ctx)
            w_out = bf(hgrn_w_out[0])
        elif kind == 2:
            cw_real = mla_w_in.shape[2]
            cw_pad = -(-cw_real // LANES) * LANES
            w_in = bf(jnp.pad(mla_w_in[0], ((0, 0), (0, cw_pad - cw_real))))
            c_lat = _inproj(x_lat, g1, m_lat[0], m_lat[1], w_in, tn=cw_pad, name="mla_inproj")
            c_ctx_ = _inproj(x_ctx, g1, m_ctx[0], m_ctx[1], w_in, tn=cw_pad, name="mla_inproj_ctx")
            wq3 = mla_w_uq[0].reshape(MLA_Q_RANK, MLA_HEADS, MLA_NOPE + MLA_ROPE)
            wq = bf(jnp.concatenate([wq3[:, :, :MLA_NOPE].reshape(MLA_Q_RANK, -1),
                                     wq3[:, :, MLA_NOPE:].reshape(MLA_Q_RANK, -1)], axis=1))
            wkv3 = mla_w_ukv[0].reshape(MLA_KV_RANK, MLA_HEADS, MLA_NOPE + MLA_V)
            wk = bf(wkv3[:, :, :MLA_NOPE].reshape(MLA_KV_RANK, -1))
            wvt = bf(wkv3[:, :, MLA_NOPE:].reshape(MLA_KV_RANK, -1).T)
            qk_g = mla_qk_g[0]
            gqn, gkn = qk_g[0:1, :MLA_NOPE], qk_g[1:2, :MLA_NOPE]
            gqr = jnp.tile(qk_g[0:1, MLA_NOPE:], (1, LANES // MLA_ROPE))
            gkr = jnp.pad(qk_g[1:2, MLA_NOPE:], ((0, 0), (0, LANES - MLA_ROPE)))
            tab = _rope_tables(n_lat, MLA_ROPE, LANES // MLA_ROPE)
            small = (mla_q_norm_g[0][None, :], mla_kv_norm_g[0][None, :], wq, wk, wvt, gqn, gqr, gkn, gkr)
            q_l, k_l, vt_l = _mla_qkv(c_lat, *small, tab, min(n_lat, 512))
            q_c, k_c, vt_c = _mla_qkv(c_ctx_, *small, None, c_ctx_.shape[1])
            gmax2 = lambda g: jnp.max(g * g)
            qk_norm2 = ((MLA_NOPE * gmax2(gqn) + MLA_ROPE * gmax2(gqr))
                        * (MLA_NOPE * gmax2(gkn) + MLA_ROPE * gmax2(gkr)))
            bound = (1.05 * MLA_SCALE * LOG2E) * jnp.sqrt(qk_norm2).reshape(1)
            y_lat = _mla_attention(q_l, k_c, vt_c, k_l, vt_l, bound)
            if need_ctx:
                y_ctx = _mla_attention(q_c, k_c, vt_c)
            w_out = bf(mla_w_out[0])
        else:
            w_in = bf(swa_w_in[0])
            gq = jnp.tile(swa_qk_g[0, 0] * (SWA_SCALE * LOG2E), SWA_Q_HEADS)
            gk = jnp.tile(swa_qk_g[0, 1], SWA_KV_HEADS)
            head_gain = jnp.concatenate([gq, gk, jnp.ones((SWA_KV_HEADS * SWA_HEAD_DIM,), F32)])[None, :]
            tab = _rope_tables(n_lat, SWA_HEAD_DIM, 1)
            qkv_lat = _swa_inproj(x_lat, g1, m_lat[0], m_lat[1], w_in, head_gain, tab)
            qkv_ctx = _swa_inproj(x_ctx, g1, m_ctx[0], m_ctx[1], w_in, head_gain, None)
            y_lat = _swa_attention(qkv_lat, qkv_ctx, swa_sink[0])
            assert not need_ctx
            w_out = bf(swa_w_out[0])

        w_ffn_in, w_ffn_out = bf(ffn_w_in[layer]), bf(ffn_w_out[layer])
        x_lat = _outproj(y_lat, w_out, x_lat, m_lat[2])
        x_lat = _ffn(x_lat, g2, m_lat[3], m_lat[4], m_lat[5], w_ffn_in, w_ffn_out)
        if need_ctx:
            x_ctx = _outproj(y_ctx, w_out, x_ctx, m_ctx[2], name="outproj_ctx")
            x_ctx = _ffn(x_ctx, g2, m_ctx[3], m_ctx[4], m_ctx[5], w_ffn_in, w_ffn_out, name="ffn_ctx")
    return x_lat
```

```python
import functools
import math

import jax
import jax.numpy as jnp
from jax import lax
from jax.experimental import pallas as pl
from jax.experimental.pallas import tpu as pltpu

F32 = jnp.float32
BF16 = jnp.bfloat16

DEPTH = 4
GRID_W = 64
NORM_EPS = 1e-6
ROPE_THETA = 10000.0
LOG2E = math.log2(math.e)

LRU_BLOCKS = 16
LRU_BLOCK = 128
LRU_C = 8.0
CONV_WIDTH = 4
CONV_PAD_LEFT = 2

HGRN_HEADS = 16
HGRN_HEAD_DIM = 128
GLA_CHUNK = 64

MLA_HEADS = 16
MLA_Q_RANK = 512
MLA_KV_RANK = 512
MLA_NOPE = 128
MLA_ROPE = 64
MLA_V = 128
MLA_SCALE = (MLA_NOPE + MLA_ROPE) ** -0.5
MLA_QK_PAD = 256
MLA_V_ROWS = MLA_V + 16

SWA_Q_HEADS = 16
SWA_KV_HEADS = 4
SWA_GROUP = SWA_Q_HEADS // SWA_KV_HEADS
SWA_HEAD_DIM = 128
SWA_WINDOW = 128
SWA_BLOCK = 128
SWA_SCALE = SWA_HEAD_DIM ** -0.5

LANES = 128
MIB = 1024 * 1024
MASK_VALUE = -1e30


def _cparams(semantics, vmem_mib):
    return pltpu.CompilerParams(dimension_semantics=semantics, vmem_limit_bytes=vmem_mib * MIB)


def _sigmoid(x):
    return jax.nn.sigmoid(x)


def _silu(x):
    return x * _sigmoid(x)


def _rms(x, eps=NORM_EPS):
    return x * lax.rsqrt(jnp.mean(x * x, axis=-1, keepdims=True) + eps)


def _adaln(x, g, shift, scale):
    return (_rms(x) * g) * (1.0 + scale) + shift


def _dot(a, b):
    return jnp.dot(a, b, preferred_element_type=F32)


def _dot_nt(a, b):
    return lax.dot_general(a, b, (((1,), (1,)), ((), ())), preferred_element_type=F32)


def _dot_tn(a, b):
    return lax.dot_general(a, b, (((0,), (0,)), ((), ())), preferred_element_type=F32)


def _mod_kernel(c_ref, w_ref, b_ref, o_ref):
    s = _silu(c_ref[...]).astype(BF16)
    o_ref[0] = _dot(s, w_ref[0].astype(BF16)) + b_ref[0]


def _modulation(cond, mod_w, mod_b):
    depth, d, n = mod_w.shape
    tn = 1024
    return pl.pallas_call(
        _mod_kernel,
        grid=(depth, n // tn),
        in_specs=[pl.BlockSpec((8, d), lambda l, j: (0, 0)),
                  pl.BlockSpec((1, d, tn), lambda l, j: (l, 0, j)),
                  pl.BlockSpec((1, 1, tn), lambda l, j: (l, 0, j))],
        out_specs=pl.BlockSpec((1, 8, tn), lambda l, j: (l, 0, j)),
        out_shape=jax.ShapeDtypeStruct((depth, 8, n), F32),
        compiler_params=_cparams(("parallel", "parallel"), 40),
        name="modulation",
    )(cond, mod_w, mod_b.reshape(depth, 1, n))


def _inproj_kernel(x_ref, g_ref, sh_ref, sc_ref, w_ref, o_ref, h_ref, *, head_major):
    def project(h):
        res = _dot(h, w_ref[...])
        if head_major:
            for c in range(res.shape[1] // LANES):
                o_ref[0, c] = res[:, c * LANES:(c + 1) * LANES].astype(o_ref.dtype)
        else:
            o_ref[0] = res.astype(o_ref.dtype)

    @pl.when(pl.program_id(2) == 0)
    def _():
        h = _adaln(x_ref[0], g_ref[...], sh_ref[0], sc_ref[0]).astype(BF16)
        h_ref[...] = h
        project(h)

    @pl.when(pl.program_id(2) > 0)
    def _():
        project(h_ref[...])


def _inproj(x, g, shift, scale, w, *, tn, head_major=False, name="inproj"):
    b, r, d = x.shape
    n = w.shape[1]
    tm = min(r, 512)
    assert r % tm == 0 and n % tn == 0
    if head_major:
        out_shape = jax.ShapeDtypeStruct((b, n // LANES, r, LANES), BF16)
        out_spec = pl.BlockSpec((1, tn // LANES, tm, LANES), lambda bi, i, j: (bi, j, i, 0))
    else:
        out_shape = jax.ShapeDtypeStruct((b, r, n), BF16)
        out_spec = pl.BlockSpec((1, tm, tn), lambda bi, i, j: (bi, i, j))
    vec = pl.BlockSpec((1, 1, d), lambda bi, i, j: (bi, 0, 0))
    return pl.pallas_call(
        functools.partial(_inproj_kernel, head_major=head_major),
        grid=(b, r // tm, n // tn),
        in_specs=[pl.BlockSpec((1, tm, d), lambda bi, i, j: (bi, i, 0)),
                  pl.BlockSpec((1, d), lambda bi, i, j: (0, 0)),
                  vec, vec,
                  pl.BlockSpec((d, tn), lambda bi, i, j: (0, j))],
        out_specs=out_spec,
        out_shape=out_shape,
        scratch_shapes=[pltpu.VMEM((tm, d), BF16)],
        compiler_params=_cparams(("parallel", "parallel", "arbitrary"), 48),
        name=name,
    )(x, g, shift, scale, w)


def _outproj_kernel(a_ref, w_ref, x_ref, gate_ref, o_ref):
    o_ref[0] = x_ref[0] + gate_ref[0] * _dot(a_ref[0], w_ref[...])


def _outproj(a, w, x, gate, name="outproj"):
    b, r, k = a.shape
    d = w.shape[1]
    tm = min(r, 512)
    return pl.pallas_call(
        _outproj_kernel,
        grid=(b, r // tm),
        in_specs=[pl.BlockSpec((1, tm, k), lambda bi, i: (bi, i, 0)),
                  pl.BlockSpec((k, d), lambda bi, i: (0, 0)),
                  pl.BlockSpec((1, tm, d), lambda bi, i: (bi, i, 0)),
                  pl.BlockSpec((1, 1, d), lambda bi, i: (bi, 0, 0))],
        out_specs=pl.BlockSpec((1, tm, d), lambda bi, i: (bi, i, 0)),
        out_shape=jax.ShapeDtypeStruct((b, r, d), F32),
        compiler_params=_cparams(("parallel", "parallel"), 48),
        name=name,
    )(a, w, x, gate)


FFN_ROW_TILE = 1024
FFN_VMEM_MIB = 56


def _ffn_kernel(x_ref, g_ref, sh_ref, sc_ref, gate_ref, wg_ref, wu_ref, wo_ref, o_ref, f_ref):
    j = pl.program_id(2)
    last = pl.num_programs(2) - 1

    def partial_out(f):
        half = wg_ref.shape[1] // 2
        halves = (slice(0, half), slice(half, 2 * half))
        pre = [(_dot(f, wg_ref[:, cs]), _dot(f, wu_ref[:, cs])) for cs in halves]
        return [_dot((_silu(gt) * up).astype(BF16), wo_ref[cs, :]) for (gt, up), cs in zip(pre, halves)]

    @pl.when(j == 0)
    def _():
        f = _adaln(x_ref[0], g_ref[...], sh_ref[0], sc_ref[0]).astype(BF16)
        f_ref[...] = f
        pa, pb = partial_out(f)
        o_ref[0] = pa
        o_ref[0] += pb

    @pl.when((j > 0) & (j < last))
    def _():
        for part in partial_out(f_ref[...]):
            o_ref[0] += part

    @pl.when(j == last)
    def _():
        pa, pb = partial_out(f_ref[...])
        o_ref[0] += pa
        o_ref[0] = x_ref[0] + gate_ref[0] * (o_ref[0] + pb)


def _ffn(x, g, shift, scale, gate, w_in, w_out, name="ffn"):
    b, r, d = x.shape
    hidden = w_out.shape[0]
    tm = min(r, FFN_ROW_TILE)
    tf = 512
    nf = hidden // tf
    assert hidden % tf == 0 and nf >= 2
    vec = pl.BlockSpec((1, 1, d), lambda bi, i, j: (bi, 0, 0))
    return pl.pallas_call(
        _ffn_kernel,
        grid=(b, r // tm, nf),
        in_specs=[pl.BlockSpec((1, tm, d), lambda bi, i, j: (bi, i, 0)),
                  pl.BlockSpec((1, d), lambda bi, i, j: (0, 0)),
                  vec, vec, vec,
                  pl.BlockSpec((d, tf), lambda bi, i, j: (0, j)),
                  pl.BlockSpec((d, tf), lambda bi, i, j: (0, j + nf)),
                  pl.BlockSpec((tf, d), lambda bi, i, j: (j, 0))],
        out_specs=pl.BlockSpec((1, tm, d), lambda bi, i, j: (bi, i, 0)),
        out_shape=jax.ShapeDtypeStruct((b, r, d), F32),
        scratch_shapes=[pltpu.VMEM((tm, d), BF16)],
        compiler_params=_cparams(("parallel", "parallel", "arbitrary"), FFN_VMEM_MIB),
        name=name,
    )(x, g, shift, scale, gate, w_in, w_in, w_out)


def _softplus(x):
    return jnp.maximum(x, 0.0) + jnp.log1p(jnp.exp(-jnp.abs(x)))


def _gelu_tanh(x):
    return 0.5 * x * (1.0 + jnp.tanh(math.sqrt(2.0 / math.pi) * (x + 0.044715 * (x * x * x))))


def _rglru_tile(rec, prev8, next8, wg_ref, gb_ref, lam_ref, cw_ref, cb_ref,
                ubuf, a_s, b_s, h_s, hcar, *, reverse):
    tt = rec.shape[0]
    ubuf[0:8, :] = prev8
    ubuf[8:8 + tt, :] = rec
    ubuf[8 + tt:16 + tt, :] = next8
    base = 8 - CONV_PAD_LEFT
    u = cb_ref[...] + cw_ref[0:1, :] * ubuf[base:base + tt, :]
    for j in range(1, CONV_WIDTH):
        u = u + cw_ref[j:j + 1, :] * ubuf[base + j:base + j + tt, :]
    sp = _softplus(-lam_ref[...])
    for k in range(LRU_BLOCKS):
        cs = slice(k * LRU_BLOCK, (k + 1) * LRU_BLOCK)
        uk = u[:, cs]
        gts = _dot(uk.astype(BF16), wg_ref[k])
        r = _sigmoid(gts[:, :LRU_BLOCK] + gb_ref[0:1, cs])
        i = _sigmoid(gts[:, LRU_BLOCK:] + gb_ref[1:2, cs])
        log_a = (-LRU_C) * r * sp[:, cs]
        a = jnp.exp(log_a)
        a_s[:, cs] = a
        b_s[:, cs] = jnp.sqrt(-jnp.tanh(log_a) * (a * a + 1.0)) * (i * uk)

    def body(t, h):
        row = (tt - 1 - t) if reverse else t
        h = a_s[pl.ds(row, 1), :] * h + b_s[pl.ds(row, 1), :]
        h_s[pl.ds(row, 1), :] = h
        return h

    hcar[...] = lax.fori_loop(0, tt, body, hcar[...], unroll=8)


def _rglru_kernel(*refs, reverse, tt, nt):
    if reverse:
        (gl_ref, rl_ref, pv_ref, nx_ref, gc_ref, rc_ref, hfl_ref, hfc_ref,
         wg_ref, gb_ref, lam_ref, cw_ref, cb_ref, ol_ref, oc_ref, ubuf, a_s, b_s, h_s, hcar) = refs
    else:
        (rl_ref, pv_ref, nx_ref, rc_ref,
         wg_ref, gb_ref, lam_ref, cw_ref, cb_ref, ol_ref, oc_ref, ubuf, a_s, b_s, h_s, hcar) = refs
    s = pl.program_id(1)
    tile = functools.partial(_rglru_tile, wg_ref=wg_ref, gb_ref=gb_ref, lam_ref=lam_ref, cw_ref=cw_ref,
                             cb_ref=cb_ref, ubuf=ubuf, a_s=a_s, b_s=b_s, h_s=h_s, hcar=hcar, reverse=reverse)
    zeros8 = jnp.zeros((8, rl_ref.shape[2]), F32)

    @pl.when(s == 0)
    def _():
        hcar[...] = jnp.zeros_like(hcar)
        tile(rc_ref[0].astype(F32), zeros8, zeros8)
        if reverse:
            oc_ref[0] = (_gelu_tanh(gc_ref[0].astype(F32)) * (hfc_ref[0].astype(F32) + h_s[...])).astype(oc_ref.dtype)
        else:
            oc_ref[0] = h_s[...].astype(oc_ref.dtype)

    @pl.when(s > 0)
    def _():
        tl = (nt - s) if reverse else (s - 1)
        has_prev = (tl > 0).astype(F32)
        has_next = (tl < nt - 1).astype(F32)
        prev8 = pv_ref[0].astype(F32)[8:16, :] * has_prev
        next8 = nx_ref[0].astype(F32)[0:8, :] * has_next
        tile(rl_ref[0].astype(F32), prev8, next8)
        if reverse:
            ol_ref[0] = (_gelu_tanh(gl_ref[0].astype(F32)) * (hfl_ref[0].astype(F32) + h_s[...])).astype(ol_ref.dtype)
        else:
            ol_ref[0] = h_s[...].astype(ol_ref.dtype)


def _rglru_scan(gr_lat, gr_ctx, wg, gb, lam, cw, cb, hf_lat=None, hf_ctx=None):
    reverse = hf_lat is not None
    b, s_len, w2 = gr_lat.shape
    w = w2 // 2
    tt = gr_ctx.shape[1]
    assert s_len % tt == 0 and tt % 16 == 0
    nt = s_len // tt
    hb = tt // 16

    def lat_tile(si):
        return (nt - jnp.maximum(si, 1)) if reverse else jnp.maximum(si - 1, 0)

    lat_rows = lambda col: pl.BlockSpec((1, tt, w), lambda bi, si: (bi, lat_tile(si), col))
    ctx_rows = lambda col: pl.BlockSpec((1, tt, w), lambda bi, si: (bi, 0, col))
    prev_spec = pl.BlockSpec((1, 16, w), lambda bi, si: (bi, jnp.maximum(lat_tile(si) * hb - 1, 0), 1))
    next_spec = pl.BlockSpec((1, 16, w), lambda bi, si: (bi, jnp.minimum((lat_tile(si) + 1) * hb, nt * hb - 1), 1))
    full = lambda shape: pl.BlockSpec(shape, lambda bi, si: (0,) * len(shape))
    params = [wg, gb, lam, cw, cb]
    param_specs = [full(wg.shape), full(gb.shape), full(lam.shape), full(cw.shape), full(cb.shape)]
    if reverse:
        args = [gr_lat, gr_lat, gr_lat, gr_lat, gr_ctx, gr_ctx, hf_lat, hf_ctx] + params
        in_specs = [lat_rows(0), lat_rows(1), prev_spec, next_spec, ctx_rows(0), ctx_rows(1),
                    lat_rows(0), ctx_rows(0)] + param_specs
    else:
        args = [gr_lat, gr_lat, gr_lat, gr_ctx] + params
        in_specs = [lat_rows(1), prev_spec, next_spec, ctx_rows(1)] + param_specs
    return pl.pallas_call(
        functools.partial(_rglru_kernel, reverse=reverse, tt=tt, nt=nt),
        grid=(b, nt + 1),
        in_specs=in_specs,
        out_specs=[lat_rows(0), ctx_rows(0)],
        out_shape=[jax.ShapeDtypeStruct((b, s_len, w), BF16), jax.ShapeDtypeStruct((b, tt, w), BF16)],
        scratch_shapes=[pltpu.VMEM((tt + 16, w), F32), pltpu.VMEM((tt, w), F32), pltpu.VMEM((tt, w), F32),
                        pltpu.VMEM((tt, w), F32), pltpu.VMEM((1, w), F32)],
        compiler_params=_cparams(("parallel", "arbitrary"), 48),
        name="rglru_bwd" if reverse else "rglru_fwd",
    )(*args)


def _gla_mask(reverse):
    r_i = lax.broadcasted_iota(jnp.int32, (GLA_CHUNK, GLA_CHUNK), 0)
    c_i = lax.broadcasted_iota(jnp.int32, (GLA_CHUNK, GLA_CHUNK), 1)
    return (c_i >= r_i) if reverse else (c_i <= r_i)


def _hgrn_head(q, f, v, st_t, mask, reverse):
    rows = q.shape[0]
    c = GLA_CHUNK
    half = c // 2
    nchunk = rows // c
    chunks = [slice(n * c, (n + 1) * c) for n in range(nchunk)]
    tri = jnp.where(mask, 1.0, 0.0).astype(BF16)
    g = jnp.log(f)
    k = 1.0 - f
    hi = g.astype(BF16)
    r1 = g - hi.astype(F32)
    mid = r1.astype(BF16)
    lo = (r1 - mid.astype(F32)).astype(BF16)
    g3 = jnp.concatenate([hi, mid, lo], axis=1)
    c3s = [_dot(tri, g3[rs]) for rs in chunks]
    yield
    cums = [(c3[:, :LANES] + c3[:, LANES:2 * LANES]) + c3[:, 2 * LANES:] for c3 in c3s]
    if reverse:
        totals = [cm[0:1] for cm in cums]
        refs = [cm[half:half + 1] for cm in cums]
    else:
        totals = [cm[c - 1:c] for cm in cums]
        refs = [cm[half - 1:half] for cm in cums]
    cum = jnp.concatenate(cums, axis=0)
    ref_b = jnp.concatenate([jnp.broadcast_to(r, (c, LANES)) for r in refs], axis=0)
    e_q = jnp.exp(cum - ref_b)
    qt = q * e_q
    kt = k * (1.0 / e_q)
    qtb, ktb, vb = qt.astype(BF16), kt.astype(BF16), v.astype(BF16)
    raw, upds, decays = [], [], []
    for n, rs in enumerate(chunks):
        raw.append(_dot_nt(qtb[rs], ktb[rs]))
        kbar = (kt[rs] * jnp.exp(totals[n] - refs[n])).astype(BF16)
        upds.append(_dot_tn(vb[rs], kbar))
        decays.append(jnp.exp(totals[n]))
    yield
    outs = [_dot(jnp.where(mask, s, 0.0).astype(BF16), vb[rs]) for s, rs in zip(raw, chunks)]
    yield
    for n in (range(nchunk - 1, -1, -1) if reverse else range(nchunk)):
        qi = (qt[chunks[n]] * jnp.exp(refs[n])).astype(BF16)
        outs[n] = outs[n] + _dot_nt(qi, st_t.astype(BF16))
        st_t = st_t * decays[n] + upds[n]
    return jnp.concatenate(outs, axis=0), st_t


def _run_in_lockstep(generators):
    results = [None] * len(generators)
    active = list(range(len(generators)))
    while active:
        for i in list(active):
            try:
                next(generators[i])
            except StopIteration as done:
                results[i] = done.value
                active.remove(i)
    return results


HGRN_HEADS_PER_STEP = 16


def _hgrn_kernel(*refs, reverse, layer):
    if reverse:
        (ql, fl, il, gl, ofl, qc, fc, ic, gc, ofc, lg_ref, gn_ref, ol_ref, oc_ref, st_ref) = refs
    else:
        (ql, fl, il, qc, fc, ic, lg_ref, ol_ref, oc_ref, st_ref) = refs
    is_ctx = pl.program_id(2) == 0

    @pl.when(is_ctx)
    def _():
        st_ref[...] = jnp.zeros_like(st_ref)

    def pick(c_ref, l_ref, hh):
        return jnp.where(is_ctx, c_ref[0, hh], l_ref[0, hh]).astype(F32)

    mask = _gla_mask(reverse)
    heads = []
    for hh in range(HGRN_HEADS_PER_STEP):
        cs = slice(hh * HGRN_HEAD_DIM, (hh + 1) * HGRN_HEAD_DIM)
        lg = lg_ref[:, cs]
        e = jnp.exp(lg - jnp.max(lg, axis=0, keepdims=True))
        sm = e / jnp.sum(e, axis=0, keepdims=True)
        lb = jnp.zeros((1, HGRN_HEAD_DIM), F32)
        for l in range(1, layer + 1):
            lb = lb + sm[l:l + 1]
        q = _silu(pick(qc, ql, hh))
        f = lb + (1.0 - lb) * _sigmoid(pick(fc, fl, hh))
        heads.append(_hgrn_head(q, f, pick(ic, il, hh), st_ref[hh], mask, reverse))
    outs = []
    for hh, (o, st_new) in enumerate(_run_in_lockstep(heads)):
        st_ref[hh] = st_new
        if reverse:
            o = (_rms(pick(ofc, ofl, hh) + o) * gn_ref[...]) * _silu(pick(gc, gl, hh))
        outs.append(o.astype(ol_ref.dtype))

    def store(out_ref):
        for hh, ob in enumerate(outs):
            if reverse:
                out_ref[0, :, hh * HGRN_HEAD_DIM:(hh + 1) * HGRN_HEAD_DIM] = ob
            else:
                out_ref[0, hh] = ob

    @pl.when(is_ctx)
    def _():
        store(oc_ref)

    @pl.when(jnp.logical_not(is_ctx))
    def _():
        store(ol_ref)


def _hgrn_scan(p_lat, p_ctx, logits_d, layer, gn=None, of_lat=None, of_ctx=None):
    reverse = of_lat is not None
    b, _, s_len, hd = p_lat.shape
    h, hps = HGRN_HEADS, HGRN_HEADS_PER_STEP
    tt = p_ctx.shape[2]
    assert s_len % tt == 0 and tt % GLA_CHUNK == 0 and h % hps == 0
    nt = s_len // tt
    ng = h // hps
    fsel = 2 if reverse else 1

    def lat_tile(si):
        return (nt - jnp.maximum(si, 1)) if reverse else jnp.maximum(si - 1, 0)

    lat = lambda grp: pl.BlockSpec((1, hps, tt, hd), lambda bi, hi, si: (bi, grp * ng + hi, lat_tile(si), 0))
    ctx = lambda grp: pl.BlockSpec((1, hps, tt, hd), lambda bi, hi, si: (bi, grp * ng + hi, 0, 0))
    lg_spec = pl.BlockSpec((logits_d.shape[0], hps * hd), lambda bi, hi, si: (0, hi))
    if reverse:
        args = [p_lat, p_lat, p_lat, p_lat, of_lat, p_ctx, p_ctx, p_ctx, p_ctx, of_ctx, logits_d, gn]
        in_specs = [lat(0), lat(fsel), lat(3), lat(4), lat(0), ctx(0), ctx(fsel), ctx(3), ctx(4), ctx(0),
                    lg_spec, pl.BlockSpec((1, hd), lambda bi, hi, si: (0, 0))]
        out_specs = [pl.BlockSpec((1, tt, hps * hd), lambda bi, hi, si: (bi, lat_tile(si), hi)),
                     pl.BlockSpec((1, tt, hps * hd), lambda bi, hi, si: (bi, 0, hi))]
        out_shape = [jax.ShapeDtypeStruct((b, s_len, h * hd), BF16), jax.ShapeDtypeStruct((b, tt, h * hd), BF16)]
    else:
        args = [p_lat, p_lat, p_lat, p_ctx, p_ctx, p_ctx, logits_d]
        in_specs = [lat(0), lat(fsel), lat(3), ctx(0), ctx(fsel), ctx(3), lg_spec]
        out_specs = [lat(0), ctx(0)]
        out_shape = [jax.ShapeDtypeStruct((b, h, s_len, hd), BF16), jax.ShapeDtypeStruct((b, h, tt, hd), BF16)]
    return pl.pallas_call(
        functools.partial(_hgrn_kernel, reverse=reverse, layer=layer),
        grid=(b, ng, nt + 1),
        in_specs=in_specs,
        out_specs=out_specs,
        out_shape=out_shape,
        scratch_shapes=[pltpu.VMEM((hps, hd, hd), F32)],
        compiler_params=_cparams(("parallel", "parallel", "arbitrary"), 32),
        name="hgrn_bwd" if reverse else "hgrn_fwd",
    )(*args)


def _rope_tables(n_lat, rot_dim, reps):
    n_freq = rot_dim // 4
    t = jnp.arange(n_lat)
    inv = ROPE_THETA ** (-jnp.arange(n_freq, dtype=F32) / n_freq)
    ang_r = (t // GRID_W).astype(F32)[:, None] * inv[None, :]
    ang_c = (t % GRID_W).astype(F32)[:, None] * inv[None, :]
    cos = jnp.concatenate([jnp.cos(ang_r)] * 2 + [jnp.cos(ang_c)] * 2, axis=-1)
    sin = jnp.concatenate([-jnp.sin(ang_r), jnp.sin(ang_r), -jnp.sin(ang_c), jnp.sin(ang_c)], axis=-1)
    return jnp.concatenate([jnp.tile(cos, (1, reps)), jnp.tile(sin, (1, reps))], axis=-1)


def _rope(x, cos, sin, n_freq):
    width = x.shape[1]
    lane = lax.broadcasted_iota(jnp.int32, (1, width), 1)
    first = (lane % (2 * n_freq)) < n_freq
    partner = jnp.where(first, pltpu.roll(x, width - n_freq, 1), pltpu.roll(x, n_freq, 1))
    return x * cos + partner * sin


def _mla_qkv_kernel(*refs, use_rope):
    if use_rope:
        (c_ref, qg_ref, kvg_ref, wq_ref, wk_ref, wvt_ref, gqn_ref, gqr_ref, gkn_ref, gkr_ref, tab_ref,
         q_ref, k_ref, vt_ref) = refs
    else:
        (c_ref, qg_ref, kvg_ref, wq_ref, wk_ref, wvt_ref, gqn_ref, gqr_ref, gkn_ref, gkr_ref,
         q_ref, k_ref, vt_ref) = refs
    c = c_ref[0].astype(F32)
    cqn = (_rms(c[:, :MLA_Q_RANK]) * qg_ref[...]).astype(BF16)
    ckvn = (_rms(c[:, MLA_Q_RANK:MLA_Q_RANK + MLA_KV_RANK]) * kvg_ref[...]).astype(BF16)
    kr = c[:, MLA_Q_RANK + MLA_KV_RANK:]
    qa = _dot(cqn, wq_ref[...])
    ka = _dot(ckvn, wk_ref[...])
    vt = _dot_nt(wvt_ref[...], ckvn)
    lane = lax.broadcasted_iota(jnp.int32, (1, LANES), 1)
    low = lane < MLA_ROPE
    if use_rope:
        cos, sin = tab_ref[:, :LANES], tab_ref[:, LANES:]
    inv_rope = 1.0 / MLA_ROPE

    def halves_rms(x):
        sq = x * x
        ss_lo = jnp.sum(jnp.where(low, sq, 0.0), axis=-1, keepdims=True)
        ss_hi = jnp.sum(jnp.where(low, 0.0, sq), axis=-1, keepdims=True)
        return x * jnp.where(low, lax.rsqrt(ss_lo * inv_rope + NORM_EPS), lax.rsqrt(ss_hi * inv_rope + NORM_EPS))

    krn = halves_rms(kr) * gkr_ref[...]
    if use_rope:
        krn = _rope(krn, cos, sin, MLA_ROPE // 4)
    krn = jnp.where(low, krn, 0.0)
    nope_w = MLA_HEADS * MLA_NOPE
    qscale = MLA_SCALE * LOG2E
    for p in range(MLA_HEADS // 2):
        qr = halves_rms(qa[:, nope_w + p * LANES:nope_w + (p + 1) * LANES]) * gqr_ref[...]
        if use_rope:
            qr = _rope(qr, cos, sin, MLA_ROPE // 4)
        for e in range(2):
            h = 2 * p + e
            hs = slice(h * MLA_NOPE, (h + 1) * MLA_NOPE)
            qn = _rms(qa[:, hs]) * gqn_ref[...]
            rot = qr if e == 0 else pltpu.roll(qr, MLA_ROPE, 1)
            rot = jnp.where(low, rot, 0.0)
            q_ref[0, h] = (jnp.concatenate([qn, rot], axis=1) * qscale).astype(BF16)
            kn = _rms(ka[:, hs]) * gkn_ref[...]
            k_ref[0, h] = jnp.concatenate([kn, krn], axis=1).astype(BF16)
            vt_ref[0, h, 0, :MLA_V, :] = vt[hs, :].astype(BF16)
            vt_ref[0, h, 0, MLA_V:, :] = jnp.ones((MLA_V_ROWS - MLA_V, vt.shape[1]), BF16)


def _mla_qkv(c, qg, kvg, wq, wk, wvt, gqn, gqr, gkn, gkr, tab, tm):
    b, r, cw = c.shape
    assert r % tm == 0
    h = MLA_HEADS
    use_rope = tab is not None
    full = lambda a: pl.BlockSpec(a.shape, lambda bi, i: (0,) * a.ndim)
    args = [c, qg, kvg, wq, wk, wvt, gqn, gqr, gkn, gkr]
    in_specs = [pl.BlockSpec((1, tm, cw), lambda bi, i: (bi, i, 0))] + [full(a) for a in args[1:]]
    if use_rope:
        args.append(tab)
        in_specs.append(pl.BlockSpec((tm, tab.shape[1]), lambda bi, i: (i, 0)))
    return pl.pallas_call(
        functools.partial(_mla_qkv_kernel, use_rope=use_rope),
        grid=(b, r // tm),
        in_specs=in_specs,
        out_specs=[pl.BlockSpec((1, h, tm, MLA_QK_PAD), lambda bi, i: (bi, 0, i, 0)),
                   pl.BlockSpec((1, h, tm, MLA_QK_PAD), lambda bi, i: (bi, 0, i, 0)),
                   pl.BlockSpec((1, h, 1, MLA_V_ROWS, tm), lambda bi, i: (bi, 0, i, 0, 0))],
        out_shape=[jax.ShapeDtypeStruct((b, h, r, MLA_QK_PAD), BF16),
                   jax.ShapeDtypeStruct((b, h, r, MLA_QK_PAD), BF16),
                   jax.ShapeDtypeStruct((b, h, r // tm, MLA_V_ROWS, tm), BF16)],
        compiler_params=_cparams(("parallel", "parallel"), 48),
        name="mla_qkv_rope" if use_rope else "mla_qkv",
    )(*args)


MLA_KV_TILES_PER_TRIP = 8
MLA_Q_CHUNK = 512
MLA_Q_CHUNKS_PER_STEP = 2


def _attn_update(s, vt, m, acc):
    m_new = jnp.maximum(m, jnp.max(s, axis=0, keepdims=True))
    alpha = jnp.exp2(m - m_new)
    p = jnp.exp2(s - m_new).astype(BF16)
    return m_new, alpha * acc + _dot(vt, p)


MLA_STALE_MARGIN = 100.0
MLA_STALE_BOUND = (126.0 + MLA_STALE_MARGIN) / 2.0


def _attn_update_stale(s, vt, m, acc):
    p = jnp.exp2(s - m).astype(BF16)
    m_new = jnp.maximum(m, jnp.max(s, axis=0, keepdims=True))
    return m_new, (acc + _dot(vt, p)) * jnp.exp2(m - m_new)


def _mla_attn_stale(qs, kc_ref, vc_ref, kl_ref, vl_ref, floor):
    nk, tk = vl_ref.shape[2], vl_ref.shape[4]
    unroll = min(MLA_KV_TILES_PER_TRIP, nk)
    ms, accs = [], []
    for q in qs:
        m = jnp.full((1, q.shape[0]), floor, F32)
        acc = jnp.zeros((vc_ref.shape[3], q.shape[0]), F32)
        m, acc = _attn_update(_dot_nt(kc_ref[0, 0], q), vc_ref[0, 0, 0], m, acc)
        ms.append(m)
        accs.append(acc)

    def body(i, carry):
        ms, accs = list(carry[0]), list(carry[1])
        for u in range(unroll):
            t = unroll * i + u
            k = kl_ref[0, 0, pl.ds(pl.multiple_of(t * tk, tk), tk), :]
            vt = vl_ref[0, 0, t]
            ss = [_dot_nt(k, q) for q in qs]
            for c in range(len(qs)):
                ms[c], accs[c] = _attn_update_stale(ss[c], vt, ms[c], accs[c])
        return tuple(ms), tuple(accs)

    return lax.fori_loop(0, nk // unroll, body, (tuple(ms), tuple(accs)))[1]


def _mla_attn_online(q, kc_ref, vc_ref, kl_ref, vl_ref):
    nk, tk = vl_ref.shape[2], vl_ref.shape[4]
    unroll = min(MLA_KV_TILES_PER_TRIP, nk)
    m = jnp.full((1, q.shape[0]), -jnp.inf, F32)
    acc = jnp.zeros((vc_ref.shape[3], q.shape[0]), F32)

    def scores(j):
        return _dot_nt(kl_ref[0, 0, pl.ds(pl.multiple_of(j * tk, tk), tk), :], q)

    s_cur = scores(0)
    m, acc = _attn_update(_dot_nt(kc_ref[0, 0], q), vc_ref[0, 0, 0], m, acc)

    def body(i, carry):
        s_cur, m, acc = carry
        for u in range(unroll):
            t = unroll * i + u
            s_next = scores(t + 1)
            m, acc = _attn_update(s_cur, vl_ref[0, 0, t], m, acc)
            s_cur = s_next
        return s_cur, m, acc

    s_cur, m, acc = lax.fori_loop(0, nk // unroll - 1, body, (s_cur, m, acc))
    for t in range(nk - unroll, nk):
        s_next = scores(t + 1) if t + 1 < nk else None
        m, acc = _attn_update(s_cur, vl_ref[0, 0, t], m, acc)
        s_cur = s_next
    return acc


def _mla_attn_kernel(*refs, with_lat, chunk):
    if with_lat:
        bound_ref, q_ref, kc_ref, vc_ref, kl_ref, vl_ref, o_ref = refs
    else:
        q_ref, kc_ref, vc_ref, o_ref = refs
    tq = q_ref.shape[2]
    rows = [slice(c * chunk, (c + 1) * chunk) for c in range(tq // chunk)]

    def finish(acc, rs):
        o_ref[0, rs, :] = (acc[:MLA_V] / acc[MLA_V:MLA_V + 1]).T.astype(o_ref.dtype)

    if with_lat:
        bound = bound_ref[0]

        @pl.when(bound < MLA_STALE_BOUND)
        def _():
            accs = _mla_attn_stale([q_ref[0, 0, rs, :] for rs in rows], kc_ref, vc_ref, kl_ref, vl_ref,
                                   bound - MLA_STALE_MARGIN)
            for acc, rs in zip(accs, rows):
                finish(acc, rs)

        @pl.when(bound >= MLA_STALE_BOUND)
        def _():
            for rs in rows:
                finish(_mla_attn_online(q_ref[0, 0, rs, :], kc_ref, vc_ref, kl_ref, vl_ref), rs)
    else:
        for rs in rows:
            q = q_ref[0, 0, rs, :]
            m = jnp.full((1, chunk), -jnp.inf, F32)
            acc = jnp.zeros((vc_ref.shape[3], chunk), F32)
            finish(_attn_update(_dot_nt(kc_ref[0, 0], q), vc_ref[0, 0, 0], m, acc)[1], rs)


def _mla_attention(q, k_ctx, vt_ctx, k_lat=None, vt_lat=None, bound=None):
    b, h, n, dq = q.shape
    with_lat = k_lat is not None
    chunk = min(n, MLA_Q_CHUNK)
    tq = min(n, MLA_Q_CHUNKS_PER_STEP * chunk)
    assert n % tq == 0
    rc = k_ctx.shape[2]
    args = [q, k_ctx, vt_ctx]
    in_specs = [pl.BlockSpec((1, 1, tq, dq), lambda bi, hi, i: (bi, hi, i, 0)),
                pl.BlockSpec((1, 1, rc, dq), lambda bi, hi, i: (bi, hi, 0, 0)),
                pl.BlockSpec((1, 1, 1, MLA_V_ROWS, rc), lambda bi, hi, i: (bi, hi, 0, 0, 0))]
    if with_lat:
        nk, tk = vt_lat.shape[2], vt_lat.shape[4]
        assert nk % min(MLA_KV_TILES_PER_TRIP, nk) == 0
        args = [bound] + args + [k_lat, vt_lat]
        in_specs = [pl.BlockSpec(memory_space=pltpu.SMEM)] + in_specs
        in_specs += [pl.BlockSpec((1, 1, nk * tk, dq), lambda bi, hi, i: (bi, hi, 0, 0)),
                     pl.BlockSpec((1, 1, nk, MLA_V_ROWS, tk), lambda bi, hi, i: (bi, hi, 0, 0, 0))]
    return pl.pallas_call(
        functools.partial(_mla_attn_kernel, with_lat=with_lat, chunk=chunk),
        grid=(b, h, n // tq),
        in_specs=in_specs,
        out_specs=pl.BlockSpec((1, tq, MLA_V), lambda bi, hi, i: (bi, i, hi)),
        out_shape=jax.ShapeDtypeStruct((b, n, h * MLA_V), BF16),
        compiler_params=_cparams(("parallel", "parallel", "arbitrary"), 48),
        name="mla_attn" if with_lat else "mla_attn_ctx",
    )(*args)


def _swa_inproj_kernel(*refs, use_rope, n_norm_tiles):
    if use_rope:
        x_ref, g_ref, sh_ref, sc_ref, w_ref, hg_ref, tab_ref, o_ref, h_ref = refs
    else:
        x_ref, g_ref, sh_ref, sc_ref, w_ref, hg_ref, o_ref, h_ref = refs
    j = pl.program_id(2)

    @pl.when(j == 0)
    def _():
        h_ref[...] = _adaln(x_ref[0], g_ref[...], sh_ref[0], sc_ref[0]).astype(BF16)

    res = _dot(h_ref[...], w_ref[...])

    @pl.when(j < n_norm_tiles)
    def _():
        for e in range(res.shape[1] // SWA_HEAD_DIM):
            cs = slice(e * SWA_HEAD_DIM, (e + 1) * SWA_HEAD_DIM)
            xh = _rms(res[:, cs]) * hg_ref[:, cs]
            if use_rope:
                xh = _rope(xh, tab_ref[:, :SWA_HEAD_DIM], tab_ref[:, SWA_HEAD_DIM:], SWA_HEAD_DIM // 4)
            o_ref[0, :, cs] = xh.astype(o_ref.dtype)

    @pl.when(j >= n_norm_tiles)
    def _():
        o_ref[0] = res.astype(o_ref.dtype)


def _swa_inproj(x, g, shift, scale, w, head_gain, tab):
    b, r, d = x.shape
    n = w.shape[1]
    tm = min(r, 512)
    tn = 512
    use_rope = tab is not None
    n_norm_tiles = (SWA_Q_HEADS + SWA_KV_HEADS) * SWA_HEAD_DIM // tn
    vec = pl.BlockSpec((1, 1, d), lambda bi, i, j: (bi, 0, 0))
    args = [x, g, shift, scale, w, head_gain]
    in_specs = [pl.BlockSpec((1, tm, d), lambda bi, i, j: (bi, i, 0)),
                pl.BlockSpec((1, d), lambda bi, i, j: (0, 0)),
                vec, vec,
                pl.BlockSpec((d, tn), lambda bi, i, j: (0, j)),
                pl.BlockSpec((1, tn), lambda bi, i, j: (0, j))]
    if use_rope:
        args.append(tab)
        in_specs.append(pl.BlockSpec((tm, tab.shape[1]), lambda bi, i, j: (i, 0)))
    return pl.pallas_call(
        functools.partial(_swa_inproj_kernel, use_rope=use_rope, n_norm_tiles=n_norm_tiles),
        grid=(b, r // tm, n // tn),
        in_specs=in_specs,
        out_specs=pl.BlockSpec((1, tm, tn), lambda bi, i, j: (bi, i, j)),
        out_shape=jax.ShapeDtypeStruct((b, r, n), BF16),
        scratch_shapes=[pltpu.VMEM((tm, d), BF16)],
        compiler_params=_cparams(("parallel", "parallel", "arbitrary"), 48),
        name="swa_inproj_rope" if use_rope else "swa_inproj",
    )(*args)


def _swa_block(q, keys, vals, sinks, nctx, prev_ok, next_ok):
    blk, grp = SWA_BLOCK, SWA_GROUP
    qs = jnp.concatenate([q[:, e * SWA_HEAD_DIM:(e + 1) * SWA_HEAD_DIM] for e in range(grp)], axis=0)
    s = _dot_nt(qs, jnp.concatenate(keys, axis=0))
    yield
    rows = s.shape[0]
    r_i = lax.broadcasted_iota(jnp.int32, (rows, blk), 0) % blk
    c_i = lax.broadcasted_iota(jnp.int32, (rows, blk), 1)
    off_prev = jnp.where(prev_ok, 0, 2 * blk)
    off_next = jnp.where(next_ok, 0, 2 * blk)
    s_prev = jnp.where(c_i >= r_i + off_prev, s[:, nctx:nctx + blk], MASK_VALUE)
    s_next = jnp.where(c_i + off_next <= r_i, s[:, nctx + 2 * blk:], MASK_VALUE)
    s = jnp.concatenate([s[:, :nctx], s_prev, s[:, nctx + blk:nctx + 2 * blk], s_next], axis=1)
    row1 = lax.broadcasted_iota(jnp.int32, (rows, 1), 0)
    sink = jnp.zeros((rows, 1), F32)
    for e in range(grp):
        sink = jnp.where(row1 // blk == e, sinks[e], sink)
    m = jnp.maximum(jnp.max(s, axis=1, keepdims=True), sink)
    p = jnp.exp2(s - m)
    l = jnp.sum(p, axis=1, keepdims=True) + jnp.exp2(sink - m)
    o = _dot(p.astype(BF16), jnp.concatenate(vals, axis=0)) / l
    return jnp.concatenate([o[e * blk:(e + 1) * blk] for e in range(grp)], axis=1)


def _swa_attn_kernel(sink_ref, q_ref, kp_ref, kc_ref, kn_ref, vp_ref, vc_ref, vn_ref, kx_ref, vx_ref, o_ref):
    hk = pl.program_id(1)
    i = pl.program_id(2)
    last = pl.num_programs(2) - 1
    blk = SWA_BLOCK
    nctx = kx_ref.shape[1]
    sinks = [sink_ref[hk * SWA_GROUP + e] * LOG2E for e in range(SWA_GROUP)]
    k0, k1 = kc_ref[0, :blk, :], kc_ref[0, blk:, :]
    v0, v1 = vc_ref[0, :blk, :], vc_ref[0, blk:, :]
    o_a, o_b = _run_in_lockstep([
        _swa_block(q_ref[0, :blk, :], [kx_ref[0], kp_ref[0], k0, k1], [vx_ref[0], vp_ref[0], v0, v1],
                   sinks, nctx, i > 0, True),
        _swa_block(q_ref[0, blk:, :], [kx_ref[0], k0, k1, kn_ref[0]], [vx_ref[0], v0, v1, vn_ref[0]],
                   sinks, nctx, True, i < last)])
    o_ref[0, :blk, :] = o_a.astype(o_ref.dtype)
    o_ref[0, blk:, :] = o_b.astype(o_ref.dtype)


def _swa_attention(qkv_lat, qkv_ctx, sink):
    b, n, _ = qkv_lat.shape
    nctx = qkv_ctx.shape[1]
    blk, dh, grp = SWA_BLOCK, SWA_HEAD_DIM, SWA_GROUP
    nb = n // blk
    assert nb % 2 == 0
    kcol = SWA_Q_HEADS
    vcol = SWA_Q_HEADS + SWA_KV_HEADS
    prev = lambda col0: pl.BlockSpec((1, blk, dh), lambda bi, hk, i: (bi, jnp.maximum(2 * i - 1, 0), col0 + hk))
    nxt = lambda col0: pl.BlockSpec((1, blk, dh), lambda bi, hk, i: (bi, jnp.minimum(2 * i + 2, nb - 1), col0 + hk))
    pair = lambda col0: pl.BlockSpec((1, 2 * blk, dh), lambda bi, hk, i: (bi, i, col0 + hk))
    ctx = lambda col0: pl.BlockSpec((1, nctx, dh), lambda bi, hk, i: (bi, 0, col0 + hk))
    return pl.pallas_call(
        _swa_attn_kernel,
        grid=(b, SWA_KV_HEADS, nb // 2),
        in_specs=[pl.BlockSpec(memory_space=pltpu.SMEM),
                  pl.BlockSpec((1, 2 * blk, grp * dh), lambda bi, hk, i: (bi, i, hk)),
                  prev(kcol), pair(kcol), nxt(kcol),
                  prev(vcol), pair(vcol), nxt(vcol),
                  ctx(kcol), ctx(vcol)],
        out_specs=pl.BlockSpec((1, 2 * blk, grp * dh), lambda bi, hk, i: (bi, i, hk)),
        out_shape=jax.ShapeDtypeStruct((b, n, SWA_Q_HEADS * dh), BF16),
        compiler_params=_cparams(("parallel", "parallel", "arbitrary"), 32),
        name="swa_attn",
    )(sink, qkv_lat, qkv_lat, qkv_lat, qkv_lat, qkv_lat, qkv_lat, qkv_lat, qkv_ctx, qkv_ctx)


def kernel(x, c, ctx, c_ctx, mod_w, mod_b, norm_g, ffn_w_in, ffn_w_out, rglru_w_in, rglru_conv_w, rglru_conv_b, rglru_gate_w, rglru_gate_b, rglru_lambda, rglru_w_out, hgrn_w_in, hgrn_lb_logits, hgrn_gnorm_g, hgrn_w_out, mla_w_in, mla_q_norm_g, mla_kv_norm_g, mla_w_uq, mla_w_ukv, mla_qk_g, mla_w_out, swa_w_in, swa_qk_g, swa_sink, swa_w_out):
    b, n_lat, d = x.shape
    depth = mod_w.shape[0]
    assert depth == DEPTH and b + 1 <= 8
    bf = lambda a: a.astype(BF16)

    cond = jnp.concatenate([c, c_ctx[None, :], jnp.zeros((8 - b - 1, d), F32)], axis=0)
    mod = _modulation(cond, mod_w, mod_b).reshape(depth, 8, 6, d)

    x_lat, x_ctx = x, ctx
    for layer in range(depth):
        kind = layer % 4
        need_ctx = layer < depth - 1
        m_lat = [mod[layer, :b, k][:, None, :] for k in range(6)]
        m_ctx = [jnp.broadcast_to(mod[layer, b:b + 1, k][:, None, :], (b, 1, d)) for k in range(6)]
        g1 = norm_g[layer, 0][None, :]
        g2 = norm_g[layer, 1][None, :]
        y_ctx = None
        if kind == 0:
            w_in = bf(rglru_w_in[0])
            gr_lat = _inproj(x_lat, g1, m_lat[0], m_lat[1], w_in, tn=2048,name="rglru_inproj")
            gr_ctx = _inproj(x_ctx, g1, m_ctx[0], m_ctx[1], w_in, tn=2048,name="rglru_inproj_ctx")
            gw = rglru_gate_w[0]
            wg = [bf(jnp.concatenate([gw[dr, 0], gw[dr, 1]], axis=-1)) for dr in range(2)]
            cw, cb = rglru_conv_w[0], rglru_conv_b[0][None, :]
            scan = lambda dr, **kw: _rglru_scan(gr_lat, gr_ctx, wg[dr], rglru_gate_b[0, dr],
                                                rglru_lambda[0, dr][None, :], cw, cb, **kw)
            hf_lat, hf_ctx = scan(0)
            y_lat, y_ctx = scan(1, hf_lat=hf_lat, hf_ctx=hf_ctx)
            w_out = bf(rglru_w_out[0])
        elif kind == 1:
            w_in = bf(hgrn_w_in[0])
            p_lat = _inproj(x_lat, g1, m_lat[0], m_lat[1], w_in, tn=2048,head_major=True, name="hgrn_inproj")
            p_ctx = _inproj(x_ctx, g1, m_ctx[0], m_ctx[1], w_in, tn=2048,head_major=True, name="hgrn_inproj_ctx")
            of_lat, of_ctx = _hgrn_scan(p_lat, p_ctx, hgrn_lb_logits[:, 0, :], layer)
            y_lat, y_ctx = _hgrn_scan(p_lat, p_ctx, hgrn_lb_logits[:, 1, :], layer, gn=hgrn_gnorm_g[0][None, :],
                                      of_lat=of_lat, of_ctx=of_ctx)
            w_out = bf(hgrn_w_out[0])
        elif kind == 2:
            cw_real = mla_w_in.shape[2]
            cw_pad = -(-cw_real // LANES) * LANES
            w_in = bf(jnp.pad(mla_w_in[0], ((0, 0), (0, cw_pad - cw_real))))
            c_lat = _inproj(x_lat, g1, m_lat[0], m_lat[1], w_in, tn=cw_pad, name="mla_inproj")
            c_ctx_ = _inproj(x_ctx, g1, m_ctx[0], m_ctx[1], w_in, tn=cw_pad, name="mla_inproj_ctx")
            wq3 = mla_w_uq[0].reshape(MLA_Q_RANK, MLA_HEADS, MLA_NOPE + MLA_ROPE)
            wq = bf(jnp.concatenate([wq3[:, :, :MLA_NOPE].reshape(MLA_Q_RANK, -1),
                                     wq3[:, :, MLA_NOPE:].reshape(MLA_Q_RANK, -1)], axis=1))
            wkv3 = mla_w_ukv[0].reshape(MLA_KV_RANK, MLA_HEADS, MLA_NOPE + MLA_V)
            wk = bf(wkv3[:, :, :MLA_NOPE].reshape(MLA_KV_RANK, -1))
            wvt = bf(wkv3[:, :, MLA_NOPE:].reshape(MLA_KV_RANK, -1).T)
            qk_g = mla_qk_g[0]
            gqn, gkn = qk_g[0:1, :MLA_NOPE], qk_g[1:2, :MLA_NOPE]
            gqr = jnp.tile(qk_g[0:1, MLA_NOPE:], (1, LANES // MLA_ROPE))
            gkr = jnp.pad(qk_g[1:2, MLA_NOPE:], ((0, 0), (0, LANES - MLA_ROPE)))
            tab = _rope_tables(n_lat, MLA_ROPE, LANES // MLA_ROPE)
            small = (mla_q_norm_g[0][None, :], mla_kv_norm_g[0][None, :], wq, wk, wvt, gqn, gqr, gkn, gkr)
            q_l, k_l, vt_l = _mla_qkv(c_lat, *small, tab, min(n_lat, 512))
            q_c, k_c, vt_c = _mla_qkv(c_ctx_, *small, None, c_ctx_.shape[1])
            gmax2 = lambda g: jnp.max(g * g)
            qk_norm2 = ((MLA_NOPE * gmax2(gqn) + MLA_ROPE * gmax2(gqr))
                        * (MLA_NOPE * gmax2(gkn) + MLA_ROPE * gmax2(gkr)))
            bound = (1.05 * MLA_SCALE * LOG2E) * jnp.sqrt(qk_norm2).reshape(1)
            y_lat = _mla_attention(q_l, k_c, vt_c, k_l, vt_l, bound)
            if need_ctx:
                y_ctx = _mla_attention(q_c, k_c, vt_c)
            w_out = bf(mla_w_out[0])
        else:
            w_in = bf(swa_w_in[0])
            gq = jnp.tile(swa_qk_g[0, 0] * (SWA_SCALE * LOG2E), SWA_Q_HEADS)
            gk = jnp.tile(swa_qk_g[0, 1], SWA_KV_HEADS)
            head_gain = jnp.concatenate([gq, gk, jnp.ones((SWA_KV_HEADS * SWA_HEAD_DIM,), F32)])[None, :]
            tab = _rope_tables(n_lat, SWA_HEAD_DIM, 1)
            qkv_lat = _swa_inproj(x_lat, g1, m_lat[0], m_lat[1], w_in, head_gain, tab)
            qkv_ctx = _swa_inproj(x_ctx, g1, m_ctx[0], m_ctx[1], w_in, head_gain, None)
            y_lat = _swa_attention(qkv_lat, qkv_ctx, swa_sink[0])
            assert not need_ctx
            w_out = bf(swa_w_out[0])

        w_ffn_in, w_ffn_out = bf(ffn_w_in[layer]), bf(ffn_w_out[layer])
        x_lat = _outproj(y_lat, w_out, x_lat, m_lat[2])
        x_lat = _ffn(x_lat, g2, m_lat[3], m_lat[4], m_lat[5], w_ffn_in, w_ffn_out)
        if need_ctx:
            x_ctx = _outproj(y_ctx, w_out, x_ctx, m_ctx[2], name="outproj_ctx")
            x_ctx = _ffn(x_ctx, g2, m_ctx[3], m_ctx[4], m_ctx[5], w_ffn_in, w_ffn_out, name="ffn_ctx")
    return x_lat
```

```python
import functools
import math

import jax
import jax.numpy as jnp
from jax import lax
from jax.experimental import pallas as pl
from jax.experimental.pallas import tpu as pltpu

F32 = jnp.float32
BF16 = jnp.bfloat16

DEPTH = 4
GRID_W = 64
NORM_EPS = 1e-6
ROPE_THETA = 10000.0
LOG2E = math.log2(math.e)

LRU_BLOCKS = 16
LRU_BLOCK = 128
LRU_C = 8.0
CONV_WIDTH = 4
CONV_PAD_LEFT = 2

HGRN_HEADS = 16
HGRN_HEAD_DIM = 128
GLA_CHUNK = 64

MLA_HEADS = 16
MLA_Q_RANK = 512
MLA_KV_RANK = 512
MLA_NOPE = 128
MLA_ROPE = 64
MLA_V = 128
MLA_SCALE = (MLA_NOPE + MLA_ROPE) ** -0.5
MLA_QK_PAD = 256
MLA_V_ROWS = MLA_V + 16

SWA_Q_HEADS = 16
SWA_KV_HEADS = 4
SWA_GROUP = SWA_Q_HEADS // SWA_KV_HEADS
SWA_HEAD_DIM = 128
SWA_WINDOW = 128
SWA_BLOCK = 128
SWA_SCALE = SWA_HEAD_DIM ** -0.5

LANES = 128
MIB = 1024 * 1024
MASK_VALUE = -1e30


def _cparams(semantics, vmem_mib):
    return pltpu.CompilerParams(dimension_semantics=semantics, vmem_limit_bytes=vmem_mib * MIB)


def _sigmoid(x):
    return jax.nn.sigmoid(x)


def _silu(x):
    return x * _sigmoid(x)


def _rms(x, eps=NORM_EPS):
    return x * lax.rsqrt(jnp.mean(x * x, axis=-1, keepdims=True) + eps)


def _adaln(x, g, shift, scale):
    return (_rms(x) * g) * (1.0 + scale) + shift


def _dot(a, b):
    return jnp.dot(a, b, preferred_element_type=F32)


def _dot_nt(a, b):
    return lax.dot_general(a, b, (((1,), (1,)), ((), ())), preferred_element_type=F32)


def _dot_tn(a, b):
    return lax.dot_general(a, b, (((0,), (0,)), ((), ())), preferred_element_type=F32)


def _mod_kernel(c_ref, w_ref, b_ref, o_ref):
    s = _silu(c_ref[...]).astype(BF16)
    o_ref[0] = _dot(s, w_ref[0].astype(BF16)) + b_ref[0]


def _modulation(cond, mod_w, mod_b):
    depth, d, n = mod_w.shape
    tn = 1024
    return pl.pallas_call(
        _mod_kernel,
        grid=(depth, n // tn),
        in_specs=[pl.BlockSpec((8, d), lambda l, j: (0, 0)),
                  pl.BlockSpec((1, d, tn), lambda l, j: (l, 0, j)),
                  pl.BlockSpec((1, 1, tn), lambda l, j: (l, 0, j))],
        out_specs=pl.BlockSpec((1, 8, tn), lambda l, j: (l, 0, j)),
        out_shape=jax.ShapeDtypeStruct((depth, 8, n), F32),
        compiler_params=_cparams(("parallel", "parallel"), 40),
        name="modulation",
    )(cond, mod_w, mod_b.reshape(depth, 1, n))


def _inproj_kernel(x_ref, g_ref, sh_ref, sc_ref, w_ref, o_ref, h_ref, *, head_major):
    def project(h):
        res = _dot(h, w_ref[...])
        if head_major:
            for c in range(res.shape[1] // LANES):
                o_ref[0, c] = res[:, c * LANES:(c + 1) * LANES].astype(o_ref.dtype)
        else:
            o_ref[0] = res.astype(o_ref.dtype)

    @pl.when(pl.program_id(2) == 0)
    def _():
        h = _adaln(x_ref[0], g_ref[...], sh_ref[0], sc_ref[0]).astype(BF16)
        h_ref[...] = h
        project(h)

    @pl.when(pl.program_id(2) > 0)
    def _():
        project(h_ref[...])


def _inproj(x, g, shift, scale, w, *, tn, head_major=False, name="inproj"):
    b, r, d = x.shape
    n = w.shape[1]
    tm = min(r, 512)
    assert r % tm == 0 and n % tn == 0
    if head_major:
        out_shape = jax.ShapeDtypeStruct((b, n // LANES, r, LANES), BF16)
        out_spec = pl.BlockSpec((1, tn // LANES, tm, LANES), lambda bi, i, j: (bi, j, i, 0))
    else:
        out_shape = jax.ShapeDtypeStruct((b, r, n), BF16)
        out_spec = pl.BlockSpec((1, tm, tn), lambda bi, i, j: (bi, i, j))
    vec = pl.BlockSpec((1, 1, d), lambda bi, i, j: (bi, 0, 0))
    return pl.pallas_call(
        functools.partial(_inproj_kernel, head_major=head_major),
        grid=(b, r // tm, n // tn),
        in_specs=[pl.BlockSpec((1, tm, d), lambda bi, i, j: (bi, i, 0)),
                  pl.BlockSpec((1, d), lambda bi, i, j: (0, 0)),
                  vec, vec,
                  pl.BlockSpec((d, tn), lambda bi, i, j: (0, j))],
        out_specs=out_spec,
        out_shape=out_shape,
        scratch_shapes=[pltpu.VMEM((tm, d), BF16)],
        compiler_params=_cparams(("parallel", "parallel", "arbitrary"), 48),
        name=name,
    )(x, g, shift, scale, w)


def _outproj_kernel(a_ref, w_ref, x_ref, gate_ref, o_ref):
    o_ref[0] = x_ref[0] + gate_ref[0] * _dot(a_ref[0], w_ref[...])


def _outproj(a, w, x, gate, name="outproj"):
    b, r, k = a.shape
    d = w.shape[1]
    tm = min(r, 512)
    return pl.pallas_call(
        _outproj_kernel,
        grid=(b, r // tm),
        in_specs=[pl.BlockSpec((1, tm, k), lambda bi, i: (bi, i, 0)),
                  pl.BlockSpec((k, d), lambda bi, i: (0, 0)),
                  pl.BlockSpec((1, tm, d), lambda bi, i: (bi, i, 0)),
                  pl.BlockSpec((1, 1, d), lambda bi, i: (bi, 0, 0))],
        out_specs=pl.BlockSpec((1, tm, d), lambda bi, i: (bi, i, 0)),
        out_shape=jax.ShapeDtypeStruct((b, r, d), F32),
        compiler_params=_cparams(("parallel", "parallel"), 48),
        name=name,
    )(a, w, x, gate)


FFN_ROW_TILE = 1024
FFN_VMEM_MIB = 56


def _ffn_kernel(x_ref, g_ref, sh_ref, sc_ref, gate_ref, wg_ref, wu_ref, wo_ref, o_ref, f_ref):
    j = pl.program_id(2)
    last = pl.num_programs(2) - 1

    def partial_out(f):
        half = wg_ref.shape[1] // 2
        halves = (slice(0, half), slice(half, 2 * half))
        pre = [(_dot(f, wg_ref[:, cs]), _dot(f, wu_ref[:, cs])) for cs in halves]
        return [_dot((_silu(gt) * up).astype(BF16), wo_ref[cs, :]) for (gt, up), cs in zip(pre, halves)]

    @pl.when(j == 0)
    def _():
        f = _adaln(x_ref[0], g_ref[...], sh_ref[0], sc_ref[0]).astype(BF16)
        f_ref[...] = f
        pa, pb = partial_out(f)
        o_ref[0] = pa
        o_ref[0] += pb

    @pl.when((j > 0) & (j < last))
    def _():
        for part in partial_out(f_ref[...]):
            o_ref[0] += part

    @pl.when(j == last)
    def _():
        pa, pb = partial_out(f_ref[...])
        o_ref[0] += pa
        o_ref[0] = x_ref[0] + gate_ref[0] * (o_ref[0] + pb)


def _ffn(x, g, shift, scale, gate, w_in, w_out, name="ffn"):
    b, r, d = x.shape
    hidden = w_out.shape[0]
    tm = min(r, FFN_ROW_TILE)
    tf = 512
    nf = hidden // tf
    assert hidden % tf == 0 and nf >= 2
    vec = pl.BlockSpec((1, 1, d), lambda bi, i, j: (bi, 0, 0))
    return pl.pallas_call(
        _ffn_kernel,
        grid=(b, r // tm, nf),
        in_specs=[pl.BlockSpec((1, tm, d), lambda bi, i, j: (bi, i, 0)),
                  pl.BlockSpec((1, d), lambda bi, i, j: (0, 0)),
                  vec, vec, vec,
                  pl.BlockSpec((d, tf), lambda bi, i, j: (0, j)),
                  pl.BlockSpec((d, tf), lambda bi, i, j: (0, j + nf)),
                  pl.BlockSpec((tf, d), lambda bi, i, j: (j, 0))],
        out_specs=pl.BlockSpec((1, tm, d), lambda bi, i, j: (bi, i, 0)),
        out_shape=jax.ShapeDtypeStruct((b, r, d), F32),
        scratch_shapes=[pltpu.VMEM((tm, d), BF16)],
        compiler_params=_cparams(("parallel", "parallel", "arbitrary"), FFN_VMEM_MIB),
        name=name,
    )(x, g, shift, scale, gate, w_in, w_in, w_out)


def _softplus(x):
    return jnp.maximum(x, 0.0) + jnp.log1p(jnp.exp(-jnp.abs(x)))


def _gelu_tanh(x):
    return 0.5 * x * (1.0 + jnp.tanh(math.sqrt(2.0 / math.pi) * (x + 0.044715 * (x * x * x))))


def _rglru_tile(rec, prev8, next8, wg_ref, gb_ref, lam_ref, cw_ref, cb_ref,
                ubuf, a_s, b_s, h_s, hcar, *, reverse):
    tt = rec.shape[0]
    ubuf[0:8, :] = prev8
    ubuf[8:8 + tt, :] = rec
    ubuf[8 + tt:16 + tt, :] = next8
    base = 8 - CONV_PAD_LEFT
    u = cb_ref[...] + cw_ref[0:1, :] * ubuf[base:base + tt, :]
    for j in range(1, CONV_WIDTH):
        u = u + cw_ref[j:j + 1, :] * ubuf[base + j:base + j + tt, :]
    sp = _softplus(-lam_ref[...])
    for k in range(LRU_BLOCKS):
        cs = slice(k * LRU_BLOCK, (k + 1) * LRU_BLOCK)
        uk = u[:, cs]
        gts = _dot(uk.astype(BF16), wg_ref[k])
        r = _sigmoid(gts[:, :LRU_BLOCK] + gb_ref[0:1, cs])
        i = _sigmoid(gts[:, LRU_BLOCK:] + gb_ref[1:2, cs])
        log_a = (-LRU_C) * r * sp[:, cs]
        a = jnp.exp(log_a)
        a_s[:, cs] = a
        b_s[:, cs] = jnp.sqrt(-jnp.tanh(log_a) * (a * a + 1.0)) * (i * uk)

    def body(t, h):
        row = (tt - 1 - t) if reverse else t
        h = a_s[pl.ds(row, 1), :] * h + b_s[pl.ds(row, 1), :]
        h_s[pl.ds(row, 1), :] = h
        return h

    hcar[...] = lax.fori_loop(0, tt, body, hcar[...], unroll=8)


def _rglru_kernel(*refs, reverse, tt, nt):
    if reverse:
        (gl_ref, rl_ref, pv_ref, nx_ref, gc_ref, rc_ref, hfl_ref, hfc_ref,
         wg_ref, gb_ref, lam_ref, cw_ref, cb_ref, ol_ref, oc_ref, ubuf, a_s, b_s, h_s, hcar) = refs
    else:
        (rl_ref, pv_ref, nx_ref, rc_ref,
         wg_ref, gb_ref, lam_ref, cw_ref, cb_ref, ol_ref, oc_ref, ubuf, a_s, b_s, h_s, hcar) = refs
    s = pl.program_id(1)
    tile = functools.partial(_rglru_tile, wg_ref=wg_ref, gb_ref=gb_ref, lam_ref=lam_ref, cw_ref=cw_ref,
                             cb_ref=cb_ref, ubuf=ubuf, a_s=a_s, b_s=b_s, h_s=h_s, hcar=hcar, reverse=reverse)
    zeros8 = jnp.zeros((8, rl_ref.shape[2]), F32)

    @pl.when(s == 0)
    def _():
        hcar[...] = jnp.zeros_like(hcar)
        tile(rc_ref[0].astype(F32), zeros8, zeros8)
        if reverse:
            oc_ref[0] = (_gelu_tanh(gc_ref[0].astype(F32)) * (hfc_ref[0].astype(F32) + h_s[...])).astype(oc_ref.dtype)
        else:
            oc_ref[0] = h_s[...].astype(oc_ref.dtype)

    @pl.when(s > 0)
    def _():
        tl = (nt - s) if reverse else (s - 1)
        has_prev = (tl > 0).astype(F32)
        has_next = (tl < nt - 1).astype(F32)
        prev8 = pv_ref[0].astype(F32)[8:16, :] * has_prev
        next8 = nx_ref[0].astype(F32)[0:8, :] * has_next
        tile(rl_ref[0].astype(F32), prev8, next8)
        if reverse:
            ol_ref[0] = (_gelu_tanh(gl_ref[0].astype(F32)) * (hfl_ref[0].astype(F32) + h_s[...])).astype(ol_ref.dtype)
        else:
            ol_ref[0] = h_s[...].astype(ol_ref.dtype)


def _rglru_scan(gr_lat, gr_ctx, wg, gb, lam, cw, cb, hf_lat=None, hf_ctx=None):
    reverse = hf_lat is not None
    b, s_len, w2 = gr_lat.shape
    w = w2 // 2
    tt = gr_ctx.shape[1]
    assert s_len % tt == 0 and tt % 16 == 0
    nt = s_len // tt
    hb = tt // 16

    def lat_tile(si):
        return (nt - jnp.maximum(si, 1)) if reverse else jnp.maximum(si - 1, 0)

    lat_rows = lambda col: pl.BlockSpec((1, tt, w), lambda bi, si: (bi, lat_tile(si), col))
    ctx_rows = lambda col: pl.BlockSpec((1, tt, w), lambda bi, si: (bi, 0, col))
    prev_spec = pl.BlockSpec((1, 16, w), lambda bi, si: (bi, jnp.maximum(lat_tile(si) * hb - 1, 0), 1))
    next_spec = pl.BlockSpec((1, 16, w), lambda bi, si: (bi, jnp.minimum((lat_tile(si) + 1) * hb, nt * hb - 1), 1))
    full = lambda shape: pl.BlockSpec(shape, lambda bi, si: (0,) * len(shape))
    params = [wg, gb, lam, cw, cb]
    param_specs = [full(wg.shape), full(gb.shape), full(lam.shape), full(cw.shape), full(cb.shape)]
    if reverse:
        args = [gr_lat, gr_lat, gr_lat, gr_lat, gr_ctx, gr_ctx, hf_lat, hf_ctx] + params
        in_specs = [lat_rows(0), lat_rows(1), prev_spec, next_spec, ctx_rows(0), ctx_rows(1),
                    lat_rows(0), ctx_rows(0)] + param_specs
    else:
        args = [gr_lat, gr_lat, gr_lat, gr_ctx] + params
        in_specs = [lat_rows(1), prev_spec, next_spec, ctx_rows(1)] + param_specs
    return pl.pallas_call(
        functools.partial(_rglru_kernel, reverse=reverse, tt=tt, nt=nt),
        grid=(b, nt + 1),
        in_specs=in_specs,
        out_specs=[lat_rows(0), ctx_rows(0)],
        out_shape=[jax.ShapeDtypeStruct((b, s_len, w), BF16), jax.ShapeDtypeStruct((b, tt, w), BF16)],
        scratch_shapes=[pltpu.VMEM((tt + 16, w), F32), pltpu.VMEM((tt, w), F32), pltpu.VMEM((tt, w), F32),
                        pltpu.VMEM((tt, w), F32), pltpu.VMEM((1, w), F32)],
        compiler_params=_cparams(("parallel", "arbitrary"), 48),
        name="rglru_bwd" if reverse else "rglru_fwd",
    )(*args)


def _gla_mask(reverse):
    r_i = lax.broadcasted_iota(jnp.int32, (GLA_CHUNK, GLA_CHUNK), 0)
    c_i = lax.broadcasted_iota(jnp.int32, (GLA_CHUNK, GLA_CHUNK), 1)
    return (c_i >= r_i) if reverse else (c_i <= r_i)


def _hgrn_head(q, f, v, st_t, mask, reverse):
    rows = q.shape[0]
    c = GLA_CHUNK
    half = c // 2
    nchunk = rows // c
    chunks = [slice(n * c, (n + 1) * c) for n in range(nchunk)]
    tri = jnp.where(mask, 1.0, 0.0).astype(BF16)
    g = jnp.log(f)
    k = 1.0 - f
    hi = g.astype(BF16)
    r1 = g - hi.astype(F32)
    mid = r1.astype(BF16)
    lo = (r1 - mid.astype(F32)).astype(BF16)
    g3 = jnp.concatenate([hi, mid, lo], axis=1)
    c3s = [_dot(tri, g3[rs]) for rs in chunks]
    yield
    cums = [(c3[:, :LANES] + c3[:, LANES:2 * LANES]) + c3[:, 2 * LANES:] for c3 in c3s]
    if reverse:
        totals = [cm[0:1] for cm in cums]
        refs = [cm[half:half + 1] for cm in cums]
    else:
        totals = [cm[c - 1:c] for cm in cums]
        refs = [cm[half - 1:half] for cm in cums]
    cum = jnp.concatenate(cums, axis=0)
    ref_b = jnp.concatenate([jnp.broadcast_to(r, (c, LANES)) for r in refs], axis=0)
    e_q = jnp.exp(cum - ref_b)
    qt = q * e_q
    kt = k * (1.0 / e_q)
    qtb, ktb, vb = qt.astype(BF16), kt.astype(BF16), v.astype(BF16)
    raw, upds, decays = [], [], []
    for n, rs in enumerate(chunks):
        raw.append(_dot_nt(qtb[rs], ktb[rs]))
        kbar = (kt[rs] * jnp.exp(totals[n] - refs[n])).astype(BF16)
        upds.append(_dot_tn(vb[rs], kbar))
        decays.append(jnp.exp(totals[n]))
    yield
    outs = [_dot(jnp.where(mask, s, 0.0).astype(BF16), vb[rs]) for s, rs in zip(raw, chunks)]
    yield
    for n in (range(nchunk - 1, -1, -1) if reverse else range(nchunk)):
        qi = (qt[chunks[n]] * jnp.exp(refs[n])).astype(BF16)
        outs[n] = outs[n] + _dot_nt(qi, st_t.astype(BF16))
        st_t = st_t * decays[n] + upds[n]
    return jnp.concatenate(outs, axis=0), st_t


def _run_in_lockstep(generators):
    results = [None] * len(generators)
    active = list(range(len(generators)))
    while active:
        for i in list(active):
            try:
                next(generators[i])
            except StopIteration as done:
                results[i] = done.value
                active.remove(i)
    return results


HGRN_HEADS_PER_STEP = 16


def _hgrn_kernel(*refs, reverse, layer):
    if reverse:
        (ql, fl, il, gl, ofl, qc, fc, ic, gc, ofc, lg_ref, gn_ref, ol_ref, oc_ref, st_ref) = refs
    else:
        (ql, fl, il, qc, fc, ic, lg_ref, ol_ref, oc_ref, st_ref) = refs
    is_ctx = pl.program_id(2) == 0

    @pl.when(is_ctx)
    def _():
        st_ref[...] = jnp.zeros_like(st_ref)

    def pick(c_ref, l_ref, hh):
        return jnp.where(is_ctx, c_ref[0, hh], l_ref[0, hh]).astype(F32)

    mask = _gla_mask(reverse)
    heads = []
    for hh in range(HGRN_HEADS_PER_STEP):
        cs = slice(hh * HGRN_HEAD_DIM, (hh + 1) * HGRN_HEAD_DIM)
        lg = lg_ref[:, cs]
        e = jnp.exp(lg - jnp.max(lg, axis=0, keepdims=True))
        sm = e / jnp.sum(e, axis=0, keepdims=True)
        lb = jnp.zeros((1, HGRN_HEAD_DIM), F32)
        for l in range(1, layer + 1):
            lb = lb + sm[l:l + 1]
        q = _silu(pick(qc, ql, hh))
        f = lb + (1.0 - lb) * _sigmoid(pick(fc, fl, hh))
        heads.append(_hgrn_head(q, f, pick(ic, il, hh), st_ref[hh], mask, reverse))
    outs = []
    for hh, (o, st_new) in enumerate(_run_in_lockstep(heads)):
        st_ref[hh] = st_new
        if reverse:
            o = (_rms(pick(ofc, ofl, hh) + o) * gn_ref[...]) * _silu(pick(gc, gl, hh))
        outs.append(o.astype(ol_ref.dtype))

    def store(out_ref):
        for hh, ob in enumerate(outs):
            if reverse:
                out_ref[0, :, hh * HGRN_HEAD_DIM:(hh + 1) * HGRN_HEAD_DIM] = ob
            else:
                out_ref[0, hh] = ob

    @pl.when(is_ctx)
    def _():
        store(oc_ref)

    @pl.when(jnp.logical_not(is_ctx))
    def _():
        store(ol_ref)


def _hgrn_scan(p_lat, p_ctx, logits_d, layer, gn=None, of_lat=None, of_ctx=None):
    reverse = of_lat is not None
    b, _, s_len, hd = p_lat.shape
    h, hps = HGRN_HEADS, HGRN_HEADS_PER_STEP
    tt = p_ctx.shape[2]
    assert s_len % tt == 0 and tt % GLA_CHUNK == 0 and h % hps == 0
    nt = s_len // tt
    ng = h // hps
    fsel = 2 if reverse else 1

    def lat_tile(si):
        return (nt - jnp.maximum(si, 1)) if reverse else jnp.maximum(si - 1, 0)

    lat = lambda grp: pl.BlockSpec((1, hps, tt, hd), lambda bi, hi, si: (bi, grp * ng + hi, lat_tile(si), 0))
    ctx = lambda grp: pl.BlockSpec((1, hps, tt, hd), lambda bi, hi, si: (bi, grp * ng + hi, 0, 0))
    lg_spec = pl.BlockSpec((logits_d.shape[0], hps * hd), lambda bi, hi, si: (0, hi))
    if reverse:
        args = [p_lat, p_lat, p_lat, p_lat, of_lat, p_ctx, p_ctx, p_ctx, p_ctx, of_ctx, logits_d, gn]
        in_specs = [lat(0), lat(fsel), lat(3), lat(4), lat(0), ctx(0), ctx(fsel), ctx(3), ctx(4), ctx(0),
                    lg_spec, pl.BlockSpec((1, hd), lambda bi, hi, si: (0, 0))]
        out_specs = [pl.BlockSpec((1, tt, hps * hd), lambda bi, hi, si: (bi, lat_tile(si), hi)),
                     pl.BlockSpec((1, tt, hps * hd), lambda bi, hi, si: (bi, 0, hi))]
        out_shape = [jax.ShapeDtypeStruct((b, s_len, h * hd), BF16), jax.ShapeDtypeStruct((b, tt, h * hd), BF16)]
    else:
        args = [p_lat, p_lat, p_lat, p_ctx, p_ctx, p_ctx, logits_d]
        in_specs = [lat(0), lat(fsel), lat(3), ctx(0), ctx(fsel), ctx(3), lg_spec]
        out_specs = [lat(0), ctx(0)]
        out_shape = [jax.ShapeDtypeStruct((b, h, s_len, hd), BF16), jax.ShapeDtypeStruct((b, h, tt, hd), BF16)]
    return pl.pallas_call(
        functools.partial(_hgrn_kernel, reverse=reverse, layer=layer),
        grid=(b, ng, nt + 1),
        in_specs=in_specs,
        out_specs=out_specs,
        out_shape=out_shape,
        scratch_shapes=[pltpu.VMEM((hps, hd, hd), F32)],
        compiler_params=_cparams(("parallel", "parallel", "arbitrary"), 32),
        name="hgrn_bwd" if reverse else "hgrn_fwd",
    )(*args)


def _rope_tables(n_lat, rot_dim, reps):
    n_freq = rot_dim // 4
    t = jnp.arange(n_lat)
    inv = ROPE_THETA ** (-jnp.arange(n_freq, dtype=F32) / n_freq)
    ang_r = (t // GRID_W).astype(F32)[:, None] * inv[None, :]
    ang_c = (t % GRID_W).astype(F32)[:, None] * inv[None, :]
    cos = jnp.concatenate([jnp.cos(ang_r)] * 2 + [jnp.cos(ang_c)] * 2, axis=-1)
    sin = jnp.concatenate([-jnp.sin(ang_r), jnp.sin(ang_r), -jnp.sin(ang_c), jnp.sin(ang_c)], axis=-1)
    return jnp.concatenate([jnp.tile(cos, (1, reps)), jnp.tile(sin, (1, reps))], axis=-1)


def _rope(x, cos, sin, n_freq):
    width = x.shape[1]
    lane = lax.broadcasted_iota(jnp.int32, (1, width), 1)
    first = (lane % (2 * n_freq)) < n_freq
    partner = jnp.where(first, pltpu.roll(x, width - n_freq, 1), pltpu.roll(x, n_freq, 1))
    return x * cos + partner * sin


def _mla_qkv_kernel(*refs, use_rope):
    if use_rope:
        (c_ref, qg_ref, kvg_ref, wq_ref, wk_ref, wvt_ref, gqn_ref, gqr_ref, gkn_ref, gkr_ref, tab_ref,
         q_ref, k_ref, vt_ref) = refs
    else:
        (c_ref, qg_ref, kvg_ref, wq_ref, wk_ref, wvt_ref, gqn_ref, gqr_ref, gkn_ref, gkr_ref,
         q_ref, k_ref, vt_ref) = refs
    c = c_ref[0].astype(F32)
    cqn = (_rms(c[:, :MLA_Q_RANK]) * qg_ref[...]).astype(BF16)
    ckvn = (_rms(c[:, MLA_Q_RANK:MLA_Q_RANK + MLA_KV_RANK]) * kvg_ref[...]).astype(BF16)
    kr = c[:, MLA_Q_RANK + MLA_KV_RANK:]
    qa = _dot(cqn, wq_ref[...])
    ka = _dot(ckvn, wk_ref[...])
    vt = _dot_nt(wvt_ref[...], ckvn)
    lane = lax.broadcasted_iota(jnp.int32, (1, LANES), 1)
    low = lane < MLA_ROPE
    if use_rope:
        cos, sin = tab_ref[:, :LANES], tab_ref[:, LANES:]
    inv_rope = 1.0 / MLA_ROPE

    def halves_rms(x):
        sq = x * x
        ss_lo = jnp.sum(jnp.where(low, sq, 0.0), axis=-1, keepdims=True)
        ss_hi = jnp.sum(jnp.where(low, 0.0, sq), axis=-1, keepdims=True)
        return x * jnp.where(low, lax.rsqrt(ss_lo * inv_rope + NORM_EPS), lax.rsqrt(ss_hi * inv_rope + NORM_EPS))

    krn = halves_rms(kr) * gkr_ref[...]
    if use_rope:
        krn = _rope(krn, cos, sin, MLA_ROPE // 4)
    krn = jnp.where(low, krn, 0.0)
    nope_w = MLA_HEADS * MLA_NOPE
    qscale = MLA_SCALE * LOG2E
    for p in range(MLA_HEADS // 2):
        qr = halves_rms(qa[:, nope_w + p * LANES:nope_w + (p + 1) * LANES]) * gqr_ref[...]
        if use_rope:
            qr = _rope(qr, cos, sin, MLA_ROPE // 4)
        for e in range(2):
            h = 2 * p + e
            hs = slice(h * MLA_NOPE, (h + 1) * MLA_NOPE)
            qn = _rms(qa[:, hs]) * gqn_ref[...]
            rot = qr if e == 0 else pltpu.roll(qr, MLA_ROPE, 1)
            rot = jnp.where(low, rot, 0.0)
            q_ref[0, h] = (jnp.concatenate([qn, rot], axis=1) * qscale).astype(BF16)
            kn = _rms(ka[:, hs]) * gkn_ref[...]
            k_ref[0, h] = jnp.concatenate([kn, krn], axis=1).astype(BF16)
            vt_ref[0, h, 0, :MLA_V, :] = vt[hs, :].astype(BF16)
            vt_ref[0, h, 0, MLA_V:, :] = jnp.ones((MLA_V_ROWS - MLA_V, vt.shape[1]), BF16)


def _mla_qkv(c, qg, kvg, wq, wk, wvt, gqn, gqr, gkn, gkr, tab, tm):
    b, r, cw = c.shape
    assert r % tm == 0
    h = MLA_HEADS
    use_rope = tab is not None
    full = lambda a: pl.BlockSpec(a.shape, lambda bi, i: (0,) * a.ndim)
    args = [c, qg, kvg, wq, wk, wvt, gqn, gqr, gkn, gkr]
    in_specs = [pl.BlockSpec((1, tm, cw), lambda bi, i: (bi, i, 0))] + [full(a) for a in args[1:]]
    if use_rope:
        args.append(tab)
        in_specs.append(pl.BlockSpec((tm, tab.shape[1]), lambda bi, i: (i, 0)))
    return pl.pallas_call(
        functools.partial(_mla_qkv_kernel, use_rope=use_rope),
        grid=(b, r // tm),
        in_specs=in_specs,
        out_specs=[pl.BlockSpec((1, h, tm, MLA_QK_PAD), lambda bi, i: (bi, 0, i, 0)),
                   pl.BlockSpec((1, h, tm, MLA_QK_PAD), lambda bi, i: (bi, 0, i, 0)),
                   pl.BlockSpec((1, h, 1, MLA_V_ROWS, tm), lambda bi, i: (bi, 0, i, 0, 0))],
        out_shape=[jax.ShapeDtypeStruct((b, h, r, MLA_QK_PAD), BF16),
                   jax.ShapeDtypeStruct((b, h, r, MLA_QK_PAD), BF16),
                   jax.ShapeDtypeStruct((b, h, r // tm, MLA_V_ROWS, tm), BF16)],
        compiler_params=_cparams(("parallel", "parallel"), 48),
        name="mla_qkv_rope" if use_rope else "mla_qkv",
    )(*args)


MLA_KV_TILES_PER_TRIP = 16
MLA_ONLINE_TILES_PER_TRIP = 2
MLA_Q_CHUNK = 512
MLA_Q_CHUNKS_PER_STEP = 2


def _attn_update(s, vt, m, acc):
    m_new = jnp.maximum(m, jnp.max(s, axis=0, keepdims=True))
    alpha = jnp.exp2(m - m_new)
    p = jnp.exp2(s - m_new).astype(BF16)
    return m_new, alpha * acc + _dot(vt, p)


MLA_STALE_MARGIN = 100.0
MLA_STALE_BOUND = (126.0 + MLA_STALE_MARGIN) / 2.0


def _attn_update_stale(s, vt, m, acc):
    p = jnp.exp2(s - m).astype(BF16)
    m_new = jnp.maximum(m, jnp.max(s, axis=0, keepdims=True))
    return m_new, (acc + _dot(vt, p)) * jnp.exp2(m - m_new)


def _mla_attn_stale(qs, kc_ref, vc_ref, kl_ref, vl_ref, floor):
    nk, tk = vl_ref.shape[2], vl_ref.shape[4]
    unroll = min(MLA_KV_TILES_PER_TRIP, nk)
    ms, accs = [], []
    for q in qs:
        m = jnp.full((1, q.shape[0]), floor, F32)
        acc = jnp.zeros((vc_ref.shape[3], q.shape[0]), F32)
        m, acc = _attn_update(_dot_nt(kc_ref[0, 0], q), vc_ref[0, 0, 0], m, acc)
        ms.append(m)
        accs.append(acc)

    def body(i, carry):
        ms, accs = list(carry[0]), list(carry[1])
        for u in range(unroll):
            t = unroll * i + u
            k = kl_ref[0, 0, pl.ds(pl.multiple_of(t * tk, tk), tk), :]
            vt = vl_ref[0, 0, t]
            ss = [_dot_nt(k, q) for q in qs]
            for c in range(len(qs)):
                ms[c], accs[c] = _attn_update_stale(ss[c], vt, ms[c], accs[c])
        return tuple(ms), tuple(accs)

    return lax.fori_loop(0, nk // unroll, body, (tuple(ms), tuple(accs)))[1]


def _mla_attn_online(q, kc_ref, vc_ref, kl_ref, vl_ref):
    nk, tk = vl_ref.shape[2], vl_ref.shape[4]
    unroll = min(MLA_ONLINE_TILES_PER_TRIP, nk)
    m = jnp.full((1, q.shape[0]), -jnp.inf, F32)
    acc = jnp.zeros((vc_ref.shape[3], q.shape[0]), F32)

    def scores(j):
        return _dot_nt(kl_ref[0, 0, pl.ds(pl.multiple_of(j * tk, tk), tk), :], q)

    s_cur = scores(0)
    m, acc = _attn_update(_dot_nt(kc_ref[0, 0], q), vc_ref[0, 0, 0], m, acc)

    def body(i, carry):
        s_cur, m, acc = carry
        for u in range(unroll):
            t = unroll * i + u
            s_next = scores(t + 1)
            m, acc = _attn_update(s_cur, vl_ref[0, 0, t], m, acc)
            s_cur = s_next
        return s_cur, m, acc

    s_cur, m, acc = lax.fori_loop(0, nk // unroll - 1, body, (s_cur, m, acc))
    for t in range(nk - unroll, nk):
        s_next = scores(t + 1) if t + 1 < nk else None
        m, acc = _attn_update(s_cur, vl_ref[0, 0, t], m, acc)
        s_cur = s_next
    return acc


def _mla_attn_kernel(*refs, with_lat, chunk):
    if with_lat:
        bound_ref, q_ref, kc_ref, vc_ref, kl_ref, vl_ref, o_ref = refs
    else:
        q_ref, kc_ref, vc_ref, o_ref = refs
    tq = q_ref.shape[2]
    rows = [slice(c * chunk, (c + 1) * chunk) for c in range(tq // chunk)]

    def finish(acc, rs):
        o_ref[0, rs, :] = (acc[:MLA_V] / acc[MLA_V:MLA_V + 1]).T.astype(o_ref.dtype)

    if with_lat:
        bound = bound_ref[0]

        @pl.when(bound < MLA_STALE_BOUND)
        def _():
            accs = _mla_attn_stale([q_ref[0, 0, rs, :] for rs in rows], kc_ref, vc_ref, kl_ref, vl_ref,
                                   bound - MLA_STALE_MARGIN)
            for acc, rs in zip(accs, rows):
                finish(acc, rs)

        @pl.when(bound >= MLA_STALE_BOUND)
        def _():
            for rs in rows:
                finish(_mla_attn_online(q_ref[0, 0, rs, :], kc_ref, vc_ref, kl_ref, vl_ref), rs)
    else:
        for rs in rows:
            q = q_ref[0, 0, rs, :]
            m = jnp.full((1, chunk), -jnp.inf, F32)
            acc = jnp.zeros((vc_ref.shape[3], chunk), F32)
            finish(_attn_update(_dot_nt(kc_ref[0, 0], q), vc_ref[0, 0, 0], m, acc)[1], rs)


def _mla_attention(q, k_ctx, vt_ctx, k_lat=None, vt_lat=None, bound=None):
    b, h, n, dq = q.shape
    with_lat = k_lat is not None
    chunk = min(n, MLA_Q_CHUNK)
    tq = min(n, MLA_Q_CHUNKS_PER_STEP * chunk)
    assert n % tq == 0
    rc = k_ctx.shape[2]
    args = [q, k_ctx, vt_ctx]
    in_specs = [pl.BlockSpec((1, 1, tq, dq), lambda bi, hi, i: (bi, hi, i, 0)),
                pl.BlockSpec((1, 1, rc, dq), lambda bi, hi, i: (bi, hi, 0, 0)),
                pl.BlockSpec((1, 1, 1, MLA_V_ROWS, rc), lambda bi, hi, i: (bi, hi, 0, 0, 0))]
    if with_lat:
        nk, tk = vt_lat.shape[2], vt_lat.shape[4]
        assert nk % min(MLA_KV_TILES_PER_TRIP, nk) == 0 and nk % min(MLA_ONLINE_TILES_PER_TRIP, nk) == 0
        args = [bound] + args + [k_lat, vt_lat]
        in_specs = [pl.BlockSpec(memory_space=pltpu.SMEM)] + in_specs
        in_specs += [pl.BlockSpec((1, 1, nk * tk, dq), lambda bi, hi, i: (bi, hi, 0, 0)),
                     pl.BlockSpec((1, 1, nk, MLA_V_ROWS, tk), lambda bi, hi, i: (bi, hi, 0, 0, 0))]
    return pl.pallas_call(
        functools.partial(_mla_attn_kernel, with_lat=with_lat, chunk=chunk),
        grid=(b, h, n // tq),
        in_specs=in_specs,
        out_specs=pl.BlockSpec((1, tq, MLA_V), lambda bi, hi, i: (bi, i, hi)),
        out_shape=jax.ShapeDtypeStruct((b, n, h * MLA_V), BF16),
        compiler_params=_cparams(("parallel", "parallel", "arbitrary"), 48),
        name="mla_attn" if with_lat else "mla_attn_ctx",
    )(*args)


def _swa_inproj_kernel(*refs, use_rope, n_norm_tiles):
    if use_rope:
        x_ref, g_ref, sh_ref, sc_ref, w_ref, hg_ref, tab_ref, o_ref, h_ref = refs
    else:
        x_ref, g_ref, sh_ref, sc_ref, w_ref, hg_ref, o_ref, h_ref = refs
    j = pl.program_id(2)

    @pl.when(j == 0)
    def _():
        h_ref[...] = _adaln(x_ref[0], g_ref[...], sh_ref[0], sc_ref[0]).astype(BF16)

    res = _dot(h_ref[...], w_ref[...])

    @pl.when(j < n_norm_tiles)
    def _():
        for e in range(res.shape[1] // SWA_HEAD_DIM):
            cs = slice(e * SWA_HEAD_DIM, (e + 1) * SWA_HEAD_DIM)
            xh = _rms(res[:, cs]) * hg_ref[:, cs]
            if use_rope:
                xh = _rope(xh, tab_ref[:, :SWA_HEAD_DIM], tab_ref[:, SWA_HEAD_DIM:], SWA_HEAD_DIM // 4)
            o_ref[0, :, cs] = xh.astype(o_ref.dtype)

    @pl.when(j >= n_norm_tiles)
    def _():
        o_ref[0] = res.astype(o_ref.dtype)


def _swa_inproj(x, g, shift, scale, w, head_gain, tab):
    b, r, d = x.shape
    n = w.shape[1]
    tm = min(r, 512)
    tn = 512
    use_rope = tab is not None
    n_norm_tiles = (SWA_Q_HEADS + SWA_KV_HEADS) * SWA_HEAD_DIM // tn
    vec = pl.BlockSpec((1, 1, d), lambda bi, i, j: (bi, 0, 0))
    args = [x, g, shift, scale, w, head_gain]
    in_specs = [pl.BlockSpec((1, tm, d), lambda bi, i, j: (bi, i, 0)),
                pl.BlockSpec((1, d), lambda bi, i, j: (0, 0)),
                vec, vec,
                pl.BlockSpec((d, tn), lambda bi, i, j: (0, j)),
                pl.BlockSpec((1, tn), lambda bi, i, j: (0, j))]
    if use_rope:
        args.append(tab)
        in_specs.append(pl.BlockSpec((tm, tab.shape[1]), lambda bi, i, j: (i, 0)))
    return pl.pallas_call(
        functools.partial(_swa_inproj_kernel, use_rope=use_rope, n_norm_tiles=n_norm_tiles),
        grid=(b, r // tm, n // tn),
        in_specs=in_specs,
        out_specs=pl.BlockSpec((1, tm, tn), lambda bi, i, j: (bi, i, j)),
        out_shape=jax.ShapeDtypeStruct((b, r, n), BF16),
        scratch_shapes=[pltpu.VMEM((tm, d), BF16)],
        compiler_params=_cparams(("parallel", "parallel", "arbitrary"), 48),
        name="swa_inproj_rope" if use_rope else "swa_inproj",
    )(*args)


def _swa_block(q, keys, vals, sinks, nctx, prev_ok, next_ok):
    blk, grp = SWA_BLOCK, SWA_GROUP
    qs = jnp.concatenate([q[:, e * SWA_HEAD_DIM:(e + 1) * SWA_HEAD_DIM] for e in range(grp)], axis=0)
    s = _dot_nt(qs, jnp.concatenate(keys, axis=0))
    yield
    rows = s.shape[0]
    r_i = lax.broadcasted_iota(jnp.int32, (rows, blk), 0) % blk
    c_i = lax.broadcasted_iota(jnp.int32, (rows, blk), 1)
    off_prev = jnp.where(prev_ok, 0, 2 * blk)
    off_next = jnp.where(next_ok, 0, 2 * blk)
    s_prev = jnp.where(c_i >= r_i + off_prev, s[:, nctx:nctx + blk], MASK_VALUE)
    s_next = jnp.where(c_i + off_next <= r_i, s[:, nctx + 2 * blk:], MASK_VALUE)
    s = jnp.concatenate([s[:, :nctx], s_prev, s[:, nctx + blk:nctx + 2 * blk], s_next], axis=1)
    row1 = lax.broadcasted_iota(jnp.int32, (rows, 1), 0)
    sink = jnp.zeros((rows, 1), F32)
    for e in range(grp):
        sink = jnp.where(row1 // blk == e, sinks[e], sink)
    m = jnp.maximum(jnp.max(s, axis=1, keepdims=True), sink)
    p = jnp.exp2(s - m)
    l = jnp.sum(p, axis=1, keepdims=True) + jnp.exp2(sink - m)
    o = _dot(p.astype(BF16), jnp.concatenate(vals, axis=0)) / l
    return jnp.concatenate([o[e * blk:(e + 1) * blk] for e in range(grp)], axis=1)


def _swa_attn_kernel(sink_ref, q_ref, kp_ref, kc_ref, kn_ref, vp_ref, vc_ref, vn_ref, kx_ref, vx_ref, o_ref):
    hk = pl.program_id(1)
    i = pl.program_id(2)
    last = pl.num_programs(2) - 1
    blk = SWA_BLOCK
    nctx = kx_ref.shape[1]
    sinks = [sink_ref[hk * SWA_GROUP + e] * LOG2E for e in range(SWA_GROUP)]
    k0, k1 = kc_ref[0, :blk, :], kc_ref[0, blk:, :]
    v0, v1 = vc_ref[0, :blk, :], vc_ref[0, blk:, :]
    o_a, o_b = _run_in_lockstep([
        _swa_block(q_ref[0, :blk, :], [kx_ref[0], kp_ref[0], k0, k1], [vx_ref[0], vp_ref[0], v0, v1],
                   sinks, nctx, i > 0, True),
        _swa_block(q_ref[0, blk:, :], [kx_ref[0], k0, k1, kn_ref[0]], [vx_ref[0], v0, v1, vn_ref[0]],
                   sinks, nctx, True, i < last)])
    o_ref[0, :blk, :] = o_a.astype(o_ref.dtype)
    o_ref[0, blk:, :] = o_b.astype(o_ref.dtype)


def _swa_attention(qkv_lat, qkv_ctx, sink):
    b, n, _ = qkv_lat.shape
    nctx = qkv_ctx.shape[1]
    blk, dh, grp = SWA_BLOCK, SWA_HEAD_DIM, SWA_GROUP
    nb = n // blk
    assert nb % 2 == 0
    kcol = SWA_Q_HEADS
    vcol = SWA_Q_HEADS + SWA_KV_HEADS
    prev = lambda col0: pl.BlockSpec((1, blk, dh), lambda bi, hk, i: (bi, jnp.maximum(2 * i - 1, 0), col0 + hk))
    nxt = lambda col0: pl.BlockSpec((1, blk, dh), lambda bi, hk, i: (bi, jnp.minimum(2 * i + 2, nb - 1), col0 + hk))
    pair = lambda col0: pl.BlockSpec((1, 2 * blk, dh), lambda bi, hk, i: (bi, i, col0 + hk))
    ctx = lambda col0: pl.BlockSpec((1, nctx, dh), lambda bi, hk, i: (bi, 0, col0 + hk))
    return pl.pallas_call(
        _swa_attn_kernel,
        grid=(b, SWA_KV_HEADS, nb // 2),
        in_specs=[pl.BlockSpec(memory_space=pltpu.SMEM),
                  pl.BlockSpec((1, 2 * blk, grp * dh), lambda bi, hk, i: (bi, i, hk)),
                  prev(kcol), pair(kcol), nxt(kcol),
                  prev(vcol), pair(vcol), nxt(vcol),
                  ctx(kcol), ctx(vcol)],
        out_specs=pl.BlockSpec((1, 2 * blk, grp * dh), lambda bi, hk, i: (bi, i, hk)),
        out_shape=jax.ShapeDtypeStruct((b, n, SWA_Q_HEADS * dh), BF16),
        compiler_params=_cparams(("parallel", "parallel", "arbitrary"), 32),
        name="swa_attn",
    )(sink, qkv_lat, qkv_lat, qkv_lat, qkv_lat, qkv_lat, qkv_lat, qkv_lat, qkv_ctx, qkv_ctx)


def kernel(x, c, ctx, c_ctx, mod_w, mod_b, norm_g, ffn_w_in, ffn_w_out, rglru_w_in, rglru_conv_w, rglru_conv_b, rglru_gate_w, rglru_gate_b, rglru_lambda, rglru_w_out, hgrn_w_in, hgrn_lb_logits, hgrn_gnorm_g, hgrn_w_out, mla_w_in, mla_q_norm_g, mla_kv_norm_g, mla_w_uq, mla_w_ukv, mla_qk_g, mla_w_out, swa_w_in, swa_qk_g, swa_sink, swa_w_out):
    b, n_lat, d = x.shape
    depth = mod_w.shape[0]
    assert depth == DEPTH and b + 1 <= 8
    bf = lambda a: a.astype(BF16)

    cond = jnp.concatenate([c, c_ctx[None, :], jnp.zeros((8 - b - 1, d), F32)], axis=0)
    mod = _modulation(cond, mod_w, mod_b).reshape(depth, 8, 6, d)

    x_lat, x_ctx = x, ctx
    for layer in range(depth):
        kind = layer % 4
        need_ctx = layer < depth - 1
        m_lat = [mod[layer, :b, k][:, None, :] for k in range(6)]
        m_ctx = [jnp.broadcast_to(mod[layer, b:b + 1, k][:, None, :], (b, 1, d)) for k in range(6)]
        g1 = norm_g[layer, 0][None, :]
        g2 = norm_g[layer, 1][None, :]
        y_ctx = None
        if kind == 0:
            w_in = bf(rglru_w_in[0])
            gr_lat = _inproj(x_lat, g1, m_lat[0], m_lat[1], w_in, tn=2048,name="rglru_inproj")
            gr_ctx = _inproj(x_ctx, g1, m_ctx[0], m_ctx[1], w_in, tn=2048,name="rglru_inproj_ctx")
            gw = rglru_gate_w[0]
            wg = [bf(jnp.concatenate([gw[dr, 0], gw[dr, 1]], axis=-1)) for dr in range(2)]
            cw, cb = rglru_conv_w[0], rglru_conv_b[0][None, :]
            scan = lambda dr, **kw: _rglru_scan(gr_lat, gr_ctx, wg[dr], rglru_gate_b[0, dr],
                                                rglru_lambda[0, dr][None, :], cw, cb, **kw)
            hf_lat, hf_ctx = scan(0)
            y_lat, y_ctx = scan(1, hf_lat=hf_lat, hf_ctx=hf_ctx)
            w_out = bf(rglru_w_out[0])
        elif kind == 1:
            w_in = bf(hgrn_w_in[0])
            p_lat = _inproj(x_lat, g1, m_lat[0], m_lat[1], w_in, tn=2048,head_major=True, name="hgrn_inproj")
            p_ctx = _inproj(x_ctx, g1, m_ctx[0], m_ctx[1], w_in, tn=2048,head_major=True, name="hgrn_inproj_ctx")
            of_lat, of_ctx = _hgrn_scan(p_lat, p_ctx, hgrn_lb_logits[:, 0, :], layer)
            y_lat, y_ctx = _hgrn_scan(p_lat, p_ctx, hgrn_lb_logits[:, 1, :], layer, gn=hgrn_gnorm_g[0][None, :],
                                      of_lat=of_lat, of_ctx=of_ctx)
            w_out = bf(hgrn_w_out[0])
        elif kind == 2:
            cw_real = mla_w_in.shape[2]
            cw_pad = -(-cw_real // LANES) * LANES
            w_in = bf(jnp.pad(mla_w_in[0], ((0, 0), (0, cw_pad - cw_real))))
            c_lat = _inproj(x_lat, g1, m_lat[0], m_lat[1], w_in, tn=cw_pad, name="mla_inproj")
            c_ctx_ = _inproj(x_ctx, g1, m_ctx[0], m_ctx[1], w_in, tn=cw_pad, name="mla_inproj_ctx")
            wq3 = mla_w_uq[0].reshape(MLA_Q_RANK, MLA_HEADS, MLA_NOPE + MLA_ROPE)
            wq = bf(jnp.concatenate([wq3[:, :, :MLA_NOPE].reshape(MLA_Q_RANK, -1),
                                     wq3[:, :, MLA_NOPE:].reshape(MLA_Q_RANK, -1)], axis=1))
            wkv3 = mla_w_ukv[0].reshape(MLA_KV_RANK, MLA_HEADS, MLA_NOPE + MLA_V)
            wk = bf(wkv3[:, :, :MLA_NOPE].reshape(MLA_KV_RANK, -1))
            wvt = bf(wkv3[:, :, MLA_NOPE:].reshape(MLA_KV_RANK, -1).T)
            qk_g = mla_qk_g[0]
            gqn, gkn = qk_g[0:1, :MLA_NOPE], qk_g[1:2, :MLA_NOPE]
            gqr = jnp.tile(qk_g[0:1, MLA_NOPE:], (1, LANES // MLA_ROPE))
            gkr = jnp.pad(qk_g[1:2, MLA_NOPE:], ((0, 0), (0, LANES - MLA_ROPE)))
            tab = _rope_tables(n_lat, MLA_ROPE, LANES // MLA_ROPE)
            small = (mla_q_norm_g[0][None, :], mla_kv_norm_g[0][None, :], wq, wk, wvt, gqn, gqr, gkn, gkr)
            q_l, k_l, vt_l = _mla_qkv(c_lat, *small, tab, min(n_lat, 512))
            q_c, k_c, vt_c = _mla_qkv(c_ctx_, *small, None, c_ctx_.shape[1])
            gmax2 = lambda g: jnp.max(g * g)
            qk_norm2 = ((MLA_NOPE * gmax2(gqn) + MLA_ROPE * gmax2(gqr))
                        * (MLA_NOPE * gmax2(gkn) + MLA_ROPE * gmax2(gkr)))
            bound = (1.05 * MLA_SCALE * LOG2E) * jnp.sqrt(qk_norm2).reshape(1)
            y_lat = _mla_attention(q_l, k_c, vt_c, k_l, vt_l, bound)
            if need_ctx:
                y_ctx = _mla_attention(q_c, k_c, vt_c)
            w_out = bf(mla_w_out[0])
        else:
            w_in = bf(swa_w_in[0])
            gq = jnp.tile(swa_qk_g[0, 0] * (SWA_SCALE * LOG2E), SWA_Q_HEADS)
            gk = jnp.tile(swa_qk_g[0, 1], SWA_KV_HEADS)
            head_gain = jnp.concatenate([gq, gk, jnp.ones((SWA_KV_HEADS * SWA_HEAD_DIM,), F32)])[None, :]
            tab = _rope_tables(n_lat, SWA_HEAD_DIM, 1)
            qkv_lat = _swa_inproj(x_lat, g1, m_lat[0], m_lat[1], w_in, head_gain, tab)
            qkv_ctx = _swa_inproj(x_ctx, g1, m_ctx[0], m_ctx[1], w_in, head_gain, None)
            y_lat = _swa_attention(qkv_lat, qkv_ctx, swa_sink[0])
            assert not need_ctx
            w_out = bf(swa_w_out[0])

        w_ffn_in, w_ffn_out = bf(ffn_w_in[layer]), bf(ffn_w_out[layer])
        x_lat = _outproj(y_lat, w_out, x_lat, m_lat[2])
        x_lat = _ffn(x_lat, g2, m_lat[3], m_lat[4], m_lat[5], w_ffn_in, w_ffn_out)
        if need_ctx:
            x_ctx = _outproj(y_ctx, w_out, x_ctx, m_ctx[2], name="outproj_ctx")
            x_ctx = _ffn(x_ctx, g2, m_ctx[3], m_ctx[4], m_ctx[5], w_ffn_in, w_ffn_out, name="ffn_ctx")
    return x_lat
```

```python
import functools
import math

import jax
import jax.numpy as jnp
from jax import lax
from jax.experimental import pallas as pl
from jax.experimental.pallas import tpu as pltpu

F32 = jnp.float32
BF16 = jnp.bfloat16

DEPTH = 4
GRID_W = 64
NORM_EPS = 1e-6
ROPE_THETA = 10000.0
LOG2E = math.log2(math.e)

LRU_BLOCKS = 16
LRU_BLOCK = 128
LRU_C = 8.0
CONV_WIDTH = 4
CONV_PAD_LEFT = 2

HGRN_HEADS = 16
HGRN_HEAD_DIM = 128
GLA_CHUNK = 64

MLA_HEADS = 16
MLA_Q_RANK = 512
MLA_KV_RANK = 512
MLA_NOPE = 128
MLA_ROPE = 64
MLA_V = 128
MLA_SCALE = (MLA_NOPE + MLA_ROPE) ** -0.5
MLA_QK_PAD = 256
MLA_V_ROWS = MLA_V + 16

SWA_Q_HEADS = 16
SWA_KV_HEADS = 4
SWA_GROUP = SWA_Q_HEADS // SWA_KV_HEADS
SWA_HEAD_DIM = 128
SWA_WINDOW = 128
SWA_BLOCK = 128
SWA_SCALE = SWA_HEAD_DIM ** -0.5

LANES = 128
MIB = 1024 * 1024
MASK_VALUE = -1e30


def _cparams(semantics, vmem_mib):
    return pltpu.CompilerParams(dimension_semantics=semantics, vmem_limit_bytes=vmem_mib * MIB)


def _sigmoid(x):
    return jax.nn.sigmoid(x)


def _silu(x):
    return x * _sigmoid(x)


def _rms(x, eps=NORM_EPS):
    return x * lax.rsqrt(jnp.mean(x * x, axis=-1, keepdims=True) + eps)


def _adaln(x, g, shift, scale):
    return (_rms(x) * g) * (1.0 + scale) + shift


def _dot(a, b):
    return jnp.dot(a, b, preferred_element_type=F32)


def _dot_nt(a, b):
    return lax.dot_general(a, b, (((1,), (1,)), ((), ())), preferred_element_type=F32)


def _dot_tn(a, b):
    return lax.dot_general(a, b, (((0,), (0,)), ((), ())), preferred_element_type=F32)


def _mod_kernel(c_ref, w_ref, b_ref, o_ref):
    s = _silu(c_ref[...]).astype(BF16)
    o_ref[0] = _dot(s, w_ref[0].astype(BF16)) + b_ref[0]


def _modulation(cond, mod_w, mod_b):
    depth, d, n = mod_w.shape
    tn = 1024
    return pl.pallas_call(
        _mod_kernel,
        grid=(depth, n // tn),
        in_specs=[pl.BlockSpec((8, d), lambda l, j: (0, 0)),
                  pl.BlockSpec((1, d, tn), lambda l, j: (l, 0, j)),
                  pl.BlockSpec((1, 1, tn), lambda l, j: (l, 0, j))],
        out_specs=pl.BlockSpec((1, 8, tn), lambda l, j: (l, 0, j)),
        out_shape=jax.ShapeDtypeStruct((depth, 8, n), F32),
        compiler_params=_cparams(("parallel", "parallel"), 40),
        name="modulation",
    )(cond, mod_w, mod_b.reshape(depth, 1, n))


def _inproj_kernel(x_ref, g_ref, sh_ref, sc_ref, w_ref, o_ref, h_ref, *, head_major):
    def project(h):
        res = _dot(h, w_ref[...])
        if head_major:
            for c in range(res.shape[1] // LANES):
                o_ref[0, c] = res[:, c * LANES:(c + 1) * LANES].astype(o_ref.dtype)
        else:
            o_ref[0] = res.astype(o_ref.dtype)

    @pl.when(pl.program_id(2) == 0)
    def _():
        h = _adaln(x_ref[0], g_ref[...], sh_ref[0], sc_ref[0]).astype(BF16)
        h_ref[...] = h
        project(h)

    @pl.when(pl.program_id(2) > 0)
    def _():
        project(h_ref[...])


def _inproj(x, g, shift, scale, w, *, tn, head_major=False, name="inproj"):
    b, r, d = x.shape
    n = w.shape[1]
    tm = min(r, 512)
    assert r % tm == 0 and n % tn == 0
    if head_major:
        out_shape = jax.ShapeDtypeStruct((b, n // LANES, r, LANES), BF16)
        out_spec = pl.BlockSpec((1, tn // LANES, tm, LANES), lambda bi, i, j: (bi, j, i, 0))
    else:
        out_shape = jax.ShapeDtypeStruct((b, r, n), BF16)
        out_spec = pl.BlockSpec((1, tm, tn), lambda bi, i, j: (bi, i, j))
    vec = pl.BlockSpec((1, 1, d), lambda bi, i, j: (bi, 0, 0))
    return pl.pallas_call(
        functools.partial(_inproj_kernel, head_major=head_major),
        grid=(b, r // tm, n // tn),
        in_specs=[pl.BlockSpec((1, tm, d), lambda bi, i, j: (bi, i, 0)),
                  pl.BlockSpec((1, d), lambda bi, i, j: (0, 0)),
                  vec, vec,
                  pl.BlockSpec((d, tn), lambda bi, i, j: (0, j))],
        out_specs=out_spec,
        out_shape=out_shape,
        scratch_shapes=[pltpu.VMEM((tm, d), BF16)],
        compiler_params=_cparams(("parallel", "parallel", "arbitrary"), 48),
        name=name,
    )(x, g, shift, scale, w)


def _outproj_kernel(a_ref, w_ref, x_ref, gate_ref, o_ref):
    o_ref[0] = x_ref[0] + gate_ref[0] * _dot(a_ref[0], w_ref[...])


def _outproj(a, w, x, gate, name="outproj"):
    b, r, k = a.shape
    d = w.shape[1]
    tm = min(r, 512)
    return pl.pallas_call(
        _outproj_kernel,
        grid=(b, r // tm),
        in_specs=[pl.BlockSpec((1, tm, k), lambda bi, i: (bi, i, 0)),
                  pl.BlockSpec((k, d), lambda bi, i: (0, 0)),
                  pl.BlockSpec((1, tm, d), lambda bi, i: (bi, i, 0)),
                  pl.BlockSpec((1, 1, d), lambda bi, i: (bi, 0, 0))],
        out_specs=pl.BlockSpec((1, tm, d), lambda bi, i: (bi, i, 0)),
        out_shape=jax.ShapeDtypeStruct((b, r, d), F32),
        compiler_params=_cparams(("parallel", "parallel"), 48),
        name=name,
    )(a, w, x, gate)


FFN_ROW_TILE = 1024
FFN_VMEM_MIB = 56


def _ffn_kernel(x_ref, g_ref, sh_ref, sc_ref, gate_ref, wg_ref, wu_ref, wo_ref, o_ref, f_ref):
    j = pl.program_id(2)
    last = pl.num_programs(2) - 1

    def partial_out(f):
        half = wg_ref.shape[1] // 2
        halves = (slice(0, half), slice(half, 2 * half))
        pre = [(_dot(f, wg_ref[:, cs]), _dot(f, wu_ref[:, cs])) for cs in halves]
        return [_dot((_silu(gt) * up).astype(BF16), wo_ref[cs, :]) for (gt, up), cs in zip(pre, halves)]

    @pl.when(j == 0)
    def _():
        f = _adaln(x_ref[0], g_ref[...], sh_ref[0], sc_ref[0]).astype(BF16)
        f_ref[...] = f
        pa, pb = partial_out(f)
        o_ref[0] = pa
        o_ref[0] += pb

    @pl.when((j > 0) & (j < last))
    def _():
        for part in partial_out(f_ref[...]):
            o_ref[0] += part

    @pl.when(j == last)
    def _():
        pa, pb = partial_out(f_ref[...])
        o_ref[0] += pa
        o_ref[0] = x_ref[0] + gate_ref[0] * (o_ref[0] + pb)


def _ffn(x, g, shift, scale, gate, w_in, w_out, name="ffn"):
    b, r, d = x.shape
    hidden = w_out.shape[0]
    tm = min(r, FFN_ROW_TILE)
    tf = 512
    nf = hidden // tf
    assert hidden % tf == 0 and nf >= 2
    vec = pl.BlockSpec((1, 1, d), lambda bi, i, j: (bi, 0, 0))
    return pl.pallas_call(
        _ffn_kernel,
        grid=(b, r // tm, nf),
        in_specs=[pl.BlockSpec((1, tm, d), lambda bi, i, j: (bi, i, 0)),
                  pl.BlockSpec((1, d), lambda bi, i, j: (0, 0)),
                  vec, vec, vec,
                  pl.BlockSpec((d, tf), lambda bi, i, j: (0, j)),
                  pl.BlockSpec((d, tf), lambda bi, i, j: (0, j + nf)),
                  pl.BlockSpec((tf, d), lambda bi, i, j: (j, 0))],
        out_specs=pl.BlockSpec((1, tm, d), lambda bi, i, j: (bi, i, 0)),
        out_shape=jax.ShapeDtypeStruct((b, r, d), F32),
        scratch_shapes=[pltpu.VMEM((tm, d), BF16)],
        compiler_params=_cparams(("parallel", "parallel", "arbitrary"), FFN_VMEM_MIB),
        name=name,
    )(x, g, shift, scale, gate, w_in, w_in, w_out)


def _softplus(x):
    return jnp.maximum(x, 0.0) + jnp.log1p(jnp.exp(-jnp.abs(x)))


def _gelu_tanh(x):
    return 0.5 * x * (1.0 + jnp.tanh(math.sqrt(2.0 / math.pi) * (x + 0.044715 * (x * x * x))))


def _rglru_tile(rec, prev8, next8, wg_ref, gb_ref, lam_ref, cw_ref, cb_ref,
                ubuf, a_s, b_s, h_s, hcar, *, reverse):
    tt = rec.shape[0]
    ubuf[0:8, :] = prev8
    ubuf[8:8 + tt, :] = rec
    ubuf[8 + tt:16 + tt, :] = next8
    base = 8 - CONV_PAD_LEFT
    u = cb_ref[...] + cw_ref[0:1, :] * ubuf[base:base + tt, :]
    for j in range(1, CONV_WIDTH):
        u = u + cw_ref[j:j + 1, :] * ubuf[base + j:base + j + tt, :]
    sp = _softplus(-lam_ref[...])
    for k in range(LRU_BLOCKS):
        cs = slice(k * LRU_BLOCK, (k + 1) * LRU_BLOCK)
        uk = u[:, cs]
        gts = _dot(uk.astype(BF16), wg_ref[k])
        r = _sigmoid(gts[:, :LRU_BLOCK] + gb_ref[0:1, cs])
        i = _sigmoid(gts[:, LRU_BLOCK:] + gb_ref[1:2, cs])
        log_a = (-LRU_C) * r * sp[:, cs]
        a = jnp.exp(log_a)
        a_s[:, cs] = a
        b_s[:, cs] = jnp.sqrt(-jnp.tanh(log_a) * (a * a + 1.0)) * (i * uk)

    def body(t, h):
        row = (tt - 1 - t) if reverse else t
        h = a_s[pl.ds(row, 1), :] * h + b_s[pl.ds(row, 1), :]
        h_s[pl.ds(row, 1), :] = h
        return h

    hcar[...] = lax.fori_loop(0, tt, body, hcar[...], unroll=8)


def _rglru_kernel(*refs, reverse, tt, nt):
    if reverse:
        (gl_ref, rl_ref, pv_ref, nx_ref, gc_ref, rc_ref, hfl_ref, hfc_ref,
         wg_ref, gb_ref, lam_ref, cw_ref, cb_ref, ol_ref, oc_ref, ubuf, a_s, b_s, h_s, hcar) = refs
    else:
        (rl_ref, pv_ref, nx_ref, rc_ref,
         wg_ref, gb_ref, lam_ref, cw_ref, cb_ref, ol_ref, oc_ref, ubuf, a_s, b_s, h_s, hcar) = refs
    s = pl.program_id(1)
    tile = functools.partial(_rglru_tile, wg_ref=wg_ref, gb_ref=gb_ref, lam_ref=lam_ref, cw_ref=cw_ref,
                             cb_ref=cb_ref, ubuf=ubuf, a_s=a_s, b_s=b_s, h_s=h_s, hcar=hcar, reverse=reverse)
    zeros8 = jnp.zeros((8, rl_ref.shape[2]), F32)

    @pl.when(s == 0)
    def _():
        hcar[...] = jnp.zeros_like(hcar)
        tile(rc_ref[0].astype(F32), zeros8, zeros8)
        if reverse:
            oc_ref[0] = (_gelu_tanh(gc_ref[0].astype(F32)) * (hfc_ref[0].astype(F32) + h_s[...])).astype(oc_ref.dtype)
        else:
            oc_ref[0] = h_s[...].astype(oc_ref.dtype)

    @pl.when(s > 0)
    def _():
        tl = (nt - s) if reverse else (s - 1)
        has_prev = (tl > 0).astype(F32)
        has_next = (tl < nt - 1).astype(F32)
        prev8 = pv_ref[0].astype(F32)[8:16, :] * has_prev
        next8 = nx_ref[0].astype(F32)[0:8, :] * has_next
        tile(rl_ref[0].astype(F32), prev8, next8)
        if reverse:
            ol_ref[0] = (_gelu_tanh(gl_ref[0].astype(F32)) * (hfl_ref[0].astype(F32) + h_s[...])).astype(ol_ref.dtype)
        else:
            ol_ref[0] = h_s[...].astype(ol_ref.dtype)


def _rglru_scan(gr_lat, gr_ctx, wg, gb, lam, cw, cb, hf_lat=None, hf_ctx=None):
    reverse = hf_lat is not None
    b, s_len, w2 = gr_lat.shape
    w = w2 // 2
    tt = gr_ctx.shape[1]
    assert s_len % tt == 0 and tt % 16 == 0
    nt = s_len // tt
    hb = tt // 16

    def lat_tile(si):
        return (nt - jnp.maximum(si, 1)) if reverse else jnp.maximum(si - 1, 0)

    lat_rows = lambda col: pl.BlockSpec((1, tt, w), lambda bi, si: (bi, lat_tile(si), col))
    ctx_rows = lambda col: pl.BlockSpec((1, tt, w), lambda bi, si: (bi, 0, col))
    prev_spec = pl.BlockSpec((1, 16, w), lambda bi, si: (bi, jnp.maximum(lat_tile(si) * hb - 1, 0), 1))
    next_spec = pl.BlockSpec((1, 16, w), lambda bi, si: (bi, jnp.minimum((lat_tile(si) + 1) * hb, nt * hb - 1), 1))
    full = lambda shape: pl.BlockSpec(shape, lambda bi, si: (0,) * len(shape))
    params = [wg, gb, lam, cw, cb]
    param_specs = [full(wg.shape), full(gb.shape), full(lam.shape), full(cw.shape), full(cb.shape)]
    if reverse:
        args = [gr_lat, gr_lat, gr_lat, gr_lat, gr_ctx, gr_ctx, hf_lat, hf_ctx] + params
        in_specs = [lat_rows(0), lat_rows(1), prev_spec, next_spec, ctx_rows(0), ctx_rows(1),
                    lat_rows(0), ctx_rows(0)] + param_specs
    else:
        args = [gr_lat, gr_lat, gr_lat, gr_ctx] + params
        in_specs = [lat_rows(1), prev_spec, next_spec, ctx_rows(1)] + param_specs
    return pl.pallas_call(
        functools.partial(_rglru_kernel, reverse=reverse, tt=tt, nt=nt),
        grid=(b, nt + 1),
        in_specs=in_specs,
        out_specs=[lat_rows(0), ctx_rows(0)],
        out_shape=[jax.ShapeDtypeStruct((b, s_len, w), BF16), jax.ShapeDtypeStruct((b, tt, w), BF16)],
        scratch_shapes=[pltpu.VMEM((tt + 16, w), F32), pltpu.VMEM((tt, w), F32), pltpu.VMEM((tt, w), F32),
                        pltpu.VMEM((tt, w), F32), pltpu.VMEM((1, w), F32)],
        compiler_params=_cparams(("parallel", "arbitrary"), 48),
        name="rglru_bwd" if reverse else "rglru_fwd",
    )(*args)


def _gla_mask(reverse):
    r_i = lax.broadcasted_iota(jnp.int32, (GLA_CHUNK, GLA_CHUNK), 0)
    c_i = lax.broadcasted_iota(jnp.int32, (GLA_CHUNK, GLA_CHUNK), 1)
    return (c_i >= r_i) if reverse else (c_i <= r_i)


def _hgrn_head(q, f, v, st_t, mask, reverse):
    rows = q.shape[0]
    c = GLA_CHUNK
    half = c // 2
    nchunk = rows // c
    chunks = [slice(n * c, (n + 1) * c) for n in range(nchunk)]
    tri = jnp.where(mask, 1.0, 0.0).astype(BF16)
    g = jnp.log(f)
    k = 1.0 - f
    hi = g.astype(BF16)
    r1 = g - hi.astype(F32)
    mid = r1.astype(BF16)
    lo = (r1 - mid.astype(F32)).astype(BF16)
    g3 = jnp.concatenate([hi, mid, lo], axis=1)
    c3s = [_dot(tri, g3[rs]) for rs in chunks]
    yield
    cums = [(c3[:, :LANES] + c3[:, LANES:2 * LANES]) + c3[:, 2 * LANES:] for c3 in c3s]
    if reverse:
        totals = [cm[0:1] for cm in cums]
        refs = [cm[half:half + 1] for cm in cums]
    else:
        totals = [cm[c - 1:c] for cm in cums]
        refs = [cm[half - 1:half] for cm in cums]
    cum = jnp.concatenate(cums, axis=0)
    ref_b = jnp.concatenate([jnp.broadcast_to(r, (c, LANES)) for r in refs], axis=0)
    e_q = jnp.exp(cum - ref_b)
    qt = q * e_q
    kt = k * (1.0 / e_q)
    qtb, ktb, vb = qt.astype(BF16), kt.astype(BF16), v.astype(BF16)
    raw, upds, decays = [], [], []
    for n, rs in enumerate(chunks):
        raw.append(_dot_nt(qtb[rs], ktb[rs]))
        kbar = (kt[rs] * jnp.exp(totals[n] - refs[n])).astype(BF16)
        upds.append(_dot_tn(vb[rs], kbar))
        decays.append(jnp.exp(totals[n]))
    yield
    outs = [_dot(jnp.where(mask, s, 0.0).astype(BF16), vb[rs]) for s, rs in zip(raw, chunks)]
    yield
    for n in (range(nchunk - 1, -1, -1) if reverse else range(nchunk)):
        qi = (qt[chunks[n]] * jnp.exp(refs[n])).astype(BF16)
        outs[n] = outs[n] + _dot_nt(qi, st_t.astype(BF16))
        st_t = st_t * decays[n] + upds[n]
    return jnp.concatenate(outs, axis=0), st_t


def _run_in_lockstep(generators):
    results = [None] * len(generators)
    active = list(range(len(generators)))
    while active:
        for i in list(active):
            try:
                next(generators[i])
            except StopIteration as done:
                results[i] = done.value
                active.remove(i)
    return results


HGRN_HEADS_PER_STEP = 16


def _hgrn_kernel(*refs, reverse, layer):
    if reverse:
        (ql, fl, il, gl, ofl, qc, fc, ic, gc, ofc, lg_ref, gn_ref, ol_ref, oc_ref, st_ref) = refs
    else:
        (ql, fl, il, qc, fc, ic, lg_ref, ol_ref, oc_ref, st_ref) = refs
    is_ctx = pl.program_id(2) == 0

    @pl.when(is_ctx)
    def _():
        st_ref[...] = jnp.zeros_like(st_ref)

    def pick(c_ref, l_ref, hh):
        return jnp.where(is_ctx, c_ref[0, hh], l_ref[0, hh]).astype(F32)

    mask = _gla_mask(reverse)
    heads = []
    for hh in range(HGRN_HEADS_PER_STEP):
        cs = slice(hh * HGRN_HEAD_DIM, (hh + 1) * HGRN_HEAD_DIM)
        lg = lg_ref[:, cs]
        e = jnp.exp(lg - jnp.max(lg, axis=0, keepdims=True))
        sm = e / jnp.sum(e, axis=0, keepdims=True)
        lb = jnp.zeros((1, HGRN_HEAD_DIM), F32)
        for l in range(1, layer + 1):
            lb = lb + sm[l:l + 1]
        q = _silu(pick(qc, ql, hh))
        f = lb + (1.0 - lb) * _sigmoid(pick(fc, fl, hh))
        heads.append(_hgrn_head(q, f, pick(ic, il, hh), st_ref[hh], mask, reverse))
    outs = []
    for hh, (o, st_new) in enumerate(_run_in_lockstep(heads)):
        st_ref[hh] = st_new
        if reverse:
            o = (_rms(pick(ofc, ofl, hh) + o) * gn_ref[...]) * _silu(pick(gc, gl, hh))
        outs.append(o.astype(ol_ref.dtype))

    def store(out_ref):
        for hh, ob in enumerate(outs):
            if reverse:
                out_ref[0, :, hh * HGRN_HEAD_DIM:(hh + 1) * HGRN_HEAD_DIM] = ob
            else:
                out_ref[0, hh] = ob

    @pl.when(is_ctx)
    def _():
        store(oc_ref)

    @pl.when(jnp.logical_not(is_ctx))
    def _():
        store(ol_ref)


def _hgrn_scan(p_lat, p_ctx, logits_d, layer, gn=None, of_lat=None, of_ctx=None):
    reverse = of_lat is not None
    b, _, s_len, hd = p_lat.shape
    h, hps = HGRN_HEADS, HGRN_HEADS_PER_STEP
    tt = p_ctx.shape[2]
    assert s_len % tt == 0 and tt % GLA_CHUNK == 0 and h % hps == 0
    nt = s_len // tt
    ng = h // hps
    fsel = 2 if reverse else 1

    def lat_tile(si):
        return (nt - jnp.maximum(si, 1)) if reverse else jnp.maximum(si - 1, 0)

    lat = lambda grp: pl.BlockSpec((1, hps, tt, hd), lambda bi, hi, si: (bi, grp * ng + hi, lat_tile(si), 0))
    ctx = lambda grp: pl.BlockSpec((1, hps, tt, hd), lambda bi, hi, si: (bi, grp * ng + hi, 0, 0))
    lg_spec = pl.BlockSpec((logits_d.shape[0], hps * hd), lambda bi, hi, si: (0, hi))
    if reverse:
        args = [p_lat, p_lat, p_lat, p_lat, of_lat, p_ctx, p_ctx, p_ctx, p_ctx, of_ctx, logits_d, gn]
        in_specs = [lat(0), lat(fsel), lat(3), lat(4), lat(0), ctx(0), ctx(fsel), ctx(3), ctx(4), ctx(0),
                    lg_spec, pl.BlockSpec((1, hd), lambda bi, hi, si: (0, 0))]
        out_specs = [pl.BlockSpec((1, tt, hps * hd), lambda bi, hi, si: (bi, lat_tile(si), hi)),
                     pl.BlockSpec((1, tt, hps * hd), lambda bi, hi, si: (bi, 0, hi))]
        out_shape = [jax.ShapeDtypeStruct((b, s_len, h * hd), BF16), jax.ShapeDtypeStruct((b, tt, h * hd), BF16)]
    else:
        args = [p_lat, p_lat, p_lat, p_ctx, p_ctx, p_ctx, logits_d]
        in_specs = [lat(0), lat(fsel), lat(3), ctx(0), ctx(fsel), ctx(3), lg_spec]
        out_specs = [lat(0), ctx(0)]
        out_shape = [jax.ShapeDtypeStruct((b, h, s_len, hd), BF16), jax.ShapeDtypeStruct((b, h, tt, hd), BF16)]
    return pl.pallas_call(
        functools.partial(_hgrn_kernel, reverse=reverse, layer=layer),
        grid=(b, ng, nt + 1),
        in_specs=in_specs,
        out_specs=out_specs,
        out_shape=out_shape,
        scratch_shapes=[pltpu.VMEM((hps, hd, hd), F32)],
        compiler_params=_cparams(("parallel", "parallel", "arbitrary"), 32),
        name="hgrn_bwd" if reverse else "hgrn_fwd",
    )(*args)


def _rope_tables(n_lat, rot_dim, reps):
    n_freq = rot_dim // 4
    t = jnp.arange(n_lat)
    inv = ROPE_THETA ** (-jnp.arange(n_freq, dtype=F32) / n_freq)
    ang_r = (t // GRID_W).astype(F32)[:, None] * inv[None, :]
    ang_c = (t % GRID_W).astype(F32)[:, None] * inv[None, :]
    cos = jnp.concatenate([jnp.cos(ang_r)] * 2 + [jnp.cos(ang_c)] * 2, axis=-1)
    sin = jnp.concatenate([-jnp.sin(ang_r), jnp.sin(ang_r), -jnp.sin(ang_c), jnp.sin(ang_c)], axis=-1)
    return jnp.concatenate([jnp.tile(cos, (1, reps)), jnp.tile(sin, (1, reps))], axis=-1)


def _rope(x, cos, sin, n_freq):
    width = x.shape[1]
    lane = lax.broadcasted_iota(jnp.int32, (1, width), 1)
    first = (lane % (2 * n_freq)) < n_freq
    partner = jnp.where(first, pltpu.roll(x, width - n_freq, 1), pltpu.roll(x, n_freq, 1))
    return x * cos + partner * sin


def _mla_qkv_kernel(*refs, use_rope):
    if use_rope:
        (c_ref, qg_ref, kvg_ref, wq_ref, wk_ref, wvt_ref, gqn_ref, gqr_ref, gkn_ref, gkr_ref, tab_ref,
         q_ref, k_ref, vt_ref) = refs
    else:
        (c_ref, qg_ref, kvg_ref, wq_ref, wk_ref, wvt_ref, gqn_ref, gqr_ref, gkn_ref, gkr_ref,
         q_ref, k_ref, vt_ref) = refs
    c = c_ref[0].astype(F32)
    cqn = (_rms(c[:, :MLA_Q_RANK]) * qg_ref[...]).astype(BF16)
    ckvn = (_rms(c[:, MLA_Q_RANK:MLA_Q_RANK + MLA_KV_RANK]) * kvg_ref[...]).astype(BF16)
    kr = c[:, MLA_Q_RANK + MLA_KV_RANK:]
    qa = _dot(cqn, wq_ref[...])
    ka = _dot(ckvn, wk_ref[...])
    vt = _dot_nt(wvt_ref[...], ckvn)
    lane = lax.broadcasted_iota(jnp.int32, (1, LANES), 1)
    low = lane < MLA_ROPE
    if use_rope:
        cos, sin = tab_ref[:, :LANES], tab_ref[:, LANES:]
    inv_rope = 1.0 / MLA_ROPE

    def halves_rms(x):
        sq = x * x
        ss_lo = jnp.sum(jnp.where(low, sq, 0.0), axis=-1, keepdims=True)
        ss_hi = jnp.sum(jnp.where(low, 0.0, sq), axis=-1, keepdims=True)
        return x * jnp.where(low, lax.rsqrt(ss_lo * inv_rope + NORM_EPS), lax.rsqrt(ss_hi * inv_rope + NORM_EPS))

    krn = halves_rms(kr) * gkr_ref[...]
    if use_rope:
        krn = _rope(krn, cos, sin, MLA_ROPE // 4)
    krn = jnp.where(low, krn, 0.0)
    nope_w = MLA_HEADS * MLA_NOPE
    qscale = MLA_SCALE * LOG2E
    for p in range(MLA_HEADS // 2):
        qr = halves_rms(qa[:, nope_w + p * LANES:nope_w + (p + 1) * LANES]) * gqr_ref[...]
        if use_rope:
            qr = _rope(qr, cos, sin, MLA_ROPE // 4)
        for e in range(2):
            h = 2 * p + e
            hs = slice(h * MLA_NOPE, (h + 1) * MLA_NOPE)
            qn = _rms(qa[:, hs]) * gqn_ref[...]
            rot = qr if e == 0 else pltpu.roll(qr, MLA_ROPE, 1)
            rot = jnp.where(low, rot, 0.0)
            q_ref[0, h] = (jnp.concatenate([qn, rot], axis=1) * qscale).astype(BF16)
            kn = _rms(ka[:, hs]) * gkn_ref[...]
            k_ref[0, h] = jnp.concatenate([kn, krn], axis=1).astype(BF16)
            vt_ref[0, h, 0, :MLA_V, :] = vt[hs, :].astype(BF16)
            vt_ref[0, h, 0, MLA_V:, :] = jnp.ones((MLA_V_ROWS - MLA_V, vt.shape[1]), BF16)


def _mla_qkv(c, qg, kvg, wq, wk, wvt, gqn, gqr, gkn, gkr, tab, tm):
    b, r, cw = c.shape
    assert r % tm == 0
    h = MLA_HEADS
    use_rope = tab is not None
    full = lambda a: pl.BlockSpec(a.shape, lambda bi, i: (0,) * a.ndim)
    args = [c, qg, kvg, wq, wk, wvt, gqn, gqr, gkn, gkr]
    in_specs = [pl.BlockSpec((1, tm, cw), lambda bi, i: (bi, i, 0))] + [full(a) for a in args[1:]]
    if use_rope:
        args.append(tab)
        in_specs.append(pl.BlockSpec((tm, tab.shape[1]), lambda bi, i: (i, 0)))
    return pl.pallas_call(
        functools.partial(_mla_qkv_kernel, use_rope=use_rope),
        grid=(b, r // tm),
        in_specs=in_specs,
        out_specs=[pl.BlockSpec((1, h, tm, MLA_QK_PAD), lambda bi, i: (bi, 0, i, 0)),
                   pl.BlockSpec((1, h, tm, MLA_QK_PAD), lambda bi, i: (bi, 0, i, 0)),
                   pl.BlockSpec((1, h, 1, MLA_V_ROWS, tm), lambda bi, i: (bi, 0, i, 0, 0))],
        out_shape=[jax.ShapeDtypeStruct((b, h, r, MLA_QK_PAD), BF16),
                   jax.ShapeDtypeStruct((b, h, r, MLA_QK_PAD), BF16),
                   jax.ShapeDtypeStruct((b, h, r // tm, MLA_V_ROWS, tm), BF16)],
        compiler_params=_cparams(("parallel", "parallel"), 48),
        name="mla_qkv_rope" if use_rope else "mla_qkv",
    )(*args)


MLA_KV_TILES_PER_TRIP = 32
MLA_ONLINE_TILES_PER_TRIP = 2
MLA_Q_CHUNK = 512
MLA_Q_CHUNKS_PER_STEP = 2


def _attn_update(s, vt, m, acc):
    m_new = jnp.maximum(m, jnp.max(s, axis=0, keepdims=True))
    alpha = jnp.exp2(m - m_new)
    p = jnp.exp2(s - m_new).astype(BF16)
    return m_new, alpha * acc + _dot(vt, p)


MLA_STALE_MARGIN = 100.0
MLA_STALE_BOUND = (126.0 + MLA_STALE_MARGIN) / 2.0


def _attn_update_stale(s, vt, m, acc):
    p = jnp.exp2(s - m).astype(BF16)
    m_new = jnp.maximum(m, jnp.max(s, axis=0, keepdims=True))
    return m_new, (acc + _dot(vt, p)) * jnp.exp2(m - m_new)


def _mla_attn_stale(qs, kc_ref, vc_ref, kl_ref, vl_ref, floor):
    nk, tk = vl_ref.shape[2], vl_ref.shape[4]
    unroll = min(MLA_KV_TILES_PER_TRIP, nk)
    ms, accs = [], []
    for q in qs:
        m = jnp.full((1, q.shape[0]), floor, F32)
        acc = jnp.zeros((vc_ref.shape[3], q.shape[0]), F32)
        m, acc = _attn_update(_dot_nt(kc_ref[0, 0], q), vc_ref[0, 0, 0], m, acc)
        ms.append(m)
        accs.append(acc)

    def body(i, carry):
        ms, accs = list(carry[0]), list(carry[1])
        for u in range(unroll):
            t = unroll * i + u
            k = kl_ref[0, 0, pl.ds(pl.multiple_of(t * tk, tk), tk), :]
            vt = vl_ref[0, 0, t]
            ss = [_dot_nt(k, q) for q in qs]
            for c in range(len(qs)):
                ms[c], accs[c] = _attn_update_stale(ss[c], vt, ms[c], accs[c])
        return tuple(ms), tuple(accs)

    return lax.fori_loop(0, nk // unroll, body, (tuple(ms), tuple(accs)))[1]


def _mla_attn_online(q, kc_ref, vc_ref, kl_ref, vl_ref):
    nk, tk = vl_ref.shape[2], vl_ref.shape[4]
    unroll = min(MLA_ONLINE_TILES_PER_TRIP, nk)
    m = jnp.full((1, q.shape[0]), -jnp.inf, F32)
    acc = jnp.zeros((vc_ref.shape[3], q.shape[0]), F32)

    def scores(j):
        return _dot_nt(kl_ref[0, 0, pl.ds(pl.multiple_of(j * tk, tk), tk), :], q)

    s_cur = scores(0)
    m, acc = _attn_update(_dot_nt(kc_ref[0, 0], q), vc_ref[0, 0, 0], m, acc)

    def body(i, carry):
        s_cur, m, acc = carry
        for u in range(unroll):
            t = unroll * i + u
            s_next = scores(t + 1)
            m, acc = _attn_update(s_cur, vl_ref[0, 0, t], m, acc)
            s_cur = s_next
        return s_cur, m, acc

    s_cur, m, acc = lax.fori_loop(0, nk // unroll - 1, body, (s_cur, m, acc))
    for t in range(nk - unroll, nk):
        s_next = scores(t + 1) if t + 1 < nk else None
        m, acc = _attn_update(s_cur, vl_ref[0, 0, t], m, acc)
        s_cur = s_next
    return acc


def _mla_attn_kernel(*refs, with_lat, chunk):
    if with_lat:
        bound_ref, q_ref, kc_ref, vc_ref, kl_ref, vl_ref, o_ref = refs
    else:
        q_ref, kc_ref, vc_ref, o_ref = refs
    tq = q_ref.shape[2]
    rows = [slice(c * chunk, (c + 1) * chunk) for c in range(tq // chunk)]

    def finish(acc, rs):
        o_ref[0, rs, :] = (acc[:MLA_V] / acc[MLA_V:MLA_V + 1]).T.astype(o_ref.dtype)

    if with_lat:
        bound = bound_ref[0]

        @pl.when(bound < MLA_STALE_BOUND)
        def _():
            accs = _mla_attn_stale([q_ref[0, 0, rs, :] for rs in rows], kc_ref, vc_ref, kl_ref, vl_ref,
                                   bound - MLA_STALE_MARGIN)
            for acc, rs in zip(accs, rows):
                finish(acc, rs)

        @pl.when(bound >= MLA_STALE_BOUND)
        def _():
            for rs in rows:
                finish(_mla_attn_online(q_ref[0, 0, rs, :], kc_ref, vc_ref, kl_ref, vl_ref), rs)
    else:
        for rs in rows:
            q = q_ref[0, 0, rs, :]
            m = jnp.full((1, chunk), -jnp.inf, F32)
            acc = jnp.zeros((vc_ref.shape[3], chunk), F32)
            finish(_attn_update(_dot_nt(kc_ref[0, 0], q), vc_ref[0, 0, 0], m, acc)[1], rs)


def _mla_attention(q, k_ctx, vt_ctx, k_lat=None, vt_lat=None, bound=None):
    b, h, n, dq = q.shape
    with_lat = k_lat is not None
    chunk = min(n, MLA_Q_CHUNK)
    tq = min(n, MLA_Q_CHUNKS_PER_STEP * chunk)
    assert n % tq == 0
    rc = k_ctx.shape[2]
    args = [q, k_ctx, vt_ctx]
    in_specs = [pl.BlockSpec((1, 1, tq, dq), lambda bi, hi, i: (bi, hi, i, 0)),
                pl.BlockSpec((1, 1, rc, dq), lambda bi, hi, i: (bi, hi, 0, 0)),
                pl.BlockSpec((1, 1, 1, MLA_V_ROWS, rc), lambda bi, hi, i: (bi, hi, 0, 0, 0))]
    if with_lat:
        nk, tk = vt_lat.shape[2], vt_lat.shape[4]
        assert nk % min(MLA_KV_TILES_PER_TRIP, nk) == 0 and nk % min(MLA_ONLINE_TILES_PER_TRIP, nk) == 0
        args = [bound] + args + [k_lat, vt_lat]
        in_specs = [pl.BlockSpec(memory_space=pltpu.SMEM)] + in_specs
        in_specs += [pl.BlockSpec((1, 1, nk * tk, dq), lambda bi, hi, i: (bi, hi, 0, 0)),
                     pl.BlockSpec((1, 1, nk, MLA_V_ROWS, tk), lambda bi, hi, i: (bi, hi, 0, 0, 0))]
    return pl.pallas_call(
        functools.partial(_mla_attn_kernel, with_lat=with_lat, chunk=chunk),
        grid=(b, h, n // tq),
        in_specs=in_specs,
        out_specs=pl.BlockSpec((1, tq, MLA_V), lambda bi, hi, i: (bi, i, hi)),
        out_shape=jax.ShapeDtypeStruct((b, n, h * MLA_V), BF16),
        compiler_params=_cparams(("parallel", "parallel", "arbitrary"), 48),
        name="mla_attn" if with_lat else "mla_attn_ctx",
    )(*args)


def _swa_inproj_kernel(*refs, use_rope, n_norm_tiles):
    if use_rope:
        x_ref, g_ref, sh_ref, sc_ref, w_ref, hg_ref, tab_ref, o_ref, h_ref = refs
    else:
        x_ref, g_ref, sh_ref, sc_ref, w_ref, hg_ref, o_ref, h_ref = refs
    j = pl.program_id(2)

    @pl.when(j == 0)
    def _():
        h_ref[...] = _adaln(x_ref[0], g_ref[...], sh_ref[0], sc_ref[0]).astype(BF16)

    res = _dot(h_ref[...], w_ref[...])

    @pl.when(j < n_norm_tiles)
    def _():
        for e in range(res.shape[1] // SWA_HEAD_DIM):
            cs = slice(e * SWA_HEAD_DIM, (e + 1) * SWA_HEAD_DIM)
            xh = _rms(res[:, cs]) * hg_ref[:, cs]
            if use_rope:
                xh = _rope(xh, tab_ref[:, :SWA_HEAD_DIM], tab_ref[:, SWA_HEAD_DIM:], SWA_HEAD_DIM // 4)
            o_ref[0, :, cs] = xh.astype(o_ref.dtype)

    @pl.when(j >= n_norm_tiles)
    def _():
        o_ref[0] = res.astype(o_ref.dtype)


def _swa_inproj(x, g, shift, scale, w, head_gain, tab):
    b, r, d = x.shape
    n = w.shape[1]
    tm = min(r, 512)
    tn = 512
    use_rope = tab is not None
    n_norm_tiles = (SWA_Q_HEADS + SWA_KV_HEADS) * SWA_HEAD_DIM // tn
    vec = pl.BlockSpec((1, 1, d), lambda bi, i, j: (bi, 0, 0))
    args = [x, g, shift, scale, w, head_gain]
    in_specs = [pl.BlockSpec((1, tm, d), lambda bi, i, j: (bi, i, 0)),
                pl.BlockSpec((1, d), lambda bi, i, j: (0, 0)),
                vec, vec,
                pl.BlockSpec((d, tn), lambda bi, i, j: (0, j)),
                pl.BlockSpec((1, tn), lambda bi, i, j: (0, j))]
    if use_rope:
        args.append(tab)
        in_specs.append(pl.BlockSpec((tm, tab.shape[1]), lambda bi, i, j: (i, 0)))
    return pl.pallas_call(
        functools.partial(_swa_inproj_kernel, use_rope=use_rope, n_norm_tiles=n_norm_tiles),
        grid=(b, r // tm, n // tn),
        in_specs=in_specs,
        out_specs=pl.BlockSpec((1, tm, tn), lambda bi, i, j: (bi, i, j)),
        out_shape=jax.ShapeDtypeStruct((b, r, n), BF16),
        scratch_shapes=[pltpu.VMEM((tm, d), BF16)],
        compiler_params=_cparams(("parallel", "parallel", "arbitrary"), 48),
        name="swa_inproj_rope" if use_rope else "swa_inproj",
    )(*args)


def _swa_block(q, keys, vals, sinks, nctx, prev_ok, next_ok):
    blk, grp = SWA_BLOCK, SWA_GROUP
    qs = jnp.concatenate([q[:, e * SWA_HEAD_DIM:(e + 1) * SWA_HEAD_DIM] for e in range(grp)], axis=0)
    s = _dot_nt(qs, jnp.concatenate(keys, axis=0))
    yield
    rows = s.shape[0]
    r_i = lax.broadcasted_iota(jnp.int32, (rows, blk), 0) % blk
    c_i = lax.broadcasted_iota(jnp.int32, (rows, blk), 1)
    off_prev = jnp.where(prev_ok, 0, 2 * blk)
    off_next = jnp.where(next_ok, 0, 2 * blk)
    s_prev = jnp.where(c_i >= r_i + off_prev, s[:, nctx:nctx + blk], MASK_VALUE)
    s_next = jnp.where(c_i + off_next <= r_i, s[:, nctx + 2 * blk:], MASK_VALUE)
    s = jnp.concatenate([s[:, :nctx], s_prev, s[:, nctx + blk:nctx + 2 * blk], s_next], axis=1)
    row1 = lax.broadcasted_iota(jnp.int32, (rows, 1), 0)
    sink = jnp.zeros((rows, 1), F32)
    for e in range(grp):
        sink = jnp.where(row1 // blk == e, sinks[e], sink)
    m = jnp.maximum(jnp.max(s, axis=1, keepdims=True), sink)
    p = jnp.exp2(s - m)
    l = jnp.sum(p, axis=1, keepdims=True) + jnp.exp2(sink - m)
    o = _dot(p.astype(BF16), jnp.concatenate(vals, axis=0)) / l
    return jnp.concatenate([o[e * blk:(e + 1) * blk] for e in range(grp)], axis=1)


def _swa_attn_kernel(sink_ref, q_ref, kp_ref, kc_ref, kn_ref, vp_ref, vc_ref, vn_ref, kx_ref, vx_ref, o_ref):
    hk = pl.program_id(1)
    i = pl.program_id(2)
    last = pl.num_programs(2) - 1
    blk = SWA_BLOCK
    nctx = kx_ref.shape[1]
    sinks = [sink_ref[hk * SWA_GROUP + e] * LOG2E for e in range(SWA_GROUP)]
    k0, k1 = kc_ref[0, :blk, :], kc_ref[0, blk:, :]
    v0, v1 = vc_ref[0, :blk, :], vc_ref[0, blk:, :]
    o_a, o_b = _run_in_lockstep([
        _swa_block(q_ref[0, :blk, :], [kx_ref[0], kp_ref[0], k0, k1], [vx_ref[0], vp_ref[0], v0, v1],
                   sinks, nctx, i > 0, True),
        _swa_block(q_ref[0, blk:, :], [kx_ref[0], k0, k1, kn_ref[0]], [vx_ref[0], v0, v1, vn_ref[0]],
                   sinks, nctx, True, i < last)])
    o_ref[0, :blk, :] = o_a.astype(o_ref.dtype)
    o_ref[0, blk:, :] = o_b.astype(o_ref.dtype)


def _swa_attention(qkv_lat, qkv_ctx, sink):
    b, n, _ = qkv_lat.shape
    nctx = qkv_ctx.shape[1]
    blk, dh, grp = SWA_BLOCK, SWA_HEAD_DIM, SWA_GROUP
    nb = n // blk
    assert nb % 2 == 0
    kcol = SWA_Q_HEADS
    vcol = SWA_Q_HEADS + SWA_KV_HEADS
    prev = lambda col0: pl.BlockSpec((1, blk, dh), lambda bi, hk, i: (bi, jnp.maximum(2 * i - 1, 0), col0 + hk))
    nxt = lambda col0: pl.BlockSpec((1, blk, dh), lambda bi, hk, i: (bi, jnp.minimum(2 * i + 2, nb - 1), col0 + hk))
    pair = lambda col0: pl.BlockSpec((1, 2 * blk, dh), lambda bi, hk, i: (bi, i, col0 + hk))
    ctx = lambda col0: pl.BlockSpec((1, nctx, dh), lambda bi, hk, i: (bi, 0, col0 + hk))
    return pl.pallas_call(
        _swa_attn_kernel,
        grid=(b, SWA_KV_HEADS, nb // 2),
        in_specs=[pl.BlockSpec(memory_space=pltpu.SMEM),
                  pl.BlockSpec((1, 2 * blk, grp * dh), lambda bi, hk, i: (bi, i, hk)),
                  prev(kcol), pair(kcol), nxt(kcol),
                  prev(vcol), pair(vcol), nxt(vcol),
                  ctx(kcol), ctx(vcol)],
        out_specs=pl.BlockSpec((1, 2 * blk, grp * dh), lambda bi, hk, i: (bi, i, hk)),
        out_shape=jax.ShapeDtypeStruct((b, n, SWA_Q_HEADS * dh), BF16),
        compiler_params=_cparams(("parallel", "parallel", "arbitrary"), 32),
        name="swa_attn",
    )(sink, qkv_lat, qkv_lat, qkv_lat, qkv_lat, qkv_lat, qkv_lat, qkv_lat, qkv_ctx, qkv_ctx)


def kernel(x, c, ctx, c_ctx, mod_w, mod_b, norm_g, ffn_w_in, ffn_w_out, rglru_w_in, rglru_conv_w, rglru_conv_b, rglru_gate_w, rglru_gate_b, rglru_lambda, rglru_w_out, hgrn_w_in, hgrn_lb_logits, hgrn_gnorm_g, hgrn_w_out, mla_w_in, mla_q_norm_g, mla_kv_norm_g, mla_w_uq, mla_w_ukv, mla_qk_g, mla_w_out, swa_w_in, swa_qk_g, swa_sink, swa_w_out):
    b, n_lat, d = x.shape
    depth = mod_w.shape[0]
    assert depth == DEPTH and b + 1 <= 8
    bf = lambda a: a.astype(BF16)

    cond = jnp.concatenate([c, c_ctx[None, :], jnp.zeros((8 - b - 1, d), F32)], axis=0)
    mod = _modulation(cond, mod_w, mod_b).reshape(depth, 8, 6, d)

    x_lat, x_ctx = x, ctx
    for layer in range(depth):
        kind = layer % 4
        need_ctx = layer < depth - 1
        m_lat = [mod[layer, :b, k][:, None, :] for k in range(6)]
        m_ctx = [jnp.broadcast_to(mod[layer, b:b + 1, k][:, None, :], (b, 1, d)) for k in range(6)]
        g1 = norm_g[layer, 0][None, :]
        g2 = norm_g[layer, 1][None, :]
        y_ctx = None
        if kind == 0:
            w_in = bf(rglru_w_in[0])
            gr_lat = _inproj(x_lat, g1, m_lat[0], m_lat[1], w_in, tn=2048,name="rglru_inproj")
            gr_ctx = _inproj(x_ctx, g1, m_ctx[0], m_ctx[1], w_in, tn=2048,name="rglru_inproj_ctx")
            gw = rglru_gate_w[0]
            wg = [bf(jnp.concatenate([gw[dr, 0], gw[dr, 1]], axis=-1)) for dr in range(2)]
            cw, cb = rglru_conv_w[0], rglru_conv_b[0][None, :]
            scan = lambda dr, **kw: _rglru_scan(gr_lat, gr_ctx, wg[dr], rglru_gate_b[0, dr],
                                                rglru_lambda[0, dr][None, :], cw, cb, **kw)
            hf_lat, hf_ctx = scan(0)
            y_lat, y_ctx = scan(1, hf_lat=hf_lat, hf_ctx=hf_ctx)
            w_out = bf(rglru_w_out[0])
        elif kind == 1:
            w_in = bf(hgrn_w_in[0])
            p_lat = _inproj(x_lat, g1, m_lat[0], m_lat[1], w_in, tn=2048,head_major=True, name="hgrn_inproj")
            p_ctx = _inproj(x_ctx, g1, m_ctx[0], m_ctx[1], w_in, tn=2048,head_major=True, name="hgrn_inproj_ctx")
            of_lat, of_ctx = _hgrn_scan(p_lat, p_ctx, hgrn_lb_logits[:, 0, :], layer)
            y_lat, y_ctx = _hgrn_scan(p_lat, p_ctx, hgrn_lb_logits[:, 1, :], layer, gn=hgrn_gnorm_g[0][None, :],
                                      of_lat=of_lat, of_ctx=of_ctx)
            w_out = bf(hgrn_w_out[0])
        elif kind == 2:
            cw_real = mla_w_in.shape[2]
            cw_pad = -(-cw_real // LANES) * LANES
            w_in = bf(jnp.pad(mla_w_in[0], ((0, 0), (0, cw_pad - cw_real))))
            c_lat = _inproj(x_lat, g1, m_lat[0], m_lat[1], w_in, tn=cw_pad, name="mla_inproj")
            c_ctx_ = _inproj(x_ctx, g1, m_ctx[0], m_ctx[1], w_in, tn=cw_pad, name="mla_inproj_ctx")
            wq3 = mla_w_uq[0].reshape(MLA_Q_RANK, MLA_HEADS, MLA_NOPE + MLA_ROPE)
            wq = bf(jnp.concatenate([wq3[:, :, :MLA_NOPE].reshape(MLA_Q_RANK, -1),
                                     wq3[:, :, MLA_NOPE:].reshape(MLA_Q_RANK, -1)], axis=1))
            wkv3 = mla_w_ukv[0].reshape(MLA_KV_RANK, MLA_HEADS, MLA_NOPE + MLA_V)
            wk = bf(wkv3[:, :, :MLA_NOPE].reshape(MLA_KV_RANK, -1))
            wvt = bf(wkv3[:, :, MLA_NOPE:].reshape(MLA_KV_RANK, -1).T)
            qk_g = mla_qk_g[0]
            gqn, gkn = qk_g[0:1, :MLA_NOPE], qk_g[1:2, :MLA_NOPE]
            gqr = jnp.tile(qk_g[0:1, MLA_NOPE:], (1, LANES // MLA_ROPE))
            gkr = jnp.pad(qk_g[1:2, MLA_NOPE:], ((0, 0), (0, LANES - MLA_ROPE)))
            tab = _rope_tables(n_lat, MLA_ROPE, LANES // MLA_ROPE)
            small = (mla_q_norm_g[0][None, :], mla_kv_norm_g[0][None, :], wq, wk, wvt, gqn, gqr, gkn, gkr)
            q_l, k_l, vt_l = _mla_qkv(c_lat, *small, tab, min(n_lat, 512))
            q_c, k_c, vt_c = _mla_qkv(c_ctx_, *small, None, c_ctx_.shape[1])
            gmax2 = lambda g: jnp.max(g * g)
            qk_norm2 = ((MLA_NOPE * gmax2(gqn) + MLA_ROPE * gmax2(gqr))
                        * (MLA_NOPE * gmax2(gkn) + MLA_ROPE * gmax2(gkr)))
            bound = (1.05 * MLA_SCALE * LOG2E) * jnp.sqrt(qk_norm2).reshape(1)
            y_lat = _mla_attention(q_l, k_c, vt_c, k_l, vt_l, bound)
            if need_ctx:
                y_ctx = _mla_attention(q_c, k_c, vt_c)
            w_out = bf(mla_w_out[0])
        else:
            w_in = bf(swa_w_in[0])
            gq = jnp.tile(swa_qk_g[0, 0] * (SWA_SCALE * LOG2E), SWA_Q_HEADS)
            gk = jnp.tile(swa_qk_g[0, 1], SWA_KV_HEADS)
            head_gain = jnp.concatenate([gq, gk, jnp.ones((SWA_KV_HEADS * SWA_HEAD_DIM,), F32)])[None, :]
            tab = _rope_tables(n_lat, SWA_HEAD_DIM, 1)
            qkv_lat = _swa_inproj(x_lat, g1, m_lat[0], m_lat[1], w_in, head_gain, tab)
            qkv_ctx = _swa_inproj(x_ctx, g1, m_ctx[0], m_ctx[1], w_in, head_gain, None)
            y_lat = _swa_attention(qkv_lat, qkv_ctx, swa_sink[0])
            assert not need_ctx
            w_out = bf(swa_w_out[0])

        w_ffn_in, w_ffn_out = bf(ffn_w_in[layer]), bf(ffn_w_out[layer])
        x_lat = _outproj(y_lat, w_out, x_lat, m_lat[2])
        x_lat = _ffn(x_lat, g2, m_lat[3], m_lat[4], m_lat[5], w_ffn_in, w_ffn_out)
        if need_ctx:
            x_ctx = _outproj(y_ctx, w_out, x_ctx, m_ctx[2], name="outproj_ctx")
            x_ctx = _ffn(x_ctx, g2, m_ctx[3], m_ctx[4], m_ctx[5], w_ffn_in, w_ffn_out, name="ffn_ctx")
    return x_lat
```
